```python
import math
import jax
import jax.numpy as jnp
from jax import lax
import numpy as np

D_MODEL = 1024
BATCH = 2
SEQ = 8192
DEPTH = 2

GRID_W = 64
CTX_LEN = 256
N_HEADS_A = 4
HEAD_DIM_A = 64
V_DIM_A = 2 * HEAD_DIM_A
ATTN_WIDTH = N_HEADS_A * V_DIM_A
SSM_WIDTH = D_MODEL // 2
SSM_GROUP = 16
SSM_GROUPS = SSM_WIDTH // SSM_GROUP
SSM_STATE = 64
D_FF = 4 * D_MODEL
Q_COLS = N_HEADS_A * 2 * HEAD_DIM_A
K_COLS = N_HEADS_A * 2 * HEAD_DIM_A
V_COLS = ATTN_WIDTH
U_COLS = SSM_WIDTH
G_COLS = 2 * D_MODEL
IN_SPLITS = (Q_COLS, Q_COLS + K_COLS, Q_COLS + K_COLS + V_COLS, Q_COLS + K_COLS + V_COLS + U_COLS)
IN_COLS = Q_COLS + K_COLS + V_COLS + U_COLS + G_COLS
Q_BLOCK = 128
ROPE_BASE = 10000.0
ROPE_AXIS_DIM = HEAD_DIM_A // 2
N_FREQ = ROPE_AXIS_DIM // 2
NORM_EPS = 1e-6

kernel_name = "hybrid_diffattn_s5_dit_trunk"


def rmsnorm(x, g):
    x32 = x.astype(jnp.float32)
    y = x32 * lax.rsqrt(jnp.mean(x32 * x32, axis=-1, keepdims=True) + NORM_EPS)
    return (y * g.astype(jnp.float32)).astype(x.dtype)


def axial_rope_tables(n_tokens):
    rows = n_tokens // GRID_W
    row = jnp.repeat(jnp.arange(rows, dtype=jnp.int32), GRID_W)
    col = jnp.tile(jnp.arange(GRID_W, dtype=jnp.int32), rows)
    inv_freq = ROPE_BASE ** (-jnp.arange(N_FREQ, dtype=jnp.float32) / N_FREQ)
    ang = jnp.stack([row.astype(jnp.float32)[:, None] * inv_freq,
                     col.astype(jnp.float32)[:, None] * inv_freq], axis=1)
    return jnp.cos(ang), jnp.sin(ang)


def apply_axial_rope(t, cos, sin):
    xs = t.reshape(t.shape[:-1] + (2, 2, N_FREQ))
    x1, x2 = xs[..., 0, :], xs[..., 1, :]
    cs = cos[None, :, None, None].astype(t.dtype)
    sn = sin[None, :, None, None].astype(t.dtype)
    out = jnp.stack([x1 * cs - x2 * sn, x2 * cs + x1 * sn], axis=-2)
    return out.reshape(t.shape)


def diff_attend(q, k, v, lam):
    s = jnp.einsum('bqhcd,bkhcd->bhcqk', q, k).astype(jnp.float32) * (HEAD_DIM_A ** -0.5)
    p = jax.nn.softmax(s, axis=-1)
    pd = p[:, :, 0] - lam * p[:, :, 1]
    return jnp.einsum('bhqk,bkhe->bqhe', pd.astype(v.dtype), v)


def latent_diff_attention(q, k_all, v_all, lam):
    b, l = q.shape[:2]
    nb = l // Q_BLOCK
    qb = q.reshape((b, nb, Q_BLOCK) + q.shape[2:]).swapaxes(0, 1)
    ob = lax.map(lambda qi: diff_attend(qi, k_all, v_all, lam), qb)
    return ob.swapaxes(0, 1).reshape(b, l, N_HEADS_A, V_DIM_A)


def s5_discretize(a_re, a_im, b_re, b_im, log_dt):
    lam = lax.complex(a_re.astype(jnp.float32), a_im.astype(jnp.float32))
    dt = jnp.exp(log_dt.astype(jnp.float32))[:, None]
    lam_bar = jnp.exp(lam * dt)
    bmat = lax.complex(b_re.astype(jnp.float32), b_im.astype(jnp.float32))
    b_bar = ((lam_bar - 1.0) / lam)[..., None] * bmat
    return lam_bar, b_bar


def _ssm_combine(e_i, e_j):
    a_i, b_i = e_i
    a_j, b_j = e_j
    return a_j * a_i, a_j * b_i + b_j


def s5_scan(u4, lam_bar, b_bar, h0, reverse):
    bu = jnp.einsum('gph,blgh->blgp', b_bar, u4.astype(jnp.complex64))
    edge = -1 if reverse else 0
    bu = bu.at[:, edge].add(lam_bar * h0)
    a = jnp.broadcast_to(lam_bar, bu.shape)
    _, xs = lax.associative_scan(_ssm_combine, (a, bu), axis=1, reverse=reverse)
    return xs


def s5_readout(cmat, xs):
    return jnp.real(jnp.einsum('ghp,blgp->blgh', cmat, xs))


def s5_bidirectional(u, uc, a_re, a_im, b_re, b_im, c_re, c_im, log_dt, ssm_d, ctx_out):
    b, l, _ = u.shape
    lc = uc.shape[1]
    d32 = ssm_d.astype(jnp.float32)
    u32 = u.astype(jnp.float32)
    uc32 = uc.astype(jnp.float32)
    u4 = u32.reshape(b, l, SSM_GROUPS, SSM_GROUP)
    uc4 = uc32.reshape(b, lc, SSM_GROUPS, SSM_GROUP)
    y = u32 * d32
    yc = uc32 * d32 if ctx_out else None
    for direction, rev in ((0, False), (1, True)):
        lam_bar, b_bar = s5_discretize(a_re[direction], a_im[direction], b_re[direction],
                                       b_im[direction], log_dt[direction])
        cmat = lax.complex(c_re[direction].astype(jnp.float32), c_im[direction].astype(jnp.float32))
        h0 = jnp.zeros((b, SSM_GROUPS, SSM_STATE), jnp.complex64)
        xs_c = s5_scan(uc4, lam_bar, b_bar, h0, rev)
        h_ctx = xs_c[:, 0] if rev else xs_c[:, -1]
        xs_l = s5_scan(u4, lam_bar, b_bar, h_ctx, rev)
        y = y + s5_readout(cmat, xs_l).reshape(b, l, SSM_WIDTH)
        if ctx_out:
            yc = yc + s5_readout(cmat, xs_c).reshape(b, lc, SSM_WIDTH)
    return y.astype(u.dtype), (yc.astype(uc.dtype) if ctx_out else None)


def ssm_glu(y, w_glu, b_glu):
    y = jax.nn.gelu(y)
    return y * jax.nn.sigmoid(y @ w_glu + b_glu)


def sq_relu_mlp(h, w1, b1, w2, b2):
    return jnp.square(jax.nn.relu(h @ w1 + b1)) @ w2 + b2


def trunk_layer(x, xc, c, c_ctx, cos, sin, lam_init, ctx_out,
                ada_w, ada_b, norm_g, w_in, gate_b, lam_qk, subln_g, w_br_a,
                a_re, a_im, b_re, b_im, c_re, c_im, log_dt, ssm_d,
                w_glu, b_glu, w_br_s, w_out, w_mlp1, b_mlp1, w_mlp2, b_mlp2):
    b, l, _ = x.shape
    lc = xc.shape[1]
    sh1, sc1, gt1, sh2, sc2, gt2 = [m[:, None, :] for m in
                                    jnp.split(jax.nn.silu(c) @ ada_w + ada_b, 6, axis=-1)]
    csh1, csc1, cgt1, csh2, csc2, cgt2 = jnp.split(jax.nn.silu(c_ctx) @ ada_w + ada_b, 6, axis=-1)

    h = rmsnorm(x, norm_g[0]) * (1 + sc1) + sh1
    hc = rmsnorm(xc, norm_g[0]) * (1 + csc1) + csh1
    q, k, v, u, g = jnp.split(h @ w_in, IN_SPLITS, axis=-1)
    if ctx_out:
        qc, kc, vc, uc, gc = jnp.split(hc @ w_in, IN_SPLITS, axis=-1)
    else:
        kc, vc, uc = jnp.split(hc @ w_in[:, Q_COLS:IN_SPLITS[3]], (K_COLS, K_COLS + V_COLS), axis=-1)

    lam32 = lam_qk.astype(jnp.float32)
    lam = jnp.exp(jnp.sum(lam32[0] * lam32[1])) - jnp.exp(jnp.sum(lam32[2] * lam32[3])) + lam_init
    qk_shape = (N_HEADS_A, 2, HEAD_DIM_A)
    q = apply_axial_rope(q.reshape((b, l) + qk_shape), cos, sin)
    k = apply_axial_rope(k.reshape((b, l) + qk_shape), cos, sin)
    v = v.reshape(b, l, N_HEADS_A, V_DIM_A)
    kc = kc.reshape((b, lc) + qk_shape)
    vc = vc.reshape(b, lc, N_HEADS_A, V_DIM_A)
    k_all = jnp.concatenate([kc, k], axis=1)
    v_all = jnp.concatenate([vc, v], axis=1)
    o = latent_diff_attention(q, k_all, v_all, lam)
    pa = (rmsnorm(o, subln_g) * (1.0 - lam_init)).reshape(b, l, ATTN_WIDTH) @ w_br_a

    y_lat, y_ctx = s5_bidirectional(u, uc, a_re, a_im, b_re, b_im, c_re, c_im, log_dt, ssm_d, ctx_out)
    ps = ssm_glu(y_lat, w_glu, b_glu) @ w_br_s

    ga, gs = jnp.split(jax.nn.sigmoid(g + gate_b), 2, axis=-1)
    mix = (ga * pa + gs * ps) @ w_out
    x = x + gt1 * rmsnorm(mix, norm_g[1])

    h2 = rmsnorm(x, norm_g[2]) * (1 + sc2) + sh2
    x = x + gt2 * rmsnorm(sq_relu_mlp(h2, w_mlp1, b_mlp1, w_mlp2, b_mlp2), norm_g[3])

    if not ctx_out:
        return x, None

    oc = diff_attend(qc.reshape((b, lc) + qk_shape), kc, vc, lam)
    pac = (rmsnorm(oc, subln_g) * (1.0 - lam_init)).reshape(b, lc, ATTN_WIDTH) @ w_br_a
    psc = ssm_glu(y_ctx, w_glu, b_glu) @ w_br_s
    gac, gsc = jnp.split(jax.nn.sigmoid(gc + gate_b), 2, axis=-1)
    mixc = (gac * pac + gsc * psc) @ w_out
    xc = xc + cgt1 * rmsnorm(mixc, norm_g[1])
    hc2 = rmsnorm(xc, norm_g[2]) * (1 + csc2) + csh2
    xc = xc + cgt2 * rmsnorm(sq_relu_mlp(hc2, w_mlp1, b_mlp1, w_mlp2, b_mlp2), norm_g[3])
    return x, xc


def setup_inputs(seed: int = 0) -> dict:
    key = jax.random.key(seed)
    ks = iter(jax.random.split(key, 40))

    def nrm(shape, std):
        return jax.random.normal(next(ks), shape, jnp.float32) * std

    D, G, P, Hg = D_MODEL, SSM_GROUPS, SSM_STATE, SSM_GROUP
    log_lo, log_hi = math.log(1e-3), math.log(1e-1)
    return {
        "x": nrm((BATCH, SEQ, D), 1.0),
        "c": nrm((BATCH, D), 1.0),
        "ctx": nrm((BATCH, CTX_LEN, D), 1.0),
        "c_ctx": nrm((D,), 1.0),
        "ada_w": nrm((DEPTH, D, 6 * D), 0.5 * D ** -0.5),
        "ada_b": nrm((DEPTH, 6 * D), 0.02),
        "norm_g": 1.0 + nrm((DEPTH, 4, D), 0.02),
        "w_in": nrm((DEPTH, D, IN_COLS), D ** -0.5),
        "gate_b": nrm((DEPTH, G_COLS), 0.02),
        "lam_qk": nrm((DEPTH, 4, HEAD_DIM_A), 0.1),
        "subln_g": 1.0 + nrm((DEPTH, V_DIM_A), 0.02),
        "w_br_a": nrm((DEPTH, ATTN_WIDTH, D), ATTN_WIDTH ** -0.5),
        "ssm_a_re": -0.5 * jnp.exp(nrm((DEPTH, 2, G, P), 0.05)),
        "ssm_a_im": jnp.pi * jnp.arange(P, dtype=jnp.float32) + nrm((DEPTH, 2, G, P), 0.01),
        "ssm_b_re": nrm((DEPTH, 2, G, P, Hg), (2 * Hg) ** -0.5),
        "ssm_b_im": nrm((DEPTH, 2, G, P, Hg), (2 * Hg) ** -0.5),
        "ssm_c_re": nrm((DEPTH, 2, G, Hg, P), (2 * P) ** -0.5),
        "ssm_c_im": nrm((DEPTH, 2, G, Hg, P), (2 * P) ** -0.5),
        "ssm_log_dt": log_lo + (log_hi - log_lo) * jax.random.uniform(next(ks), (DEPTH, 2, G), jnp.float32),
        "ssm_d": nrm((DEPTH, SSM_WIDTH), 0.5),
        "w_glu": nrm((DEPTH, SSM_WIDTH, SSM_WIDTH), SSM_WIDTH ** -0.5),
        "b_glu": nrm((DEPTH, SSM_WIDTH), 0.02),
        "w_br_s": nrm((DEPTH, SSM_WIDTH, D), SSM_WIDTH ** -0.5),
        "w_out": nrm((DEPTH, D, D), D ** -0.5),
        "w_mlp1": nrm((DEPTH, D, D_FF), D ** -0.5),
        "b_mlp1": nrm((DEPTH, D_FF), 0.02),
        "w_mlp2": nrm((DEPTH, D_FF, D), D_FF ** -0.5),
        "b_mlp2": nrm((DEPTH, D), 0.02),
    }


def reference(x, c, ctx, c_ctx, ada_w, ada_b, norm_g, w_in, gate_b, lam_qk, subln_g, w_br_a,
              ssm_a_re, ssm_a_im, ssm_b_re, ssm_b_im, ssm_c_re, ssm_c_im, ssm_log_dt, ssm_d,
              w_glu, b_glu, w_br_s, w_out, w_mlp1, b_mlp1, w_mlp2, b_mlp2):
    cos, sin = axial_rope_tables(x.shape[1])
    xc = ctx
    for i in range(DEPTH):
        lam_init = 0.8 - 0.6 * math.exp(-0.3 * i)
        x, xc = trunk_layer(
            x, xc, c, c_ctx, cos, sin, lam_init, i < DEPTH - 1,
            ada_w[i], ada_b[i], norm_g[i], w_in[i], gate_b[i], lam_qk[i], subln_g[i], w_br_a[i],
            ssm_a_re[i], ssm_a_im[i], ssm_b_re[i], ssm_b_im[i], ssm_c_re[i], ssm_c_im[i],
            ssm_log_dt[i], ssm_d[i], w_glu[i], b_glu[i], w_br_s[i], w_out[i],
            w_mlp1[i], b_mlp1[i], w_mlp2[i], b_mlp2[i])
    return x
```

```python
import functools
import math

import jax
import jax.numpy as jnp
from jax import lax
from jax.experimental import pallas as pl
from jax.experimental.pallas import tpu as pltpu

F32 = jnp.float32
BF16 = jnp.bfloat16

D_MODEL = 1024
DEPTH = 2
GRID_W = 64
CTX_LEN = 256
N_HEADS = 4
HEAD_DIM = 64
V_DIM = 128
QK_W = N_HEADS * 2 * HEAD_DIM
ATTN_W = 512
SSM_W = 512
SSM_GROUP = 16
SSM_GROUPS = 32
SSM_STATE = 64
D_FF = 4096
IN_COLS = 4096
N_FREQ = 16
ROPE_BASE = 10000.0
EPS = 1e-6
LOG2E = 1.4426950408889634

LANES = 128
TM = 256
S5_T = 32
S5_TC = S5_T * SSM_GROUP
VMEM_LIMIT = 56 * 1024 * 1024

NEG_BIG = -1e30


def _cparams(sem):
    return pltpu.CompilerParams(dimension_semantics=sem, vmem_limit_bytes=VMEM_LIMIT)


def _const_spec(shape):
    nd = len(shape)
    return pl.BlockSpec(shape, lambda *_: (0,) * nd, pipeline_mode=pl.Buffered(1))


def _rms(x, g):
    ms = jnp.mean(x * x, axis=-1, keepdims=True)
    return x * lax.rsqrt(ms + EPS) * g


def _mod_kernel(c_ref, w_ref, b_ref, o_ref):
    c = c_ref[...]
    s = c * jax.nn.sigmoid(c)
    o_ref[...] = jnp.dot(s.astype(BF16), w_ref[...].astype(BF16),
                         preferred_element_type=F32) + b_ref[...]


def _modulation(cc, ada_w, ada_b):
    n = ada_w.shape[1]
    bn = 1024
    return pl.pallas_call(
        _mod_kernel,
        grid=(n // bn,),
        in_specs=[pl.BlockSpec((8, D_MODEL), lambda j: (0, 0)),
                  pl.BlockSpec((D_MODEL, bn), lambda j: (0, j)),
                  pl.BlockSpec((1, bn), lambda j: (0, j))],
        out_specs=pl.BlockSpec((8, bn), lambda j: (0, j)),
        out_shape=jax.ShapeDtypeStruct((8, n), F32),
        compiler_params=_cparams(("arbitrary",)),
        name="adaln_mod",
    )(cc, ada_w, ada_b.reshape(1, n))


def _in_kernel(x_ref, mod_ref, g_ref, w_ref, gb_ref, cos_ref, sin_ref,
               q_ref, k_ref, v_ref, u_ref, gate_ref):
    x = x_ref[...]
    h = _rms(x, g_ref[...]) * (1.0 + mod_ref[:, D_MODEL:2 * D_MODEL]) + mod_ref[:, 0:D_MODEL]
    hb = h.astype(BF16)
    cos = cos_ref[...]
    sin = sin_ref[...]
    lane = lax.broadcasted_iota(jnp.int32, (TM, LANES), 1)
    first_half = (lane & 31) < 16

    def rope_store(col0, out_ref, scale):
        t = jnp.dot(hb, w_ref[:, col0:col0 + QK_W], preferred_element_type=F32)
        for j in range(QK_W // LANES):
            tj = t[:, j * LANES:(j + 1) * LANES]
            partner = jnp.where(first_half, pltpu.roll(tj, LANES - 16, 1), pltpu.roll(tj, 16, 1))
            r = tj * cos + partner * sin
            if scale != 1.0:
                r = r * scale
            out_ref[:, j * LANES:(j + 1) * LANES] = r.astype(out_ref.dtype)

    rope_store(0, q_ref, HEAD_DIM ** -0.5 * LOG2E)
    rope_store(QK_W, k_ref, 1.0)
    v_ref[...] = jnp.dot(hb, w_ref[:, 1024:1536], preferred_element_type=F32).astype(v_ref.dtype)
    u_ref[...] = jnp.dot(hb, w_ref[:, 1536:2048], preferred_element_type=F32)
    g = jnp.dot(hb, w_ref[:, 2048:IN_COLS], preferred_element_type=F32) + gb_ref[...]
    gate_ref[...] = jax.nn.sigmoid(g).astype(gate_ref.dtype)


def _in_proj(xcat, modsel, g0, w_in_b, gate_b, cos_t, sin_t):
    b, n, _ = xcat.shape
    nt = n // TM
    row = lambda bi, i: (bi, i, 0)
    return pl.pallas_call(
        _in_kernel,
        grid=(b, nt),
        in_specs=[pl.BlockSpec((None, TM, D_MODEL), row),
                  pl.BlockSpec((None, None, 1, 6 * D_MODEL),
                               lambda bi, i: (bi, jnp.minimum(i, 1), 0, 0)),
                  _const_spec((1, D_MODEL)),
                  _const_spec((D_MODEL, IN_COLS)),
                  _const_spec((1, 2 * D_MODEL)),
                  pl.BlockSpec((TM, LANES), lambda bi, i: (i, 0)),
                  pl.BlockSpec((TM, LANES), lambda bi, i: (i, 0))],
        out_specs=[pl.BlockSpec((None, TM, QK_W), row),
                   pl.BlockSpec((None, TM, QK_W), row),
                   pl.BlockSpec((None, TM, ATTN_W), row),
                   pl.BlockSpec((None, TM, SSM_W), row),
                   pl.BlockSpec((None, TM, 2 * D_MODEL), row)],
        out_shape=[jax.ShapeDtypeStruct((b, n, QK_W), BF16),
                   jax.ShapeDtypeStruct((b, n, QK_W), BF16),
                   jax.ShapeDtypeStruct((b, n, ATTN_W), BF16),
                   jax.ShapeDtypeStruct((b, n, SSM_W), F32),
                   jax.ShapeDtypeStruct((b, n, 2 * D_MODEL), BF16)],
        compiler_params=_cparams(("parallel", "arbitrary")),
        name="in_proj",
    )(xcat, modsel, g0, w_in_b, gate_b, cos_t, sin_t)


def _attn_kernel(lamqk_ref, subg_ref, q_ref, k_ref, v_ref, o_ref,
                 qm_sc, m_sc, l_sc, acc_sc, *, lam_init, nk, ctx_first):
    qi = pl.program_id(2)
    ki = pl.program_id(3)

    @pl.when(ki == 0)
    def _init():
        q = q_ref[...]
        lane = lax.broadcasted_iota(jnp.int32, q.shape, 1)
        zero = jnp.zeros_like(q)
        qm_sc[0] = jnp.where(lane < HEAD_DIM, q, zero)
        qm_sc[1] = jnp.where(lane >= HEAD_DIM, q, zero)
        m_sc[...] = jnp.full(m_sc.shape, NEG_BIG, F32)
        l_sc[...] = jnp.zeros(l_sc.shape, F32)
        acc_sc[...] = jnp.zeros(acc_sc.shape, F32)

    def _step():
        k = k_ref[...]
        v = v_ref[...]
        for c in range(2):
            s = lax.dot_general(qm_sc[c], k, (((1,), (1,)), ((), ())),
                                preferred_element_type=F32)
            m_prev = m_sc[c]
            m_new = jnp.maximum(m_prev, jnp.max(s, axis=1, keepdims=True))
            alpha = jnp.exp2(m_prev - m_new)
            p = jnp.exp2(s - m_new)
            l_sc[c] = alpha * l_sc[c] + jnp.sum(p, axis=1, keepdims=True)
            acc_sc[c] = alpha * acc_sc[c] + jnp.dot(p.astype(BF16), v, preferred_element_type=F32)
            m_sc[c] = m_new

    if ctx_first:
        pl.when((qi > 0) | (ki == 0))(_step)
    else:
        _step()

    @pl.when(ki == nk - 1)
    def _fin():
        lq = lamqk_ref[...]
        lam = (jnp.exp(jnp.sum(lq[0:1] * lq[1:2], axis=1, keepdims=True))
               - jnp.exp(jnp.sum(lq[2:3] * lq[3:4], axis=1, keepdims=True)) + lam_init)
        o = acc_sc[0] / l_sc[0] - lam * (acc_sc[1] / l_sc[1])
        o_ref[...] = (_rms(o, subg_ref[...]) * (1.0 - lam_init)).astype(o_ref.dtype)


def _attention(q, k, v, lam_qk, subln_g, lam_init, ctx_first):
    b, n, _ = q.shape
    nk = n // TM
    qoff = 0 if ctx_first else 1
    nq = nk - qoff
    if ctx_first:
        kv_idx = lambda bi, h, qi, ki: (bi, jnp.where(qi == 0, 0, ki), h)
    else:
        kv_idx = lambda bi, h, qi, ki: (bi, ki, h)
    q_idx = lambda bi, h, qi, ki: (bi, qi + qoff, h)
    kern = functools.partial(_attn_kernel, lam_init=lam_init, nk=nk, ctx_first=ctx_first)
    return pl.pallas_call(
        kern,
        grid=(b, N_HEADS, nq, nk),
        in_specs=[pl.BlockSpec((4, HEAD_DIM), lambda *_: (0, 0)),
                  pl.BlockSpec((1, V_DIM), lambda *_: (0, 0)),
                  pl.BlockSpec((None, TM, LANES), q_idx),
                  pl.BlockSpec((None, TM, LANES), kv_idx),
                  pl.BlockSpec((None, TM, LANES), kv_idx)],
        out_specs=pl.BlockSpec((None, TM, LANES), q_idx),
        out_shape=jax.ShapeDtypeStruct((b, n, ATTN_W), BF16),
        scratch_shapes=[pltpu.VMEM((2, TM, LANES), BF16),
                        pltpu.VMEM((2, TM, 1), F32),
                        pltpu.VMEM((2, TM, 1), F32),
                        pltpu.VMEM((2, TM, V_DIM), F32)],
        compiler_params=_cparams(("parallel", "parallel", "parallel", "arbitrary")),
        name="diff_attn",
    )(lam_qk, subln_g.reshape(1, V_DIM), q, k, v)


def _cmul(ar, ai, br, bi):
    return ar * br - ai * bi, ar * bi + ai * br


def _cpow_by_bits(lr, li, expo, shape, nbits):
    pr = jnp.ones(shape, F32)
    pi = jnp.zeros(shape, F32)
    br, bi = lr, li
    for bit in range(nbits):
        on = ((expo >> bit) & 1) == 1
        fr = jnp.where(on, br, 1.0)
        fi = jnp.where(on, bi, 0.0)
        pr, pi = _cmul(pr, pi, fr, fi)
        if bit + 1 < nbits:
            br, bi = _cmul(br, bi, br, bi)
    return pr, pi


def _discretize(ar, ai, log_dt):
    dt = jnp.exp(log_dt)
    er = jnp.exp(ar * dt)
    lbr = er * jnp.cos(ai * dt)
    lbi = er * jnp.sin(ai * dt)
    den = ar * ar + ai * ai
    nr = lbr - 1.0
    cr = (nr * ar + lbi * ai) / den
    ci = (lbi * ar - nr * ai) / den
    return lbr, lbi, cr, ci


def _s5_kernel(x_ref, ar_r, ai_r, ar_c, ai_c, ldt_ref, btr_ref, bti_ref, ctr_ref, cti_ref,
               y_ref, sext, toep, bsr, bsi, csr, csi, zre, zim, hre, him, *, nchunk, nbatch):
    ldt = ldt_ref[...]
    lbr, lbi, cr, ci = _discretize(ar_r[...], ai_r[...], ldt)
    lbr_c, lbi_c, _, _ = _discretize(ar_c[...], ai_c[...], ldt)

    bbr, bbi = _cmul(cr, ci, btr_ref[...], bti_ref[...])
    bbr_t = jnp.concatenate([bbr] * S5_T, axis=0)
    bbi_t = jnp.concatenate([bbi] * S5_T, axis=0)
    rowj = lax.broadcasted_iota(jnp.int32, (S5_TC, SSM_STATE), 0) // SSM_GROUP
    pwr, pwi = _cpow_by_bits(lbr, lbi, (S5_T - 1) - rowj, (S5_TC, SSM_STATE), 5)
    wr, wi = _cmul(pwr, pwi, bbr_t, bbi_t)
    bsr[...] = wr.astype(BF16)
    bsi[...] = wi.astype(BF16)

    tau = lax.broadcasted_iota(jnp.int32, (SSM_STATE, S5_TC), 1) // SSM_GROUP
    per, pei = _cpow_by_bits(lbr_c, lbi_c, tau, (SSM_STATE, S5_TC), 5)
    wcr, wci = _cmul(per, pei, ctr_ref[...], cti_ref[...])
    c1r, c1i = _cmul(wcr, wci, lbr_c, lbi_c)
    csr[...] = c1r.astype(BF16)
    csi[...] = (-c1i).astype(BF16)
    strip = (jnp.dot(bbr, wcr, preferred_element_type=F32, precision=lax.Precision.HIGHEST)
             - jnp.dot(bbi, wci, preferred_element_type=F32, precision=lax.Precision.HIGHEST))
    sext[:, 0:S5_TC] = jnp.zeros((SSM_GROUP, S5_TC), F32)
    sext[:, S5_TC:2 * S5_TC] = strip
    for j in range(S5_T):
        toep[j * SSM_GROUP:(j + 1) * SSM_GROUP, :] = (
            sext[:, S5_TC - SSM_GROUP * j:2 * S5_TC - SSM_GROUP * j].astype(BF16))

    x = x_ref[...]
    y_ref[...] = jnp.dot(x, toep[...], preferred_element_type=F32)
    zre[...] = jnp.dot(x, bsr[...], preferred_element_type=F32)
    zim[...] = jnp.dot(x, bsi[...], preferred_element_type=F32)

    tr, ti = lbr, lbi
    for _ in range(int(math.log2(S5_T))):
        tr, ti = _cmul(tr, ti, tr, ti)

    def body(n, carry):
        new = []
        for b in range(nbatch):
            h_r, h_i = carry[2 * b], carry[2 * b + 1]
            r = b * nchunk + n
            hre[pl.ds(r, 1), :] = h_r
            him[pl.ds(r, 1), :] = h_i
            n_r, n_i = _cmul(tr, ti, h_r, h_i)
            new.append(n_r + zre[pl.ds(r, 1), :])
            new.append(n_i + zim[pl.ds(r, 1), :])
        return tuple(new)

    zero = jnp.zeros((1, SSM_STATE), F32)
    lax.fori_loop(0, nchunk, body, (zero,) * (2 * nbatch))

    y_ref[...] += (jnp.dot(hre[...].astype(BF16), csr[...], preferred_element_type=F32)
                   + jnp.dot(him[...].astype(BF16), csi[...], preferred_element_type=F32))


def _s5(xg, a_re, a_im, log_dt, b_re, b_im, c_re, c_im, nbatch):
    _, g, nr, _ = xg.shape
    nchunk = nr // nbatch
    ar_r = a_re.reshape(2, g, 1, SSM_STATE)
    ai_r = a_im.reshape(2, g, 1, SSM_STATE)
    ar_c = a_re.reshape(2, g, SSM_STATE, 1)
    ai_c = a_im.reshape(2, g, SSM_STATE, 1)
    ldt = log_dt.reshape(2, g, 1, 1)
    btr = jnp.swapaxes(b_re, -1, -2)
    bti = jnp.swapaxes(b_im, -1, -2)
    ctr = jnp.tile(jnp.swapaxes(c_re, -1, -2), (1, 1, 1, S5_T))
    cti = jnp.tile(jnp.swapaxes(c_im, -1, -2), (1, 1, 1, S5_T))
    blk = lambda *s: pl.BlockSpec((None, None) + s, lambda d, gi: (d, gi, 0, 0))
    kern = functools.partial(_s5_kernel, nchunk=nchunk, nbatch=nbatch)
    return pl.pallas_call(
        kern,
        grid=(2, g),
        in_specs=[blk(nr, S5_TC),
                  blk(1, SSM_STATE), blk(1, SSM_STATE), blk(SSM_STATE, 1), blk(SSM_STATE, 1),
                  blk(1, 1),
                  blk(SSM_GROUP, SSM_STATE), blk(SSM_GROUP, SSM_STATE),
                  blk(SSM_STATE, S5_TC), blk(SSM_STATE, S5_TC)],
        out_specs=blk(nr, S5_TC),
        out_shape=jax.ShapeDtypeStruct((2, g, nr, S5_TC), F32),
        scratch_shapes=[pltpu.VMEM((SSM_GROUP, 2 * S5_TC), F32),
                        pltpu.VMEM((S5_TC, S5_TC), BF16),
                        pltpu.VMEM((S5_TC, SSM_STATE), BF16),
                        pltpu.VMEM((S5_TC, SSM_STATE), BF16),
                        pltpu.VMEM((SSM_STATE, S5_TC), BF16),
                        pltpu.VMEM((SSM_STATE, S5_TC), BF16),
                        pltpu.VMEM((nr, SSM_STATE), F32),
                        pltpu.VMEM((nr, SSM_STATE), F32),
                        pltpu.VMEM((nr, SSM_STATE), F32),
                        pltpu.VMEM((nr, SSM_STATE), F32)],
        compiler_params=_cparams(("parallel", "parallel")),
        name="s5_scan",
    )(xg, ar_r, ai_r, ar_c, ai_c, ldt, btr, bti, ctr, cti)


def _to_chunks(u_seq):
    b, n, _ = u_seq.shape
    x = u_seq.reshape(b, n // S5_T, S5_T, SSM_GROUPS, SSM_GROUP)
    x = jnp.transpose(x, (3, 0, 1, 2, 4))
    return x.reshape(SSM_GROUPS, b * (n // S5_T), S5_TC)


def _from_chunks(y, b):
    g, nr, _ = y.shape
    n = nr // b * S5_T
    y = y.reshape(g, b, n // S5_T, S5_T, SSM_GROUP)
    y = jnp.transpose(y, (1, 2, 3, 0, 4))
    return y.reshape(b, n, SSM_W)


def _flip_parts(a):
    return jnp.concatenate([a[:, :CTX_LEN][:, ::-1], a[:, CTX_LEN:][:, ::-1]], axis=1)


def _merge_kernel(x_ref, on_ref, y_ref, u_ref, gate_ref, mod_ref, d_ref, wglu_ref, bglu_ref,
                  wbs_ref, wba_ref, wout_ref, g1_ref, o_ref):
    ys = y_ref[...] + u_ref[...] * d_ref[...]
    gl = jax.nn.gelu(ys)
    z = gl * jax.nn.sigmoid(jnp.dot(gl.astype(BF16), wglu_ref[...], preferred_element_type=F32)
                            + bglu_ref[...])
    ps = jnp.dot(z.astype(BF16), wbs_ref[...], preferred_element_type=F32)
    pa = jnp.dot(on_ref[...], wba_ref[...], preferred_element_type=F32)
    gate = gate_ref[...].astype(F32)
    mix = gate[:, 0:D_MODEL] * pa + gate[:, D_MODEL:2 * D_MODEL] * ps
    m2 = jnp.dot(mix.astype(BF16), wout_ref[...], preferred_element_type=F32)
    o_ref[...] = x_ref[...] + mod_ref[:, 2 * D_MODEL:3 * D_MODEL] * _rms(m2, g1_ref[...])


def _merge(xcat, o_n, y, u, gates, modsel, ssm_d, w_glu, b_glu, w_br_s, w_br_a, w_out, g1, roff):
    b, n, _ = xcat.shape
    nt = n // TM - roff
    row = lambda bi, i: (bi, i + roff, 0)
    return pl.pallas_call(
        _merge_kernel,
        grid=(b, nt),
        in_specs=[pl.BlockSpec((None, TM, D_MODEL), row),
                  pl.BlockSpec((None, TM, ATTN_W), row),
                  pl.BlockSpec((None, TM, SSM_W), row),
                  pl.BlockSpec((None, TM, SSM_W), row),
                  pl.BlockSpec((None, TM, 2 * D_MODEL), row),
                  pl.BlockSpec((None, None, 1, 6 * D_MODEL),
                               lambda bi, i: (bi, jnp.minimum(i + roff, 1), 0, 0)),
                  _const_spec((1, SSM_W)),
                  _const_spec((SSM_W, SSM_W)),
                  _const_spec((1, SSM_W)),
                  _const_spec((SSM_W, D_MODEL)),
                  _const_spec((ATTN_W, D_MODEL)),
                  _const_spec((D_MODEL, D_MODEL)),
                  _const_spec((1, D_MODEL))],
        out_specs=pl.BlockSpec((None, TM, D_MODEL), row),
        out_shape=jax.ShapeDtypeStruct((b, n, D_MODEL), F32),
        compiler_params=_cparams(("parallel", "arbitrary")),
        name="merge",
    )(xcat, o_n, y, u, gates, modsel, ssm_d, w_glu, b_glu, w_br_s, w_br_a, w_out, g1)


def _mlp_kernel(x_ref, mod_ref, g2_ref, w1_ref, b1_ref, w2_ref, b2_ref, g3_ref, o_ref):
    x = x_ref[...]
    h = _rms(x, g2_ref[...]) * (1.0 + mod_ref[:, 4 * D_MODEL:5 * D_MODEL]) \
        + mod_ref[:, 3 * D_MODEL:4 * D_MODEL]
    f = jnp.dot(h.astype(BF16), w1_ref[...], preferred_element_type=F32) + b1_ref[...]
    f = jnp.square(jnp.maximum(f, 0.0))
    o = jnp.dot(f.astype(BF16), w2_ref[...], preferred_element_type=F32) + b2_ref[...]
    o_ref[...] = x + mod_ref[:, 5 * D_MODEL:6 * D_MODEL] * _rms(o, g3_ref[...])


def _mlp(x1, modsel, g2, w1, b1, w2, b2, g3, roff, n_out):
    b, n, _ = x1.shape
    nt = n // TM - roff
    return pl.pallas_call(
        _mlp_kernel,
        grid=(b, nt),
        in_specs=[pl.BlockSpec((None, TM, D_MODEL), lambda bi, i: (bi, i + roff, 0)),
                  pl.BlockSpec((None, None, 1, 6 * D_MODEL),
                               lambda bi, i: (bi, jnp.minimum(i + roff, 1), 0, 0)),
                  _const_spec((1, D_MODEL)),
                  _const_spec((D_MODEL, D_FF)),
                  _const_spec((1, D_FF)),
                  _const_spec((D_FF, D_MODEL)),
                  _const_spec((1, D_MODEL)),
                  _const_spec((1, D_MODEL))],
        out_specs=pl.BlockSpec((None, TM, D_MODEL), lambda bi, i: (bi, i, 0)),
        out_shape=jax.ShapeDtypeStruct((b, n_out, D_MODEL), F32),
        compiler_params=_cparams(("parallel", "arbitrary")),
        name="mlp",
    )(x1, modsel, g2, w1, b1, w2, b2, g3)


def _rope_tables(n_tokens):
    rows = n_tokens // GRID_W
    row = jnp.repeat(jnp.arange(rows, dtype=jnp.int32), GRID_W).astype(F32)
    col = jnp.tile(jnp.arange(GRID_W, dtype=jnp.int32), rows).astype(F32)
    inv_freq = ROPE_BASE ** (-jnp.arange(N_FREQ, dtype=F32) / N_FREQ)
    ang = jnp.stack([row[:, None] * inv_freq, col[:, None] * inv_freq], axis=1)
    cos, sin = jnp.cos(ang), jnp.sin(ang)
    cos_l = jnp.tile(jnp.concatenate([cos[:, 0], cos[:, 0], cos[:, 1], cos[:, 1]], axis=1), (1, 2))
    sin_l = jnp.tile(jnp.concatenate([-sin[:, 0], sin[:, 0], -sin[:, 1], sin[:, 1]], axis=1), (1, 2))
    cos_t = jnp.concatenate([jnp.ones((CTX_LEN, LANES), F32), cos_l], axis=0)
    sin_t = jnp.concatenate([jnp.zeros((CTX_LEN, LANES), F32), sin_l], axis=0)
    return cos_t, sin_t


def kernel(x, c, ctx, c_ctx, ada_w, ada_b, norm_g, w_in, gate_b, lam_qk, subln_g, w_br_a,
           ssm_a_re, ssm_a_im, ssm_b_re, ssm_b_im, ssm_c_re, ssm_c_im, ssm_log_dt, ssm_d,
           w_glu, b_glu, w_br_s, w_out, w_mlp1, b_mlp1, w_mlp2, b_mlp2):
    b, seq, _ = x.shape
    assert b + 1 <= 8 and ctx.shape[1] == CTX_LEN == TM and seq % TM == 0
    cos_t, sin_t = _rope_tables(seq)
    xcat = jnp.concatenate([ctx, x], axis=1)
    cc = jnp.zeros((8, D_MODEL), F32).at[:b].set(c).at[b].set(c_ctx)
    out = None
    for i in range(DEPTH):
        last = i == DEPTH - 1
        lam_init = 0.8 - 0.6 * math.exp(-0.3 * i)
        roff = 1 if last else 0
        mod = _modulation(cc, ada_w[i], ada_b[i])
        modsel = jnp.stack([jnp.broadcast_to(mod[b], (b, 6 * D_MODEL)), mod[:b]], axis=1)
        modsel = modsel.reshape(b, 2, 1, 6 * D_MODEL)
        ng = norm_g[i]
        q, k, v, u, gates = _in_proj(xcat, modsel, ng[0].reshape(1, -1), w_in[i].astype(BF16),
                                     gate_b[i].reshape(1, -1), cos_t, sin_t)
        o_n = _attention(q, k, v, lam_qk[i], subln_g[i], lam_init, ctx_first=not last)
        ub = u.astype(BF16)
        xg = jnp.stack([_to_chunks(ub), _to_chunks(_flip_parts(ub))], axis=0)
        yg = _s5(xg, ssm_a_re[i], ssm_a_im[i], ssm_log_dt[i], ssm_b_re[i], ssm_b_im[i],
                 ssm_c_re[i], ssm_c_im[i], b)
        y = _from_chunks(yg[0], b) + _flip_parts(_from_chunks(yg[1], b))
        x1 = _merge(xcat, o_n, y, u, gates, modsel, ssm_d[i].reshape(1, -1),
                    w_glu[i].astype(BF16), b_glu[i].reshape(1, -1), w_br_s[i].astype(BF16),
                    w_br_a[i].astype(BF16), w_out[i].astype(BF16), ng[1].reshape(1, -1), roff)
        n_out = seq if last else seq + CTX_LEN
        xcat = _mlp(x1, modsel, ng[2].reshape(1, -1), w_mlp1[i].astype(BF16),
                    b_mlp1[i].reshape(1, -1), w_mlp2[i].astype(BF16), b_mlp2[i].reshape(1, -1),
                    ng[3].reshape(1, -1), roff, n_out)
        out = xcat
    return out
```

```python
import functools
import math

import jax
import jax.numpy as jnp
from jax import lax
from jax.experimental import pallas as pl
from jax.experimental.pallas import tpu as pltpu

F32 = jnp.float32
BF16 = jnp.bfloat16

D_MODEL = 1024
DEPTH = 2
GRID_W = 64
CTX_LEN = 256
N_HEADS = 4
HEAD_DIM = 64
V_DIM = 128
QK_W = N_HEADS * 2 * HEAD_DIM
ATTN_W = 512
SSM_W = 512
SSM_GROUP = 16
SSM_GROUPS = 32
SSM_STATE = 64
D_FF = 4096
IN_COLS = 4096
N_FREQ = 16
ROPE_BASE = 10000.0
EPS = 1e-6
LOG2E = 1.4426950408889634

LANES = 128
TM = 256
TQ = 512
TK = 2816
S5_T = 16
S5_TC = S5_T * SSM_GROUP
S5_SLABS = SSM_W // LANES
S5_GPS = LANES // SSM_GROUP
PAIR = 2 * SSM_STATE
VMEM_LIMIT = 56 * 1024 * 1024

NEG_BIG = -1e30


def _cparams(sem):
    return pltpu.CompilerParams(dimension_semantics=sem, vmem_limit_bytes=VMEM_LIMIT)


def _const_spec(shape):
    nd = len(shape)
    return pl.BlockSpec(shape, lambda *_: (0,) * nd, pipeline_mode=pl.Buffered(1))


def _mod_spec(roff, ctx_block):
    return pl.BlockSpec((None, None, 1, 6 * D_MODEL),
                        lambda bi, i: (bi, (i + roff) // ctx_block, 0, 0))


def _rms(x, g):
    ms = jnp.mean(x * x, axis=-1, keepdims=True)
    return x * lax.rsqrt(ms + EPS) * g


def _mod_kernel(c_ref, w_ref, b_ref, o_ref):
    c = c_ref[...]
    s = c * jax.nn.sigmoid(c)
    o_ref[...] = jnp.dot(s.astype(BF16), w_ref[...].astype(BF16),
                         preferred_element_type=F32) + b_ref[...]


def _modulation(cc, ada_w, ada_b):
    n = ada_w.shape[1]
    bn = 1024
    return pl.pallas_call(
        _mod_kernel,
        grid=(n // bn,),
        in_specs=[pl.BlockSpec((8, D_MODEL), lambda j: (0, 0)),
                  pl.BlockSpec((D_MODEL, bn), lambda j: (0, j)),
                  pl.BlockSpec((1, bn), lambda j: (0, j))],
        out_specs=pl.BlockSpec((8, bn), lambda j: (0, j)),
        out_shape=jax.ShapeDtypeStruct((8, n), F32),
        compiler_params=_cparams(("arbitrary",)),
        name="adaln_mod",
    )(cc, ada_w, ada_b.reshape(1, n))


def _lane_window(off):
    lane = lax.broadcasted_iota(jnp.int32, (S5_T, LANES), 1)
    return (lane >= off) & (lane < off + SSM_GROUP)


def _rows_to_chunks(u, usc, xg_ref):
    for sl in range(S5_SLABS):
        usc[sl] = u[:, sl * LANES:(sl + 1) * LANES]
    for sl in range(S5_SLABS):
        acc = [[None] * (S5_TC // LANES) for _ in range(S5_GPS)]
        for t in range(S5_T):
            v = usc[sl, pl.ds(t, TM // S5_T, stride=S5_T), :]
            lt, off = divmod(t * SSM_GROUP, LANES)
            win = _lane_window(off)
            for gl in range(S5_GPS):
                shift = (off - gl * SSM_GROUP) % LANES
                moved = pltpu.roll(v, shift, 1) if shift else v
                prev = acc[gl][lt]
                acc[gl][lt] = jnp.where(win, moved, 0.0 if prev is None else prev)
        for gl in range(S5_GPS):
            for lt in range(S5_TC // LANES):
                xg_ref[sl * S5_GPS + gl, :, lt * LANES:(lt + 1) * LANES] = acc[gl][lt].astype(xg_ref.dtype)


def _chunks_to_rows(yg_ref, ysc):
    for sl in range(S5_SLABS):
        tiles = [[yg_ref[sl * S5_GPS + gl, :, lt * LANES:(lt + 1) * LANES]
                  for lt in range(S5_TC // LANES)] for gl in range(S5_GPS)]
        for t in range(S5_T):
            lt, off = divmod(t * SSM_GROUP, LANES)
            v = None
            for gl in range(S5_GPS):
                shift = (gl * SSM_GROUP - off) % LANES
                src = tiles[gl][lt]
                moved = pltpu.roll(src, shift, 1) if shift else src
                v = jnp.where(_lane_window(gl * SSM_GROUP), moved, 0.0 if v is None else v)
            ysc[sl, pl.ds(t, TM // S5_T, stride=S5_T), :] = v
    return jnp.concatenate([ysc[sl] for sl in range(S5_SLABS)], axis=1)


def _in_kernel(x_ref, mod_ref, g_ref, w_ref, gb_ref, cos_ref, sin_ref,
               q_ref, k_ref, v_ref, u_ref, gate_ref, xg_ref, usc):
    x = x_ref[...]
    h = _rms(x, g_ref[...]) * (1.0 + mod_ref[:, D_MODEL:2 * D_MODEL]) + mod_ref[:, 0:D_MODEL]
    hb = h.astype(BF16)
    cos = cos_ref[...]
    sin = sin_ref[...]
    lane = lax.broadcasted_iota(jnp.int32, (TM, LANES), 1)
    first_half = (lane & 31) < 16

    def rope_store(col0, out_ref, scale):
        t = jnp.dot(hb, w_ref[:, col0:col0 + QK_W], preferred_element_type=F32)
        for j in range(QK_W // LANES):
            tj = t[:, j * LANES:(j + 1) * LANES]
            partner = jnp.where(first_half, pltpu.roll(tj, LANES - 16, 1), pltpu.roll(tj, 16, 1))
            r = tj * cos + partner * sin
            if scale != 1.0:
                r = r * scale
            out_ref[:, j * LANES:(j + 1) * LANES] = r.astype(out_ref.dtype)

    rope_store(0, q_ref, HEAD_DIM ** -0.5 * LOG2E)
    rope_store(QK_W, k_ref, 1.0)
    v_ref[...] = jnp.dot(hb, w_ref[:, 1024:1536], preferred_element_type=F32).astype(v_ref.dtype)
    u = jnp.dot(hb, w_ref[:, 1536:2048], preferred_element_type=F32)
    u_ref[...] = u
    _rows_to_chunks(u, usc, xg_ref)
    g = jnp.dot(hb, w_ref[:, 2048:IN_COLS], preferred_element_type=F32) + gb_ref[...]
    gate_ref[...] = jax.nn.sigmoid(g).astype(gate_ref.dtype)


def _in_proj(xcat, modsel, g0, w_in_b, gate_b, cos_t, sin_t):
    b, n, _ = xcat.shape
    nt = n // TM
    row = lambda bi, i: (bi, i, 0)
    return pl.pallas_call(
        _in_kernel,
        grid=(b, nt),
        in_specs=[pl.BlockSpec((None, TM, D_MODEL), row),
                  _mod_spec(0, nt - 1),
                  _const_spec((1, D_MODEL)),
                  _const_spec((D_MODEL, IN_COLS)),
                  _const_spec((1, 2 * D_MODEL)),
                  pl.BlockSpec((TM, LANES), lambda bi, i: (i, 0)),
                  pl.BlockSpec((TM, LANES), lambda bi, i: (i, 0))],
        out_specs=[pl.BlockSpec((None, TM, QK_W), row),
                   pl.BlockSpec((None, TM, QK_W), row),
                   pl.BlockSpec((None, TM, ATTN_W), row),
                   pl.BlockSpec((None, TM, SSM_W), row),
                   pl.BlockSpec((None, TM, 2 * D_MODEL), row),
                   pl.BlockSpec((SSM_GROUPS, TM // S5_T, S5_TC), lambda bi, i: (0, bi * nt + i, 0))],
        out_shape=[jax.ShapeDtypeStruct((b, n, QK_W), BF16),
                   jax.ShapeDtypeStruct((b, n, QK_W), BF16),
                   jax.ShapeDtypeStruct((b, n, ATTN_W), BF16),
                   jax.ShapeDtypeStruct((b, n, SSM_W), F32),
                   jax.ShapeDtypeStruct((b, n, 2 * D_MODEL), BF16),
                   jax.ShapeDtypeStruct((SSM_GROUPS, b * n // S5_T, S5_TC), BF16)],
        scratch_shapes=[pltpu.VMEM((S5_SLABS, TM, LANES), F32)],
        compiler_params=_cparams(("parallel", "arbitrary")),
        name="in_proj",
    )(xcat, modsel, g0, w_in_b, gate_b, cos_t, sin_t)


def _attn_kernel(lamqk_ref, subg_ref, q_ref, k_ref, v_ref, o_ref,
                 qm_sc, m_sc, l_sc, acc_sc, *, lam_init, nk):
    ki = pl.program_id(3)

    @pl.when(ki == 0)
    def _init():
        q = q_ref[...]
        lane = lax.broadcasted_iota(jnp.int32, q.shape, 1)
        zero = jnp.zeros_like(q)
        qm_sc[0] = jnp.where(lane < HEAD_DIM, q, zero)
        qm_sc[1] = jnp.where(lane >= HEAD_DIM, q, zero)
        m_sc[...] = jnp.full(m_sc.shape, NEG_BIG, F32)
        l_sc[...] = jnp.zeros(l_sc.shape, F32)
        acc_sc[...] = jnp.zeros(acc_sc.shape, F32)

    k = k_ref[...]
    v = v_ref[...]
    for c in range(2):
        s = lax.dot_general(qm_sc[c], k, (((1,), (1,)), ((), ())), preferred_element_type=F32)
        m_prev = m_sc[c]
        m_new = jnp.maximum(m_prev, jnp.max(s, axis=1, keepdims=True))
        alpha = jnp.exp2(m_prev - m_new)
        p = jnp.exp2(s - m_new)
        l_sc[c] = alpha * l_sc[c] + jnp.sum(p, axis=1, keepdims=True)
        acc_sc[c] = alpha * acc_sc[c] + jnp.dot(p.astype(BF16), v, preferred_element_type=F32)
        m_sc[c] = m_new

    @pl.when(ki == nk - 1)
    def _fin():
        lq = lamqk_ref[...]
        lam = (jnp.exp(jnp.sum(lq[0:1] * lq[1:2], axis=1, keepdims=True))
               - jnp.exp(jnp.sum(lq[2:3] * lq[3:4], axis=1, keepdims=True)) + lam_init)
        o = acc_sc[0] / l_sc[0] - lam * (acc_sc[1] / l_sc[1])
        o_ref[...] = (_rms(o, subg_ref[...]) * (1.0 - lam_init)).astype(o_ref.dtype)


def _attention(q, k, v, lam_qk, subln_g, lam_init, *, tq, tk, q0, nq, k0, nk, o_prev=None):
    b, n, _ = q.shape
    q_idx = lambda bi, h, qi, ki: (bi, qi + q0, h)
    kv_idx = lambda bi, h, qi, ki: (bi, ki + k0, h)
    kern = functools.partial(_attn_kernel, lam_init=lam_init, nk=nk)
    in_specs = [pl.BlockSpec((4, HEAD_DIM), lambda *_: (0, 0)),
                pl.BlockSpec((1, V_DIM), lambda *_: (0, 0)),
                pl.BlockSpec((None, tq, LANES), q_idx),
                pl.BlockSpec((None, tk, LANES), kv_idx),
                pl.BlockSpec((None, tk, LANES), kv_idx)]
    args = [lam_qk, subln_g.reshape(1, V_DIM), q, k, v]
    aliases = {}
    if o_prev is not None:
        in_specs.append(pl.BlockSpec(memory_space=pl.ANY))
        args.append(o_prev)
        aliases = {5: 0}
        kern = functools.partial(_drop_last_input, kern, 5)
    return pl.pallas_call(
        kern,
        grid=(b, N_HEADS, nq, nk),
        in_specs=in_specs,
        out_specs=pl.BlockSpec((None, tq, LANES), q_idx),
        out_shape=jax.ShapeDtypeStruct((b, n, ATTN_W), BF16),
        scratch_shapes=[pltpu.VMEM((2, tq, LANES), BF16),
                        pltpu.VMEM((2, tq, 1), F32),
                        pltpu.VMEM((2, tq, 1), F32),
                        pltpu.VMEM((2, tq, V_DIM), F32)],
        input_output_aliases=aliases,
        compiler_params=_cparams(("parallel", "parallel", "parallel", "arbitrary")),
        name="diff_attn",
    )(*args)


def _drop_last_input(kern, idx, *refs):
    return kern(*refs[:idx], *refs[idx + 1:])


def _cmul(ar, ai, br, bi):
    return ar * br - ai * bi, ar * bi + ai * br


def _cpow_by_bits(lr, li, expo, nbits):
    pr = jnp.ones(expo.shape, F32)
    pi = jnp.zeros(expo.shape, F32)
    br, bi = lr, li
    for bit in range(nbits):
        on = ((expo >> bit) & 1) == 1
        fr = jnp.where(on, br, 1.0)
        fi = jnp.where(on, bi, 0.0)
        pr, pi = _cmul(pr, pi, fr, fi)
        if bit + 1 < nbits:
            br, bi = _cmul(br, bi, br, bi)
    return pr, pi


def _discretize(ar, ai, log_dt):
    dt = jnp.exp(log_dt)
    er = jnp.exp(ar * dt)
    lbr = er * jnp.cos(ai * dt)
    lbi = er * jnp.sin(ai * dt)
    den = ar * ar + ai * ai
    nr = lbr - 1.0
    cr = (nr * ar + lbi * ai) / den
    ci = (lbi * ar - nr * ai) / den
    return lbr, lbi, cr, ci


def _s5_kernel(x_ref, ar_r, ai_r, ldt_r, ar_c, ai_c, ldt_c, btr_ref, bti_ref, ctr_ref, cti_ref,
               y_ref, sext, toep, zre, zim, hre, him, *, nchunk, nbatch, nctx):
    nlat = nchunk - nctx
    rowj = lax.broadcasted_iota(jnp.int32, (S5_TC, PAIR), 0) // SSM_GROUP
    tau = lax.broadcasted_iota(jnp.int32, (PAIR, S5_TC), 1) // SSM_GROUP
    lane_g = lax.broadcasted_iota(jnp.int32, (1, PAIR), 1) // SSM_STATE
    row_g = lax.broadcasted_iota(jnp.int32, (PAIR, 1), 0) // SSM_STATE

    for d in range(2):
        fwd = d == 0
        lbr, lbi, cr, ci = _discretize(ar_r[d], ai_r[d], ldt_r[d])
        lbr_c, lbi_c, _, _ = _discretize(ar_c[d], ai_c[d], ldt_c[d])
        bbr, bbi = _cmul(cr, ci, btr_ref[d], bti_ref[d])
        bbr_t = jnp.concatenate([bbr] * S5_T, axis=0)
        bbi_t = jnp.concatenate([bbi] * S5_T, axis=0)
        pwr, pwi = _cpow_by_bits(lbr, lbi, (S5_T - 1 - rowj) if fwd else rowj, 4)
        bsr, bsi = _cmul(pwr, pwi, bbr_t, bbi_t)
        per, pei = _cpow_by_bits(lbr_c, lbi_c, tau if fwd else (S5_T - 1 - tau), 4)
        wcr, wci = _cmul(per, pei, ctr_ref[d], cti_ref[d])
        ser, sei = _cpow_by_bits(lbr_c, lbi_c, (tau + 1) if fwd else (S5_T - tau), 5)
        csr, csi = _cmul(ser, sei, ctr_ref[d], cti_ref[d])
        tr, ti = lbr, lbi
        for _ in range(int(math.log2(S5_T))):
            tr, ti = _cmul(tr, ti, tr, ti)

        for gg in range(2):
            lm = lane_g == gg
            rm = row_g == gg
            strip = (jnp.dot(jnp.where(lm, bbr, 0.0), wcr, preferred_element_type=F32,
                             precision=lax.Precision.HIGHEST)
                     - jnp.dot(jnp.where(lm, bbi, 0.0), wci, preferred_element_type=F32,
                               precision=lax.Precision.HIGHEST))
            if fwd:
                sext[:, 0:S5_TC] = jnp.zeros((SSM_GROUP, S5_TC), F32)
                sext[:, S5_TC:2 * S5_TC] = strip
                for j in range(S5_T):
                    lo = S5_TC - SSM_GROUP * j
                    toep[j * SSM_GROUP:(j + 1) * SSM_GROUP, :] = sext[:, lo:lo + S5_TC].astype(BF16)
            else:
                sext[:, 0:S5_TC] = strip
                sext[:, S5_TC:2 * S5_TC] = jnp.zeros((SSM_GROUP, S5_TC), F32)
                for j in range(S5_T):
                    lo = SSM_GROUP * (S5_T - 1 - j)
                    toep[j * SSM_GROUP:(j + 1) * SSM_GROUP, :] = sext[:, lo:lo + S5_TC].astype(BF16)
            x = x_ref[gg]
            yi = jnp.dot(x, toep[...], preferred_element_type=F32)
            if fwd:
                y_ref[gg] = yi
            else:
                y_ref[gg] += yi
            z_r = jnp.dot(x, jnp.where(lm, bsr, 0.0).astype(BF16), preferred_element_type=F32)
            z_i = jnp.dot(x, jnp.where(lm, bsi, 0.0).astype(BF16), preferred_element_type=F32)
            if gg == 0:
                zre[...] = z_r
                zim[...] = z_i
            else:
                zre[...] += z_r
                zim[...] += z_i

        def body(n, carry):
            if fwd:
                ch = jnp.where(n < nctx, nlat + n, n - nctx)
            else:
                ch = nchunk - 1 - n
            new = []
            for b in range(nbatch):
                h_r, h_i = carry[2 * b], carry[2 * b + 1]
                r = b * nchunk + ch
                hre[pl.ds(r, 1), :] = h_r
                him[pl.ds(r, 1), :] = h_i
                n_r, n_i = _cmul(tr, ti, h_r, h_i)
                new.append(n_r + zre[pl.ds(r, 1), :])
                new.append(n_i + zim[pl.ds(r, 1), :])
            return tuple(new)

        zero = jnp.zeros((1, PAIR), F32)
        lax.fori_loop(0, nchunk, body, (zero,) * (2 * nbatch), unroll=8)

        h_r = hre[...].astype(BF16)
        h_i = him[...].astype(BF16)
        for gg in range(2):
            rm = row_g == gg
            y_ref[gg] += (jnp.dot(h_r, jnp.where(rm, csr, 0.0).astype(BF16), preferred_element_type=F32)
                          - jnp.dot(h_i, jnp.where(rm, csi, 0.0).astype(BF16), preferred_element_type=F32))


def _s5(xg, a_re, a_im, log_dt, b_re, b_im, c_re, c_im, nbatch):
    g, nr, _ = xg.shape
    gp = g // 2
    nchunk = nr // nbatch
    pair_row = lambda a: a.reshape(2, gp, 1, PAIR)
    pair_col = lambda a: a.reshape(2, gp, PAIR, 1)
    ldt = jnp.broadcast_to(log_dt[..., None], a_re.shape)
    bt = lambda a: jnp.transpose(a.reshape(2, gp, 2, SSM_STATE, SSM_GROUP), (0, 1, 4, 2, 3)).reshape(
        2, gp, SSM_GROUP, PAIR)
    ct = lambda a: jnp.tile(jnp.transpose(a.reshape(2, gp, 2, SSM_GROUP, SSM_STATE), (0, 1, 2, 4, 3)).reshape(
        2, gp, PAIR, SSM_GROUP), (1, 1, 1, S5_T))
    blk = lambda *s: pl.BlockSpec((2, None) + s, lambda i: (0, i, 0, 0))
    kern = functools.partial(_s5_kernel, nchunk=nchunk, nbatch=nbatch, nctx=CTX_LEN // S5_T)
    return pl.pallas_call(
        kern,
        grid=(gp,),
        in_specs=[pl.BlockSpec((2, nr, S5_TC), lambda i: (i, 0, 0)),
                  blk(1, PAIR), blk(1, PAIR), blk(1, PAIR),
                  blk(PAIR, 1), blk(PAIR, 1), blk(PAIR, 1),
                  blk(SSM_GROUP, PAIR), blk(SSM_GROUP, PAIR),
                  blk(PAIR, S5_TC), blk(PAIR, S5_TC)],
        out_specs=pl.BlockSpec((2, nr, S5_TC), lambda i: (i, 0, 0)),
        out_shape=jax.ShapeDtypeStruct((g, nr, S5_TC), F32),
        scratch_shapes=[pltpu.VMEM((SSM_GROUP, 2 * S5_TC), F32),
                        pltpu.VMEM((S5_TC, S5_TC), BF16),
                        pltpu.VMEM((nr, PAIR), F32),
                        pltpu.VMEM((nr, PAIR), F32),
                        pltpu.VMEM((nr, PAIR), F32),
                        pltpu.VMEM((nr, PAIR), F32)],
        compiler_params=_cparams(("parallel",)),
        name="s5_scan",
    )(xg, pair_row(a_re), pair_row(a_im), pair_row(ldt), pair_col(a_re), pair_col(a_im), pair_col(ldt),
      bt(b_re), bt(b_im), ct(c_re), ct(c_im))


def _merge_kernel(x_ref, on_ref, yg_ref, u_ref, gate_ref, mod_ref, d_ref, wglu_ref, bglu_ref,
                  wbs_ref, wba_ref, wout_ref, g1_ref, o_ref, ysc):
    ys = _chunks_to_rows(yg_ref, ysc) + u_ref[...] * d_ref[...]
    gl = jax.nn.gelu(ys)
    z = gl * jax.nn.sigmoid(jnp.dot(gl.astype(BF16), wglu_ref[...], preferred_element_type=F32)
                            + bglu_ref[...])
    ps = jnp.dot(z.astype(BF16), wbs_ref[...], preferred_element_type=F32)
    pa = jnp.dot(on_ref[...], wba_ref[...], preferred_element_type=F32)
    gate = gate_ref[...].astype(F32)
    mix = gate[:, 0:D_MODEL] * pa + gate[:, D_MODEL:2 * D_MODEL] * ps
    m2 = jnp.dot(mix.astype(BF16), wout_ref[...], preferred_element_type=F32)
    o_ref[...] = x_ref[...] + mod_ref[:, 2 * D_MODEL:3 * D_MODEL] * _rms(m2, g1_ref[...])


def _merge(xcat, o_n, yg, u, gates, modsel, ssm_d, w_glu, b_glu, w_br_s, w_br_a, w_out, g1, nt):
    b, n, _ = xcat.shape
    nt_all = n // TM
    row = lambda bi, i: (bi, i, 0)
    return pl.pallas_call(
        _merge_kernel,
        grid=(b, nt),
        in_specs=[pl.BlockSpec((None, TM, D_MODEL), row),
                  pl.BlockSpec((None, TM, ATTN_W), row),
                  pl.BlockSpec((SSM_GROUPS, TM // S5_T, S5_TC), lambda bi, i: (0, bi * nt_all + i, 0)),
                  pl.BlockSpec((None, TM, SSM_W), row),
                  pl.BlockSpec((None, TM, 2 * D_MODEL), row),
                  _mod_spec(0, nt_all - 1),
                  _const_spec((1, SSM_W)),
                  _const_spec((SSM_W, SSM_W)),
                  _const_spec((1, SSM_W)),
                  _const_spec((SSM_W, D_MODEL)),
                  _const_spec((ATTN_W, D_MODEL)),
                  _const_spec((D_MODEL, D_MODEL)),
                  _const_spec((1, D_MODEL))],
        out_specs=pl.BlockSpec((None, TM, D_MODEL), row),
        out_shape=jax.ShapeDtypeStruct((b, nt * TM, D_MODEL), F32),
        scratch_shapes=[pltpu.VMEM((S5_SLABS, TM, LANES), F32)],
        compiler_params=_cparams(("parallel", "arbitrary")),
        name="merge",
    )(xcat, o_n, yg, u, gates, modsel, ssm_d, w_glu, b_glu, w_br_s, w_br_a, w_out, g1)


def _mlp_kernel(x_ref, mod_ref, g2_ref, w1_ref, b1_ref, w2_ref, b2_ref, g3_ref, o_ref):
    x = x_ref[...]
    h = _rms(x, g2_ref[...]) * (1.0 + mod_ref[:, 4 * D_MODEL:5 * D_MODEL]) \
        + mod_ref[:, 3 * D_MODEL:4 * D_MODEL]
    f = jnp.dot(h.astype(BF16), w1_ref[...], preferred_element_type=F32) + b1_ref[...]
    f = jnp.square(jnp.maximum(f, 0.0))
    o = jnp.dot(f.astype(BF16), w2_ref[...], preferred_element_type=F32) + b2_ref[...]
    o_ref[...] = x + mod_ref[:, 5 * D_MODEL:6 * D_MODEL] * _rms(o, g3_ref[...])


def _mlp(x1, modsel, g2, w1, b1, w2, b2, g3, nt_all):
    b, n, _ = x1.shape
    nt = n // TM
    row = lambda bi, i: (bi, i, 0)
    return pl.pallas_call(
        _mlp_kernel,
        grid=(b, nt),
        in_specs=[pl.BlockSpec((None, TM, D_MODEL), row),
                  _mod_spec(0, nt_all - 1),
                  _const_spec((1, D_MODEL)),
                  _const_spec((D_MODEL, D_FF)),
                  _const_spec((1, D_FF)),
                  _const_spec((D_FF, D_MODEL)),
                  _const_spec((1, D_MODEL)),
                  _const_spec((1, D_MODEL))],
        out_specs=pl.BlockSpec((None, TM, D_MODEL), row),
        out_shape=jax.ShapeDtypeStruct((b, n, D_MODEL), F32),
        compiler_params=_cparams(("parallel", "arbitrary")),
        name="mlp",
    )(x1, modsel, g2, w1, b1, w2, b2, g3)


def _rope_tables(n_tokens):
    rows = n_tokens // GRID_W
    row = jnp.repeat(jnp.arange(rows, dtype=jnp.int32), GRID_W).astype(F32)
    col = jnp.tile(jnp.arange(GRID_W, dtype=jnp.int32), rows).astype(F32)
    inv_freq = ROPE_BASE ** (-jnp.arange(N_FREQ, dtype=F32) / N_FREQ)
    ang = jnp.stack([row[:, None] * inv_freq, col[:, None] * inv_freq], axis=1)
    cos, sin = jnp.cos(ang), jnp.sin(ang)
    cos_l = jnp.tile(jnp.concatenate([cos[:, 0], cos[:, 0], cos[:, 1], cos[:, 1]], axis=1), (1, 2))
    sin_l = jnp.tile(jnp.concatenate([-sin[:, 0], sin[:, 0], -sin[:, 1], sin[:, 1]], axis=1), (1, 2))
    cos_t = jnp.concatenate([cos_l, jnp.ones((CTX_LEN, LANES), F32)], axis=0)
    sin_t = jnp.concatenate([sin_l, jnp.zeros((CTX_LEN, LANES), F32)], axis=0)
    return cos_t, sin_t


def kernel(x, c, ctx, c_ctx, ada_w, ada_b, norm_g, w_in, gate_b, lam_qk, subln_g, w_br_a,
           ssm_a_re, ssm_a_im, ssm_b_re, ssm_b_im, ssm_c_re, ssm_c_im, ssm_log_dt, ssm_d,
           w_glu, b_glu, w_br_s, w_out, w_mlp1, b_mlp1, w_mlp2, b_mlp2):
    b, seq, _ = x.shape
    n = seq + CTX_LEN
    assert b + 1 <= 8 and ctx.shape[1] == CTX_LEN == TM and seq % TQ == 0 and n % TK == 0
    cos_t, sin_t = _rope_tables(seq)
    xcat = jnp.concatenate([x, ctx], axis=1)
    cc = jnp.zeros((8, D_MODEL), F32).at[:b].set(c).at[b].set(c_ctx)
    nt_all = n // TM
    for i in range(DEPTH):
        last = i == DEPTH - 1
        lam_init = 0.8 - 0.6 * math.exp(-0.3 * i)
        mod = _modulation(cc, ada_w[i], ada_b[i])
        modsel = jnp.stack([mod[:b], jnp.broadcast_to(mod[b], (b, 6 * D_MODEL))], axis=1)
        modsel = modsel.reshape(b, 2, 1, 6 * D_MODEL)
        ng = norm_g[i]
        q, k, v, u, gates, xg = _in_proj(xcat, modsel, ng[0].reshape(1, -1), w_in[i].astype(BF16),
                                         gate_b[i].reshape(1, -1), cos_t, sin_t)
        o_n = _attention(q, k, v, lam_qk[i], subln_g[i], lam_init,
                         tq=TQ, tk=TK, q0=0, nq=seq // TQ, k0=0, nk=n // TK)
        if not last:
            o_n = _attention(q, k, v, lam_qk[i], subln_g[i], lam_init,
                             tq=CTX_LEN, tk=CTX_LEN, q0=seq // CTX_LEN, nq=1, k0=seq // CTX_LEN, nk=1,
                             o_prev=o_n)
        yg = _s5(xg, ssm_a_re[i], ssm_a_im[i], ssm_log_dt[i], ssm_b_re[i], ssm_b_im[i],
                 ssm_c_re[i], ssm_c_im[i], b)
        x1 = _merge(xcat, o_n, yg, u, gates, modsel, ssm_d[i].reshape(1, -1),
                    w_glu[i].astype(BF16), b_glu[i].reshape(1, -1), w_br_s[i].astype(BF16),
                    w_br_a[i].astype(BF16), w_out[i].astype(BF16), ng[1].reshape(1, -1),
                    nt_all - 1 if last else nt_all)
        xcat = _mlp(x1, modsel, ng[2].reshape(1, -1), w_mlp1[i].astype(BF16),
                    b_mlp1[i].reshape(1, -1), w_mlp2[i].astype(BF16), b_mlp2[i].reshape(1, -1),
                    ng[3].reshape(1, -1), nt_all)
    return xcat
```

```python
import functools
import math

import jax
import jax.numpy as jnp
from jax import lax
from jax.experimental import pallas as pl
from jax.experimental.pallas import tpu as pltpu

F32 = jnp.float32
BF16 = jnp.bfloat16

D_MODEL = 1024
DEPTH = 2
GRID_W = 64
CTX_LEN = 256
N_HEADS = 4
HEAD_DIM = 64
V_DIM = 128
QK_W = N_HEADS * 2 * HEAD_DIM
ATTN_W = 512
SSM_W = 512
SSM_GROUP = 16
SSM_GROUPS = 32
SSM_STATE = 64
D_FF = 4096
IN_COLS = 4096
N_FREQ = 16
ROPE_BASE = 10000.0
EPS = 1e-6
LOG2E = 1.4426950408889634

LANES = 128
TM = 256
TQ = 512
TK = 2816
S5_T = 16
S5_TC = S5_T * SSM_GROUP
S5_SLABS = SSM_W // LANES
S5_GPS = LANES // SSM_GROUP
PAIR = 2 * SSM_STATE
VMEM_LIMIT = 56 * 1024 * 1024

NEG_BIG = -1e30
BOUND_SLACK = 1.03
BOUND_LIMIT = 50.0


def _cparams(sem):
    return pltpu.CompilerParams(dimension_semantics=sem, vmem_limit_bytes=VMEM_LIMIT)


def _const_spec(shape):
    nd = len(shape)
    return pl.BlockSpec(shape, lambda *_: (0,) * nd, pipeline_mode=pl.Buffered(1))


def _mod_spec(roff, ctx_block):
    return pl.BlockSpec((None, None, 1, 6 * D_MODEL),
                        lambda bi, i: (bi, (i + roff) // ctx_block, 0, 0))


def _rms(x, g):
    ms = jnp.mean(x * x, axis=-1, keepdims=True)
    return x * lax.rsqrt(ms + EPS) * g


def _mod_kernel(c_ref, w_ref, b_ref, o_ref):
    c = c_ref[...]
    s = c * jax.nn.sigmoid(c)
    o_ref[...] = jnp.dot(s.astype(BF16), w_ref[...].astype(BF16),
                         preferred_element_type=F32) + b_ref[...]


def _modulation(cc, ada_w, ada_b):
    n = ada_w.shape[1]
    bn = 1024
    return pl.pallas_call(
        _mod_kernel,
        grid=(n // bn,),
        in_specs=[pl.BlockSpec((8, D_MODEL), lambda j: (0, 0)),
                  pl.BlockSpec((D_MODEL, bn), lambda j: (0, j)),
                  pl.BlockSpec((1, bn), lambda j: (0, j))],
        out_specs=pl.BlockSpec((8, bn), lambda j: (0, j)),
        out_shape=jax.ShapeDtypeStruct((8, n), F32),
        compiler_params=_cparams(("arbitrary",)),
        name="adaln_mod",
    )(cc, ada_w, ada_b.reshape(1, n))


def _lane_window(off):
    lane = lax.broadcasted_iota(jnp.int32, (S5_T, LANES), 1)
    return (lane >= off) & (lane < off + SSM_GROUP)


def _rows_to_chunks(u, usc, xg_ref):
    for sl in range(S5_SLABS):
        usc[sl] = u[:, sl * LANES:(sl + 1) * LANES]
    for sl in range(S5_SLABS):
        acc = [[None] * (S5_TC // LANES) for _ in range(S5_GPS)]
        for t in range(S5_T):
            v = usc[sl, pl.ds(t, TM // S5_T, stride=S5_T), :]
            lt, off = divmod(t * SSM_GROUP, LANES)
            win = _lane_window(off)
            for gl in range(S5_GPS):
                shift = (off - gl * SSM_GROUP) % LANES
                moved = pltpu.roll(v, shift, 1) if shift else v
                prev = acc[gl][lt]
                acc[gl][lt] = jnp.where(win, moved, 0.0 if prev is None else prev)
        for gl in range(S5_GPS):
            for lt in range(S5_TC // LANES):
                xg_ref[sl * S5_GPS + gl, :, lt * LANES:(lt + 1) * LANES] = acc[gl][lt].astype(xg_ref.dtype)


def _chunks_to_rows(yg_ref, ysc):
    for sl in range(S5_SLABS):
        tiles = [[yg_ref[sl * S5_GPS + gl, :, lt * LANES:(lt + 1) * LANES]
                  for lt in range(S5_TC // LANES)] for gl in range(S5_GPS)]
        for t in range(S5_T):
            lt, off = divmod(t * SSM_GROUP, LANES)
            v = None
            for gl in range(S5_GPS):
                shift = (gl * SSM_GROUP - off) % LANES
                src = tiles[gl][lt]
                moved = pltpu.roll(src, shift, 1) if shift else src
                v = jnp.where(_lane_window(gl * SSM_GROUP), moved, 0.0 if v is None else v)
            ysc[sl, pl.ds(t, TM // S5_T, stride=S5_T), :] = v
    return jnp.concatenate([ysc[sl] for sl in range(S5_SLABS)], axis=1)


def _map_segments():
    li = lax.broadcasted_iota(jnp.int32, (LANES, LANES), 0) // HEAD_DIM
    lj = lax.broadcasted_iota(jnp.int32, (LANES, LANES), 1) // HEAD_DIM
    return (li == lj).astype(BF16)


def _in_kernel(x_ref, mod_ref, g_ref, w_ref, gb_ref, cos_ref, sin_ref,
               q_ref, k_ref, v_ref, u_ref, gate_ref, xg_ref, nrm_ref, usc):
    x = x_ref[...]
    h = _rms(x, g_ref[...]) * (1.0 + mod_ref[:, D_MODEL:2 * D_MODEL]) + mod_ref[:, 0:D_MODEL]
    hb = h.astype(BF16)
    cos = cos_ref[...]
    sin = sin_ref[...]
    lane = lax.broadcasted_iota(jnp.int32, (TM, LANES), 1)
    first_half = (lane & 31) < 16
    seg = _map_segments()

    def rope_store(col0, out_ref, scale, nrm_row):
        t = jnp.dot(hb, w_ref[:, col0:col0 + QK_W], preferred_element_type=F32)
        for j in range(QK_W // LANES):
            tj = t[:, j * LANES:(j + 1) * LANES]
            partner = jnp.where(first_half, pltpu.roll(tj, LANES - 16, 1), pltpu.roll(tj, 16, 1))
            r = tj * cos + partner * sin
            if scale != 1.0:
                r = r * scale
            out_ref[:, j * LANES:(j + 1) * LANES] = r.astype(out_ref.dtype)
            n2 = jnp.dot((r * r).astype(BF16), seg, preferred_element_type=F32)
            nrm_ref[nrm_row:nrm_row + 1, j * LANES:(j + 1) * LANES] = jnp.max(n2, axis=0, keepdims=True)

    rope_store(0, q_ref, HEAD_DIM ** -0.5 * LOG2E, 0)
    rope_store(QK_W, k_ref, 1.0, 1)
    v_ref[...] = jnp.dot(hb, w_ref[:, 1024:1536], preferred_element_type=F32).astype(v_ref.dtype)
    u = jnp.dot(hb, w_ref[:, 1536:2048], preferred_element_type=F32)
    u_ref[...] = u
    _rows_to_chunks(u, usc, xg_ref)
    g = jnp.dot(hb, w_ref[:, 2048:IN_COLS], preferred_element_type=F32) + gb_ref[...]
    gate_ref[...] = jax.nn.sigmoid(g).astype(gate_ref.dtype)


def _in_proj(xcat, modsel, g0, w_in_b, gate_b, cos_t, sin_t):
    b, n, _ = xcat.shape
    nt = n // TM
    row = lambda bi, i: (bi, i, 0)
    return pl.pallas_call(
        _in_kernel,
        grid=(b, nt),
        in_specs=[pl.BlockSpec((None, TM, D_MODEL), row),
                  _mod_spec(0, nt - 1),
                  _const_spec((1, D_MODEL)),
                  _const_spec((D_MODEL, IN_COLS)),
                  _const_spec((1, 2 * D_MODEL)),
                  pl.BlockSpec((TM, LANES), lambda bi, i: (i, 0)),
                  pl.BlockSpec((TM, LANES), lambda bi, i: (i, 0))],
        out_specs=[pl.BlockSpec((None, TM, QK_W), row),
                   pl.BlockSpec((None, TM, QK_W), row),
                   pl.BlockSpec((None, TM, ATTN_W), row),
                   pl.BlockSpec((None, TM, SSM_W), row),
                   pl.BlockSpec((None, TM, 2 * D_MODEL), row),
                   pl.BlockSpec((SSM_GROUPS, TM // S5_T, S5_TC), lambda bi, i: (0, bi * nt + i, 0)),
                   pl.BlockSpec((None, None, 2, QK_W), lambda bi, i: (bi, i, 0, 0))],
        out_shape=[jax.ShapeDtypeStruct((b, n, QK_W), BF16),
                   jax.ShapeDtypeStruct((b, n, QK_W), BF16),
                   jax.ShapeDtypeStruct((b, n, ATTN_W), BF16),
                   jax.ShapeDtypeStruct((b, n, SSM_W), F32),
                   jax.ShapeDtypeStruct((b, n, 2 * D_MODEL), BF16),
                   jax.ShapeDtypeStruct((SSM_GROUPS, b * n // S5_T, S5_TC), BF16),
                   jax.ShapeDtypeStruct((b, nt, 2, QK_W), F32)],
        scratch_shapes=[pltpu.VMEM((S5_SLABS, TM, LANES), F32)],
        compiler_params=_cparams(("parallel", "arbitrary")),
        name="in_proj",
    )(xcat, modsel, g0, w_in_b, gate_b, cos_t, sin_t)


def _attn_kernel(lamqk_ref, subg_ref, q_ref, k_ref, v_ref, o_ref,
                 qm_sc, m_sc, l_sc, acc_sc, *, lam_init, nk):
    ki = pl.program_id(3)

    @pl.when(ki == 0)
    def _init():
        q = q_ref[...]
        lane = lax.broadcasted_iota(jnp.int32, q.shape, 1)
        zero = jnp.zeros_like(q)
        qm_sc[0] = jnp.where(lane < HEAD_DIM, q, zero)
        qm_sc[1] = jnp.where(lane >= HEAD_DIM, q, zero)
        m_sc[...] = jnp.full(m_sc.shape, NEG_BIG, F32)
        l_sc[...] = jnp.zeros(l_sc.shape, F32)
        acc_sc[...] = jnp.zeros(acc_sc.shape, F32)

    k = k_ref[...]
    v = v_ref[...]
    for c in range(2):
        s = lax.dot_general(qm_sc[c], k, (((1,), (1,)), ((), ())), preferred_element_type=F32)
        m_prev = m_sc[c]
        m_new = jnp.maximum(m_prev, jnp.max(s, axis=1, keepdims=True))
        alpha = jnp.exp2(m_prev - m_new)
        p = jnp.exp2(s - m_new)
        l_sc[c] = alpha * l_sc[c] + jnp.sum(p, axis=1, keepdims=True)
        acc_sc[c] = alpha * acc_sc[c] + jnp.dot(p.astype(BF16), v, preferred_element_type=F32)
        m_sc[c] = m_new

    @pl.when(ki == nk - 1)
    def _fin():
        lq = lamqk_ref[...]
        lam = (jnp.exp(jnp.sum(lq[0:1] * lq[1:2], axis=1, keepdims=True))
               - jnp.exp(jnp.sum(lq[2:3] * lq[3:4], axis=1, keepdims=True)) + lam_init)
        o = acc_sc[0] / l_sc[0] - lam * (acc_sc[1] / l_sc[1])
        o_ref[...] = (_rms(o, subg_ref[...]) * (1.0 - lam_init)).astype(o_ref.dtype)


def _attn_bounded_kernel(lamqk_ref, subg_ref, kmax_ref, q_ref, k_ref, v_ref, o_ref,
                         qx_sc, kx_sc, l_sc, acc_sc, *, lam_init, nk):
    ki = pl.program_id(3)
    tq = q_ref.shape[0]
    tk = k_ref.shape[0]

    @pl.when(ki == 0)
    def _init():
        q = q_ref[...]
        lane = lax.broadcasted_iota(jnp.int32, (tq, LANES), 1)
        qf = q.astype(F32)
        n2 = jnp.dot((qf * qf).astype(BF16), _map_segments(), preferred_element_type=F32)
        bound = jnp.sqrt(n2) * kmax_ref[...] * BOUND_SLACK
        zero = jnp.zeros_like(q)
        qx_sc[0, :, 0:LANES] = jnp.where(lane < HEAD_DIM, q, zero)
        qx_sc[0, :, LANES:2 * LANES] = jnp.where(lane == 0, -bound, 0.0).astype(BF16)
        qx_sc[1, :, 0:LANES] = jnp.where(lane >= HEAD_DIM, q, zero)
        qx_sc[1, :, LANES:2 * LANES] = jnp.where(lane == 0, -pltpu.roll(bound, HEAD_DIM, 1), 0.0).astype(BF16)
        lane_k = lax.broadcasted_iota(jnp.int32, (tk, LANES), 1)
        kx_sc[:, LANES:2 * LANES] = jnp.where(lane_k == 0, 1.0, 0.0).astype(BF16)
        l_sc[...] = jnp.zeros(l_sc.shape, F32)
        acc_sc[...] = jnp.zeros(acc_sc.shape, F32)

    kx_sc[:, 0:LANES] = k_ref[...]
    kx = kx_sc[...]
    v = v_ref[...]
    for c in range(2):
        s = lax.dot_general(qx_sc[c], kx, (((1,), (1,)), ((), ())), preferred_element_type=F32)
        p = jnp.exp2(s)
        part = p[:, 0:LANES]
        for j in range(1, tk // LANES):
            part = part + p[:, j * LANES:(j + 1) * LANES]
        l_sc[c] += part
        acc_sc[c] += jnp.dot(p.astype(BF16), v, preferred_element_type=F32)

    @pl.when(ki == nk - 1)
    def _fin():
        lq = lamqk_ref[...]
        lam = (jnp.exp(jnp.sum(lq[0:1] * lq[1:2], axis=1, keepdims=True))
               - jnp.exp(jnp.sum(lq[2:3] * lq[3:4], axis=1, keepdims=True)) + lam_init)
        l0 = jnp.sum(l_sc[0], axis=1, keepdims=True)
        l1 = jnp.sum(l_sc[1], axis=1, keepdims=True)
        o = acc_sc[0] / l0 - lam * (acc_sc[1] / l1)
        o_ref[...] = (_rms(o, subg_ref[...]) * (1.0 - lam_init)).astype(o_ref.dtype)


def _attention(q, k, v, kmax, lam_qk, subln_g, lam_init, *, bounded, tq, tk, q0, nq, k0, nk, o_prev=None):
    b, n, _ = q.shape
    q_idx = lambda bi, h, qi, ki: (bi, qi + q0, h)
    kv_idx = lambda bi, h, qi, ki: (bi, ki + k0, h)
    in_specs = [pl.BlockSpec((4, HEAD_DIM), lambda *_: (0, 0)),
                pl.BlockSpec((1, V_DIM), lambda *_: (0, 0)),
                pl.BlockSpec((None, None, 1, LANES), lambda bi, h, qi, ki: (bi, h, 0, 0)),
                pl.BlockSpec((None, tq, LANES), q_idx),
                pl.BlockSpec((None, tk, LANES), kv_idx),
                pl.BlockSpec((None, tk, LANES), kv_idx)]
    args = [lam_qk, subln_g.reshape(1, V_DIM), kmax, q, k, v]
    if bounded:
        kern = functools.partial(_attn_bounded_kernel, lam_init=lam_init, nk=nk)
        scratch = [pltpu.VMEM((2, tq, 2 * LANES), BF16),
                   pltpu.VMEM((tk, 2 * LANES), BF16),
                   pltpu.VMEM((2, tq, LANES), F32),
                   pltpu.VMEM((2, tq, V_DIM), F32)]
    else:
        kern = functools.partial(_drop_input, functools.partial(_attn_kernel, lam_init=lam_init, nk=nk), 2)
        scratch = [pltpu.VMEM((2, tq, LANES), BF16),
                   pltpu.VMEM((2, tq, 1), F32),
                   pltpu.VMEM((2, tq, 1), F32),
                   pltpu.VMEM((2, tq, V_DIM), F32)]
    aliases = {}
    if o_prev is not None:
        in_specs.append(pl.BlockSpec(memory_space=pl.ANY))
        args.append(o_prev)
        aliases = {6: 0}
        kern = functools.partial(_drop_input, kern, 6)
    return pl.pallas_call(
        kern,
        grid=(b, N_HEADS, nq, nk),
        in_specs=in_specs,
        out_specs=pl.BlockSpec((None, tq, LANES), q_idx),
        out_shape=jax.ShapeDtypeStruct((b, n, ATTN_W), BF16),
        scratch_shapes=scratch,
        input_output_aliases=aliases,
        compiler_params=_cparams(("parallel", "parallel", "parallel", "arbitrary")),
        name="diff_attn_bounded" if bounded else "diff_attn",
    )(*args)


def _attention_layer(q, k, v, nrm, lam_qk, subln_g, lam_init, seq, with_ctx):
    b, n, _ = q.shape
    kmax = jnp.sqrt(jnp.max(nrm[:, :, 1], axis=1))
    qmax = jnp.sqrt(jnp.max(nrm[:, :, 0], axis=1))
    small = jnp.max(qmax * kmax) * BOUND_SLACK < BOUND_LIMIT
    kmax4 = kmax.reshape(b, N_HEADS, 1, LANES)

    def run(bounded):
        o_n = _attention(q, k, v, kmax4, lam_qk, subln_g, lam_init, bounded=bounded,
                         tq=TQ, tk=TK, q0=0, nq=seq // TQ, k0=0, nk=n // TK)
        if with_ctx:
            o_n = _attention(q, k, v, kmax4, lam_qk, subln_g, lam_init, bounded=bounded,
                             tq=CTX_LEN, tk=CTX_LEN, q0=seq // CTX_LEN, nq=1, k0=seq // CTX_LEN, nk=1,
                             o_prev=o_n)
        return o_n

    return lax.cond(small, lambda: run(True), lambda: run(False))


def _drop_input(kern, idx, *refs):
    return kern(*refs[:idx], *refs[idx + 1:])


def _cmul(ar, ai, br, bi):
    return ar * br - ai * bi, ar * bi + ai * br


def _cpow_by_bits(lr, li, expo, nbits):
    pr = jnp.ones(expo.shape, F32)
    pi = jnp.zeros(expo.shape, F32)
    br, bi = lr, li
    for bit in range(nbits):
        on = ((expo >> bit) & 1) == 1
        fr = jnp.where(on, br, 1.0)
        fi = jnp.where(on, bi, 0.0)
        pr, pi = _cmul(pr, pi, fr, fi)
        if bit + 1 < nbits:
            br, bi = _cmul(br, bi, br, bi)
    return pr, pi


def _discretize(ar, ai, log_dt):
    dt = jnp.exp(log_dt)
    er = jnp.exp(ar * dt)
    lbr = er * jnp.cos(ai * dt)
    lbi = er * jnp.sin(ai * dt)
    den = ar * ar + ai * ai
    nr = lbr - 1.0
    cr = (nr * ar + lbi * ai) / den
    ci = (lbi * ar - nr * ai) / den
    return lbr, lbi, cr, ci


def _s5_kernel(x_ref, ar_r, ai_r, ldt_r, ar_c, ai_c, ldt_c, btr_ref, bti_ref, ctr_ref, cti_ref,
               y_ref, sext, toep, zre, zim, hre, him, *, nchunk, nbatch, nctx):
    nlat = nchunk - nctx
    rowj = lax.broadcasted_iota(jnp.int32, (S5_TC, PAIR), 0) // SSM_GROUP
    tau = lax.broadcasted_iota(jnp.int32, (PAIR, S5_TC), 1) // SSM_GROUP
    lane_g = lax.broadcasted_iota(jnp.int32, (1, PAIR), 1) // SSM_STATE
    row_g = lax.broadcasted_iota(jnp.int32, (PAIR, 1), 0) // SSM_STATE

    for d in range(2):
        fwd = d == 0
        lbr, lbi, cr, ci = _discretize(ar_r[d], ai_r[d], ldt_r[d])
        lbr_c, lbi_c, _, _ = _discretize(ar_c[d], ai_c[d], ldt_c[d])
        bbr, bbi = _cmul(cr, ci, btr_ref[d], bti_ref[d])
        bbr_t = jnp.concatenate([bbr] * S5_T, axis=0)
        bbi_t = jnp.concatenate([bbi] * S5_T, axis=0)
        pwr, pwi = _cpow_by_bits(lbr, lbi, (S5_T - 1 - rowj) if fwd else rowj, 4)
        bsr, bsi = _cmul(pwr, pwi, bbr_t, bbi_t)
        per, pei = _cpow_by_bits(lbr_c, lbi_c, tau if fwd else (S5_T - 1 - tau), 4)
        wcr, wci = _cmul(per, pei, ctr_ref[d], cti_ref[d])
        ser, sei = _cpow_by_bits(lbr_c, lbi_c, (tau + 1) if fwd else (S5_T - tau), 5)
        csr, csi = _cmul(ser, sei, ctr_ref[d], cti_ref[d])
        tr, ti = lbr, lbi
        for _ in range(int(math.log2(S5_T))):
            tr, ti = _cmul(tr, ti, tr, ti)

        for gg in range(2):
            lm = lane_g == gg
            rm = row_g == gg
            strip = (jnp.dot(jnp.where(lm, bbr, 0.0), wcr, preferred_element_type=F32,
                             precision=lax.Precision.HIGHEST)
                     - jnp.dot(jnp.where(lm, bbi, 0.0), wci, preferred_element_type=F32,
                               precision=lax.Precision.HIGHEST))
            if fwd:
                sext[:, 0:S5_TC] = jnp.zeros((SSM_GROUP, S5_TC), F32)
                sext[:, S5_TC:2 * S5_TC] = strip
                for j in range(S5_T):
                    lo = S5_TC - SSM_GROUP * j
                    toep[j * SSM_GROUP:(j + 1) * SSM_GROUP, :] = sext[:, lo:lo + S5_TC].astype(BF16)
            else:
                sext[:, 0:S5_TC] = strip
                sext[:, S5_TC:2 * S5_TC] = jnp.zeros((SSM_GROUP, S5_TC), F32)
                for j in range(S5_T):
                    lo = SSM_GROUP * (S5_T - 1 - j)
                    toep[j * SSM_GROUP:(j + 1) * SSM_GROUP, :] = sext[:, lo:lo + S5_TC].astype(BF16)
            x = x_ref[gg]
            yi = jnp.dot(x, toep[...], preferred_element_type=F32)
            if fwd:
                y_ref[gg] = yi
            else:
                y_ref[gg] += yi
            z_r = jnp.dot(x, jnp.where(lm, bsr, 0.0).astype(BF16), preferred_element_type=F32)
            z_i = jnp.dot(x, jnp.where(lm, bsi, 0.0).astype(BF16), preferred_element_type=F32)
            if gg == 0:
                zre[...] = z_r
                zim[...] = z_i
            else:
                zre[...] += z_r
                zim[...] += z_i

        def body(n, carry):
            if fwd:
                ch = jnp.where(n < nctx, nlat + n, n - nctx)
            else:
                ch = nchunk - 1 - n
            new = []
            for b in range(nbatch):
                h_r, h_i = carry[2 * b], carry[2 * b + 1]
                r = b * nchunk + ch
                hre[pl.ds(r, 1), :] = h_r
                him[pl.ds(r, 1), :] = h_i
                n_r, n_i = _cmul(tr, ti, h_r, h_i)
                new.append(n_r + zre[pl.ds(r, 1), :])
                new.append(n_i + zim[pl.ds(r, 1), :])
            return tuple(new)

        zero = jnp.zeros((1, PAIR), F32)
        lax.fori_loop(0, nchunk, body, (zero,) * (2 * nbatch), unroll=8)

        h_r = hre[...].astype(BF16)
        h_i = him[...].astype(BF16)
        for gg in range(2):
            rm = row_g == gg
            y_ref[gg] += (jnp.dot(h_r, jnp.where(rm, csr, 0.0).astype(BF16), preferred_element_type=F32)
                          - jnp.dot(h_i, jnp.where(rm, csi, 0.0).astype(BF16), preferred_element_type=F32))


def _s5(xg, a_re, a_im, log_dt, b_re, b_im, c_re, c_im, nbatch):
    g, nr, _ = xg.shape
    gp = g // 2
    nchunk = nr // nbatch
    pair_row = lambda a: a.reshape(2, gp, 1, PAIR)
    pair_col = lambda a: a.reshape(2, gp, PAIR, 1)
    ldt = jnp.broadcast_to(log_dt[..., None], a_re.shape)
    bt = lambda a: jnp.transpose(a.reshape(2, gp, 2, SSM_STATE, SSM_GROUP), (0, 1, 4, 2, 3)).reshape(
        2, gp, SSM_GROUP, PAIR)
    ct = lambda a: jnp.tile(jnp.transpose(a.reshape(2, gp, 2, SSM_GROUP, SSM_STATE), (0, 1, 2, 4, 3)).reshape(
        2, gp, PAIR, SSM_GROUP), (1, 1, 1, S5_T))
    blk = lambda *s: pl.BlockSpec((2, None) + s, lambda i: (0, i, 0, 0))
    kern = functools.partial(_s5_kernel, nchunk=nchunk, nbatch=nbatch, nctx=CTX_LEN // S5_T)
    return pl.pallas_call(
        kern,
        grid=(gp,),
        in_specs=[pl.BlockSpec((2, nr, S5_TC), lambda i: (i, 0, 0)),
                  blk(1, PAIR), blk(1, PAIR), blk(1, PAIR),
                  blk(PAIR, 1), blk(PAIR, 1), blk(PAIR, 1),
                  blk(SSM_GROUP, PAIR), blk(SSM_GROUP, PAIR),
                  blk(PAIR, S5_TC), blk(PAIR, S5_TC)],
        out_specs=pl.BlockSpec((2, nr, S5_TC), lambda i: (i, 0, 0)),
        out_shape=jax.ShapeDtypeStruct((g, nr, S5_TC), F32),
        scratch_shapes=[pltpu.VMEM((SSM_GROUP, 2 * S5_TC), F32),
                        pltpu.VMEM((S5_TC, S5_TC), BF16),
                        pltpu.VMEM((nr, PAIR), F32),
                        pltpu.VMEM((nr, PAIR), F32),
                        pltpu.VMEM((nr, PAIR), F32),
                        pltpu.VMEM((nr, PAIR), F32)],
        compiler_params=_cparams(("parallel",)),
        name="s5_scan",
    )(xg, pair_row(a_re), pair_row(a_im), pair_row(ldt), pair_col(a_re), pair_col(a_im), pair_col(ldt),
      bt(b_re), bt(b_im), ct(c_re), ct(c_im))


def _merge_kernel(x_ref, on_ref, yg_ref, u_ref, gate_ref, mod_ref, d_ref, wglu_ref, bglu_ref,
                  wbs_ref, wba_ref, wout_ref, g1_ref, o_ref, ysc):
    ys = _chunks_to_rows(yg_ref, ysc) + u_ref[...] * d_ref[...]
    gl = jax.nn.gelu(ys)
    z = gl * jax.nn.sigmoid(jnp.dot(gl.astype(BF16), wglu_ref[...], preferred_element_type=F32)
                            + bglu_ref[...])
    ps = jnp.dot(z.astype(BF16), wbs_ref[...], preferred_element_type=F32)
    pa = jnp.dot(on_ref[...], wba_ref[...], preferred_element_type=F32)
    gate = gate_ref[...].astype(F32)
    mix = gate[:, 0:D_MODEL] * pa + gate[:, D_MODEL:2 * D_MODEL] * ps
    m2 = jnp.dot(mix.astype(BF16), wout_ref[...], preferred_element_type=F32)
    o_ref[...] = x_ref[...] + mod_ref[:, 2 * D_MODEL:3 * D_MODEL] * _rms(m2, g1_ref[...])


def _merge(xcat, o_n, yg, u, gates, modsel, ssm_d, w_glu, b_glu, w_br_s, w_br_a, w_out, g1, nt):
    b, n, _ = xcat.shape
    nt_all = n // TM
    row = lambda bi, i: (bi, i, 0)
    return pl.pallas_call(
        _merge_kernel,
        grid=(b, nt),
        in_specs=[pl.BlockSpec((None, TM, D_MODEL), row),
                  pl.BlockSpec((None, TM, ATTN_W), row),
                  pl.BlockSpec((SSM_GROUPS, TM // S5_T, S5_TC), lambda bi, i: (0, bi * nt_all + i, 0)),
                  pl.BlockSpec((None, TM, SSM_W), row),
                  pl.BlockSpec((None, TM, 2 * D_MODEL), row),
                  _mod_spec(0, nt_all - 1),
                  _const_spec((1, SSM_W)),
                  _const_spec((SSM_W, SSM_W)),
                  _const_spec((1, SSM_W)),
                  _const_spec((SSM_W, D_MODEL)),
                  _const_spec((ATTN_W, D_MODEL)),
                  _const_spec((D_MODEL, D_MODEL)),
                  _const_spec((1, D_MODEL))],
        out_specs=pl.BlockSpec((None, TM, D_MODEL), row),
        out_shape=jax.ShapeDtypeStruct((b, nt * TM, D_MODEL), F32),
        scratch_shapes=[pltpu.VMEM((S5_SLABS, TM, LANES), F32)],
        compiler_params=_cparams(("parallel", "arbitrary")),
        name="merge",
    )(xcat, o_n, yg, u, gates, modsel, ssm_d, w_glu, b_glu, w_br_s, w_br_a, w_out, g1)


def _mlp_kernel(x_ref, mod_ref, g2_ref, w1_ref, b1_ref, w2_ref, b2_ref, g3_ref, o_ref):
    x = x_ref[...]
    h = _rms(x, g2_ref[...]) * (1.0 + mod_ref[:, 4 * D_MODEL:5 * D_MODEL]) \
        + mod_ref[:, 3 * D_MODEL:4 * D_MODEL]
    f = jnp.dot(h.astype(BF16), w1_ref[...], preferred_element_type=F32) + b1_ref[...]
    f = jnp.square(jnp.maximum(f, 0.0))
    o = jnp.dot(f.astype(BF16), w2_ref[...], preferred_element_type=F32) + b2_ref[...]
    o_ref[...] = x + mod_ref[:, 5 * D_MODEL:6 * D_MODEL] * _rms(o, g3_ref[...])


def _mlp(x1, modsel, g2, w1, b1, w2, b2, g3, nt_all):
    b, n, _ = x1.shape
    nt = n // TM
    row = lambda bi, i: (bi, i, 0)
    return pl.pallas_call(
        _mlp_kernel,
        grid=(b, nt),
        in_specs=[pl.BlockSpec((None, TM, D_MODEL), row),
                  _mod_spec(0, nt_all - 1),
                  _const_spec((1, D_MODEL)),
                  _const_spec((D_MODEL, D_FF)),
                  _const_spec((1, D_FF)),
                  _const_spec((D_FF, D_MODEL)),
                  _const_spec((1, D_MODEL)),
                  _const_spec((1, D_MODEL))],
        out_specs=pl.BlockSpec((None, TM, D_MODEL), row),
        out_shape=jax.ShapeDtypeStruct((b, n, D_MODEL), F32),
        compiler_params=_cparams(("parallel", "arbitrary")),
        name="mlp",
    )(x1, modsel, g2, w1, b1, w2, b2, g3)


def _rope_tables(n_tokens):
    rows = n_tokens // GRID_W
    row = jnp.repeat(jnp.arange(rows, dtype=jnp.int32), GRID_W).astype(F32)
    col = jnp.tile(jnp.arange(GRID_W, dtype=jnp.int32), rows).astype(F32)
    inv_freq = ROPE_BASE ** (-jnp.arange(N_FREQ, dtype=F32) / N_FREQ)
    ang = jnp.stack([row[:, None] * inv_freq, col[:, None] * inv_freq], axis=1)
    cos, sin = jnp.cos(ang), jnp.sin(ang)
    cos_l = jnp.tile(jnp.concatenate([cos[:, 0], cos[:, 0], cos[:, 1], cos[:, 1]], axis=1), (1, 2))
    sin_l = jnp.tile(jnp.concatenate([-sin[:, 0], sin[:, 0], -sin[:, 1], sin[:, 1]], axis=1), (1, 2))
    cos_t = jnp.concatenate([cos_l, jnp.ones((CTX_LEN, LANES), F32)], axis=0)
    sin_t = jnp.concatenate([sin_l, jnp.zeros((CTX_LEN, LANES), F32)], axis=0)
    return cos_t, sin_t


def kernel(x, c, ctx, c_ctx, ada_w, ada_b, norm_g, w_in, gate_b, lam_qk, subln_g, w_br_a,
           ssm_a_re, ssm_a_im, ssm_b_re, ssm_b_im, ssm_c_re, ssm_c_im, ssm_log_dt, ssm_d,
           w_glu, b_glu, w_br_s, w_out, w_mlp1, b_mlp1, w_mlp2, b_mlp2):
    b, seq, _ = x.shape
    n = seq + CTX_LEN
    assert b + 1 <= 8 and ctx.shape[1] == CTX_LEN == TM and seq % TQ == 0 and n % TK == 0
    cos_t, sin_t = _rope_tables(seq)
    xcat = jnp.concatenate([x, ctx], axis=1)
    cc = jnp.zeros((8, D_MODEL), F32).at[:b].set(c).at[b].set(c_ctx)
    nt_all = n // TM
    for i in range(DEPTH):
        last = i == DEPTH - 1
        lam_init = 0.8 - 0.6 * math.exp(-0.3 * i)
        mod = _modulation(cc, ada_w[i], ada_b[i])
        modsel = jnp.stack([mod[:b], jnp.broadcast_to(mod[b], (b, 6 * D_MODEL))], axis=1)
        modsel = modsel.reshape(b, 2, 1, 6 * D_MODEL)
        ng = norm_g[i]
        q, k, v, u, gates, xg, nrm = _in_proj(xcat, modsel, ng[0].reshape(1, -1), w_in[i].astype(BF16),
                                              gate_b[i].reshape(1, -1), cos_t, sin_t)
        o_n = _attention_layer(q, k, v, nrm, lam_qk[i], subln_g[i], lam_init, seq, with_ctx=not last)
        yg = _s5(xg, ssm_a_re[i], ssm_a_im[i], ssm_log_dt[i], ssm_b_re[i], ssm_b_im[i],
                 ssm_c_re[i], ssm_c_im[i], b)
        x1 = _merge(xcat, o_n, yg, u, gates, modsel, ssm_d[i].reshape(1, -1),
                    w_glu[i].astype(BF16), b_glu[i].reshape(1, -1), w_br_s[i].astype(BF16),
                    w_br_a[i].astype(BF16), w_out[i].astype(BF16), ng[1].reshape(1, -1),
                    nt_all - 1 if last else nt_all)
        xcat = _mlp(x1, modsel, ng[2].reshape(1, -1), w_mlp1[i].astype(BF16),
                    b_mlp1[i].reshape(1, -1), w_mlp2[i].astype(BF16), b_mlp2[i].reshape(1, -1),
                    ng[3].reshape(1, -1), nt_all)
    return xcat
```

```python
import functools
import math

import jax
import jax.numpy as jnp
from jax import lax
from jax.experimental import pallas as pl
from jax.experimental.pallas import tpu as pltpu

F32 = jnp.float32
BF16 = jnp.bfloat16

D_MODEL = 1024
DEPTH = 2
GRID_W = 64
CTX_LEN = 256
N_HEADS = 4
HEAD_DIM = 64
V_DIM = 128
QK_W = N_HEADS * 2 * HEAD_DIM
ATTN_W = 512
SSM_W = 512
SSM_GROUP = 16
SSM_GROUPS = 32
SSM_STATE = 64
D_FF = 4096
IN_COLS = 4096
N_FREQ = 16
ROPE_BASE = 10000.0
EPS = 1e-6
LOG2E = 1.4426950408889634

LANES = 128
TM = 512
ROWS_PAD = TM - CTX_LEN
GATHER_ROWS = 16
TQ = 512
TK = 2816
S5_T = 16
S5_TC = S5_T * SSM_GROUP
S5_SLABS = SSM_W // LANES
S5_GPS = LANES // SSM_GROUP
PAIR = 2 * SSM_STATE
VMEM_LIMIT = 56 * 1024 * 1024

NEG_BIG = -1e30
BOUND_SLACK = 1.03
BOUND_LIMIT = 50.0


def _cparams(sem):
    return pltpu.CompilerParams(dimension_semantics=sem, vmem_limit_bytes=VMEM_LIMIT)


def _const_spec(shape):
    nd = len(shape)
    return pl.BlockSpec(shape, lambda *_: (0,) * nd, pipeline_mode=pl.Buffered(1))


def _mod_spec(roff, ctx_block):
    return pl.BlockSpec((None, None, 1, 6 * D_MODEL),
                        lambda bi, i: (bi, (i + roff) // ctx_block, 0, 0))


def _rms(x, g):
    ms = jnp.mean(x * x, axis=-1, keepdims=True)
    return x * lax.rsqrt(ms + EPS) * g


def _mod_kernel(c_ref, w_ref, b_ref, o_ref):
    c = c_ref[...]
    s = c * jax.nn.sigmoid(c)
    o_ref[...] = jnp.dot(s.astype(BF16), w_ref[...].astype(BF16),
                         preferred_element_type=F32) + b_ref[...]


def _modulation(cc, ada_w, ada_b):
    n = ada_w.shape[1]
    bn = 1024
    return pl.pallas_call(
        _mod_kernel,
        grid=(n // bn,),
        in_specs=[pl.BlockSpec((8, D_MODEL), lambda j: (0, 0)),
                  pl.BlockSpec((D_MODEL, bn), lambda j: (0, j)),
                  pl.BlockSpec((1, bn), lambda j: (0, j))],
        out_specs=pl.BlockSpec((8, bn), lambda j: (0, j)),
        out_shape=jax.ShapeDtypeStruct((8, n), F32),
        compiler_params=_cparams(("arbitrary",)),
        name="adaln_mod",
    )(cc, ada_w, ada_b.reshape(1, n))


def _lane_window(off):
    lane = lax.broadcasted_iota(jnp.int32, (GATHER_ROWS, LANES), 1)
    return (lane >= off) & (lane < off + SSM_GROUP)


def _rows_to_chunks(u, usc, xg_ref):
    for sl in range(S5_SLABS):
        usc[sl] = u[:, sl * LANES:(sl + 1) * LANES]
    for part in range(TM // (S5_T * GATHER_ROWS)):
        r0 = part * GATHER_ROWS
        for sl in range(S5_SLABS):
            acc = [[None] * (S5_TC // LANES) for _ in range(S5_GPS)]
            for t in range(S5_T):
                v = usc[sl, pl.ds(r0 * S5_T + t, GATHER_ROWS, stride=S5_T), :]
                lt, off = divmod(t * SSM_GROUP, LANES)
                win = _lane_window(off)
                for gl in range(S5_GPS):
                    shift = (off - gl * SSM_GROUP) % LANES
                    moved = pltpu.roll(v, shift, 1) if shift else v
                    prev = acc[gl][lt]
                    acc[gl][lt] = jnp.where(win, moved, 0.0 if prev is None else prev)
            for gl in range(S5_GPS):
                for lt in range(S5_TC // LANES):
                    xg_ref[sl * S5_GPS + gl, r0:r0 + GATHER_ROWS, lt * LANES:(lt + 1) * LANES] = (
                        acc[gl][lt].astype(xg_ref.dtype))


def _chunks_to_rows(yg_ref, ysc):
    for part in range(TM // (S5_T * GATHER_ROWS)):
        r0 = part * GATHER_ROWS
        for sl in range(S5_SLABS):
            tiles = [[yg_ref[sl * S5_GPS + gl, r0:r0 + GATHER_ROWS, lt * LANES:(lt + 1) * LANES]
                      for lt in range(S5_TC // LANES)] for gl in range(S5_GPS)]
            for t in range(S5_T):
                lt, off = divmod(t * SSM_GROUP, LANES)
                v = None
                for gl in range(S5_GPS):
                    shift = (gl * SSM_GROUP - off) % LANES
                    src = tiles[gl][lt]
                    moved = pltpu.roll(src, shift, 1) if shift else src
                    v = jnp.where(_lane_window(gl * SSM_GROUP), moved, 0.0 if v is None else v)
                ysc[sl, pl.ds(r0 * S5_T + t, GATHER_ROWS, stride=S5_T), :] = v
    return jnp.concatenate([ysc[sl] for sl in range(S5_SLABS)], axis=1)


def _map_segments():
    li = lax.broadcasted_iota(jnp.int32, (LANES, LANES), 0) // HEAD_DIM
    lj = lax.broadcasted_iota(jnp.int32, (LANES, LANES), 1) // HEAD_DIM
    return (li == lj).astype(BF16)


def _in_kernel(x_ref, mod_ref, g_ref, w_ref, gb_ref, cos_ref, sin_ref,
               q_ref, k_ref, v_ref, u_ref, gate_ref, xg_ref, nrm_ref, usc):
    x = x_ref[...]
    h = _rms(x, g_ref[...]) * (1.0 + mod_ref[:, D_MODEL:2 * D_MODEL]) + mod_ref[:, 0:D_MODEL]
    hb = h.astype(BF16)
    cos = cos_ref[...]
    sin = sin_ref[...]
    lane = lax.broadcasted_iota(jnp.int32, (TM, LANES), 1)
    first_half = (lane & 31) < 16
    sel = (lax.broadcasted_iota(jnp.int32, (8, LANES), 0)
           == lax.broadcasted_iota(jnp.int32, (8, LANES), 1) // HEAD_DIM).astype(BF16)
    slot = lax.broadcasted_iota(jnp.int32, (8, LANES), 1)
    nrm = jnp.zeros((8, LANES), F32)

    def rope_store(col0, out_ref, scale, slot0, nrm):
        t = jnp.dot(hb, w_ref[:, col0:col0 + QK_W], preferred_element_type=F32)
        for j in range(QK_W // LANES):
            tj = t[:, j * LANES:(j + 1) * LANES]
            partner = jnp.where(first_half, pltpu.roll(tj, LANES - 16, 1), pltpu.roll(tj, 16, 1))
            r = tj * cos + partner * sin
            if scale != 1.0:
                r = r * scale
            out_ref[:, j * LANES:(j + 1) * LANES] = r.astype(out_ref.dtype)
            n2 = lax.dot_general(sel, (r * r).astype(BF16), (((1,), (1,)), ((), ())),
                                 preferred_element_type=F32)
            nrm = jnp.where(slot == slot0 + j, jnp.max(n2, axis=1, keepdims=True), nrm)
        return nrm

    nrm = rope_store(0, q_ref, HEAD_DIM ** -0.5 * LOG2E, 0, nrm)
    nrm = rope_store(QK_W, k_ref, 1.0, N_HEADS, nrm)
    nrm_ref[...] = nrm
    v_ref[...] = jnp.dot(hb, w_ref[:, 1024:1536], preferred_element_type=F32).astype(v_ref.dtype)
    u = jnp.dot(hb, w_ref[:, 1536:2048], preferred_element_type=F32)
    u_ref[...] = u
    _rows_to_chunks(u, usc, xg_ref)
    g = jnp.dot(hb, w_ref[:, 2048:IN_COLS], preferred_element_type=F32) + gb_ref[...]
    gate_ref[...] = jax.nn.sigmoid(g).astype(gate_ref.dtype)


def _in_proj(xcat, modsel, g0, w_in_b, gate_b, cos_t, sin_t):
    b, n, _ = xcat.shape
    nt = n // TM
    row = lambda bi, i: (bi, i, 0)
    return pl.pallas_call(
        _in_kernel,
        grid=(b, nt),
        in_specs=[pl.BlockSpec((None, TM, D_MODEL), row),
                  _mod_spec(0, nt - 1),
                  _const_spec((1, D_MODEL)),
                  _const_spec((D_MODEL, IN_COLS)),
                  _const_spec((1, 2 * D_MODEL)),
                  pl.BlockSpec((TM, LANES), lambda bi, i: (i, 0)),
                  pl.BlockSpec((TM, LANES), lambda bi, i: (i, 0))],
        out_specs=[pl.BlockSpec((None, TM, QK_W), row),
                   pl.BlockSpec((None, TM, QK_W), row),
                   pl.BlockSpec((None, TM, ATTN_W), row),
                   pl.BlockSpec((None, TM, SSM_W), row),
                   pl.BlockSpec((None, TM, 2 * D_MODEL), row),
                   pl.BlockSpec((SSM_GROUPS, TM // S5_T, S5_TC), lambda bi, i: (0, bi * nt + i, 0)),
                   pl.BlockSpec((None, None, 8, LANES), lambda bi, i: (bi, i, 0, 0))],
        out_shape=[jax.ShapeDtypeStruct((b, n, QK_W), BF16),
                   jax.ShapeDtypeStruct((b, n, QK_W), BF16),
                   jax.ShapeDtypeStruct((b, n, ATTN_W), BF16),
                   jax.ShapeDtypeStruct((b, n, SSM_W), F32),
                   jax.ShapeDtypeStruct((b, n, 2 * D_MODEL), BF16),
                   jax.ShapeDtypeStruct((SSM_GROUPS, b * n // S5_T, S5_TC), BF16),
                   jax.ShapeDtypeStruct((b, nt, 8, LANES), F32)],
        scratch_shapes=[pltpu.VMEM((S5_SLABS, TM, LANES), F32)],
        compiler_params=_cparams(("parallel", "arbitrary")),
        name="in_proj",
    )(xcat, modsel, g0, w_in_b, gate_b, cos_t, sin_t)


def _attn_kernel(lamqk_ref, subg_ref, q_ref, k_ref, v_ref, o_ref,
                 qm_sc, m_sc, l_sc, acc_sc, *, lam_init, nk):
    ki = pl.program_id(3)

    @pl.when(ki == 0)
    def _init():
        q = q_ref[...]
        lane = lax.broadcasted_iota(jnp.int32, q.shape, 1)
        zero = jnp.zeros_like(q)
        qm_sc[0] = jnp.where(lane < HEAD_DIM, q, zero)
        qm_sc[1] = jnp.where(lane >= HEAD_DIM, q, zero)
        m_sc[...] = jnp.full(m_sc.shape, NEG_BIG, F32)
        l_sc[...] = jnp.zeros(l_sc.shape, F32)
        acc_sc[...] = jnp.zeros(acc_sc.shape, F32)

    k = k_ref[...]
    v = v_ref[...]
    for c in range(2):
        s = lax.dot_general(qm_sc[c], k, (((1,), (1,)), ((), ())), preferred_element_type=F32)
        m_prev = m_sc[c]
        m_new = jnp.maximum(m_prev, jnp.max(s, axis=1, keepdims=True))
        alpha = jnp.exp2(m_prev - m_new)
        p = jnp.exp2(s - m_new)
        l_sc[c] = alpha * l_sc[c] + jnp.sum(p, axis=1, keepdims=True)
        acc_sc[c] = alpha * acc_sc[c] + jnp.dot(p.astype(BF16), v, preferred_element_type=F32)
        m_sc[c] = m_new

    @pl.when(ki == nk - 1)
    def _fin():
        lq = lamqk_ref[...]
        lam = (jnp.exp(jnp.sum(lq[0:1] * lq[1:2], axis=1, keepdims=True))
               - jnp.exp(jnp.sum(lq[2:3] * lq[3:4], axis=1, keepdims=True)) + lam_init)
        o = acc_sc[0] / l_sc[0] - lam * (acc_sc[1] / l_sc[1])
        o_ref[...] = (_rms(o, subg_ref[...]) * (1.0 - lam_init)).astype(o_ref.dtype)


def _attn_bounded_kernel(lamqk_ref, subg_ref, kmax_ref, q_ref, k_ref, v_ref, o_ref,
                         qx_sc, kx_sc, l_sc, acc_sc, *, lam_init, nk):
    ki = pl.program_id(3)
    tq = q_ref.shape[0]
    tk = k_ref.shape[0]

    @pl.when(ki == 0)
    def _init():
        q = q_ref[...]
        lane = lax.broadcasted_iota(jnp.int32, (tq, LANES), 1)
        qf = q.astype(F32)
        n2 = jnp.dot((qf * qf).astype(BF16), _map_segments(), preferred_element_type=F32)
        bound = jnp.sqrt(n2) * kmax_ref[...] * BOUND_SLACK
        zero = jnp.zeros_like(q)
        qx_sc[0, :, 0:LANES] = jnp.where(lane < HEAD_DIM, q, zero)
        qx_sc[0, :, LANES:2 * LANES] = jnp.where(lane == 0, -bound, 0.0).astype(BF16)
        qx_sc[1, :, 0:LANES] = jnp.where(lane >= HEAD_DIM, q, zero)
        qx_sc[1, :, LANES:2 * LANES] = jnp.where(lane == 0, -pltpu.roll(bound, HEAD_DIM, 1), 0.0).astype(BF16)
        lane_k = lax.broadcasted_iota(jnp.int32, (tk, LANES), 1)
        kx_sc[:, LANES:2 * LANES] = jnp.where(lane_k == 0, 1.0, 0.0).astype(BF16)
        l_sc[...] = jnp.zeros(l_sc.shape, F32)
        acc_sc[...] = jnp.zeros(acc_sc.shape, F32)

    kx_sc[:, 0:LANES] = k_ref[...]
    kx = kx_sc[...]
    v = v_ref[...]
    for c in range(2):
        s = lax.dot_general(qx_sc[c], kx, (((1,), (1,)), ((), ())), preferred_element_type=F32)
        p = jnp.exp2(s)
        part = p[:, 0:LANES]
        for j in range(1, tk // LANES):
            part = part + p[:, j * LANES:(j + 1) * LANES]
        l_sc[c] += part
        acc_sc[c] += jnp.dot(p.astype(BF16), v, preferred_element_type=F32)

    @pl.when(ki == nk - 1)
    def _fin():
        lq = lamqk_ref[...]
        lam = (jnp.exp(jnp.sum(lq[0:1] * lq[1:2], axis=1, keepdims=True))
               - jnp.exp(jnp.sum(lq[2:3] * lq[3:4], axis=1, keepdims=True)) + lam_init)
        l0 = jnp.sum(l_sc[0], axis=1, keepdims=True)
        l1 = jnp.sum(l_sc[1], axis=1, keepdims=True)
        o = acc_sc[0] / l0 - lam * (acc_sc[1] / l1)
        o_ref[...] = (_rms(o, subg_ref[...]) * (1.0 - lam_init)).astype(o_ref.dtype)


def _attention(q, k, v, kmax, lam_qk, subln_g, lam_init, *, bounded, tq, tk, q0, nq, k0, nk, o_prev=None):
    b, n, _ = q.shape
    q_idx = lambda bi, h, qi, ki: (bi, qi + q0, h)
    kv_idx = lambda bi, h, qi, ki: (bi, ki + k0, h)
    in_specs = [pl.BlockSpec((4, HEAD_DIM), lambda *_: (0, 0)),
                pl.BlockSpec((1, V_DIM), lambda *_: (0, 0)),
                pl.BlockSpec((None, None, 1, LANES), lambda bi, h, qi, ki: (bi, h, 0, 0)),
                pl.BlockSpec((None, tq, LANES), q_idx),
                pl.BlockSpec((None, tk, LANES), kv_idx),
                pl.BlockSpec((None, tk, LANES), kv_idx)]
    args = [lam_qk, subln_g.reshape(1, V_DIM), kmax, q, k, v]
    if bounded:
        kern = functools.partial(_attn_bounded_kernel, lam_init=lam_init, nk=nk)
        scratch = [pltpu.VMEM((2, tq, 2 * LANES), BF16),
                   pltpu.VMEM((tk, 2 * LANES), BF16),
                   pltpu.VMEM((2, tq, LANES), F32),
                   pltpu.VMEM((2, tq, V_DIM), F32)]
    else:
        kern = functools.partial(_drop_input, functools.partial(_attn_kernel, lam_init=lam_init, nk=nk), 2)
        scratch = [pltpu.VMEM((2, tq, LANES), BF16),
                   pltpu.VMEM((2, tq, 1), F32),
                   pltpu.VMEM((2, tq, 1), F32),
                   pltpu.VMEM((2, tq, V_DIM), F32)]
    aliases = {}
    if o_prev is not None:
        in_specs.append(pl.BlockSpec(memory_space=pl.ANY))
        args.append(o_prev)
        aliases = {6: 0}
        kern = functools.partial(_drop_input, kern, 6)
    return pl.pallas_call(
        kern,
        grid=(b, N_HEADS, nq, nk),
        in_specs=in_specs,
        out_specs=pl.BlockSpec((None, tq, LANES), q_idx),
        out_shape=jax.ShapeDtypeStruct((b, n, ATTN_W), BF16),
        scratch_shapes=scratch,
        input_output_aliases=aliases,
        compiler_params=_cparams(("parallel", "parallel", "parallel", "arbitrary")),
        name="diff_attn_bounded" if bounded else "diff_attn",
    )(*args)


def _attention_layer(q, k, v, nrm, lam_qk, subln_g, lam_init, seq, with_ctx):
    b = q.shape[0]
    nkeys = seq + CTX_LEN
    top = jnp.sqrt(jnp.max(nrm[:, :, 0:2, 0:2 * N_HEADS], axis=1))
    qmax, kmax = top[:, :, 0:N_HEADS], top[:, :, N_HEADS:]
    small = jnp.max(qmax * kmax) * BOUND_SLACK < BOUND_LIMIT
    kmax4 = jnp.repeat(jnp.swapaxes(kmax, 1, 2), HEAD_DIM, axis=-1).reshape(b, N_HEADS, 1, LANES)

    def run(bounded):
        o_n = _attention(q, k, v, kmax4, lam_qk, subln_g, lam_init, bounded=bounded,
                         tq=TQ, tk=TK, q0=0, nq=seq // TQ, k0=0, nk=nkeys // TK)
        if with_ctx:
            o_n = _attention(q, k, v, kmax4, lam_qk, subln_g, lam_init, bounded=bounded,
                             tq=TM, tk=CTX_LEN, q0=seq // TM, nq=1, k0=seq // CTX_LEN, nk=1,
                             o_prev=o_n)
        return o_n

    return lax.cond(small, lambda: run(True), lambda: run(False))


def _drop_input(kern, idx, *refs):
    return kern(*refs[:idx], *refs[idx + 1:])


def _cmul(ar, ai, br, bi):
    return ar * br - ai * bi, ar * bi + ai * br


def _cpow_by_bits(lr, li, expo, nbits):
    pr = jnp.ones(expo.shape, F32)
    pi = jnp.zeros(expo.shape, F32)
    br, bi = lr, li
    for bit in range(nbits):
        on = ((expo >> bit) & 1) == 1
        fr = jnp.where(on, br, 1.0)
        fi = jnp.where(on, bi, 0.0)
        pr, pi = _cmul(pr, pi, fr, fi)
        if bit + 1 < nbits:
            br, bi = _cmul(br, bi, br, bi)
    return pr, pi


def _discretize(ar, ai, log_dt):
    dt = jnp.exp(log_dt)
    er = jnp.exp(ar * dt)
    lbr = er * jnp.cos(ai * dt)
    lbi = er * jnp.sin(ai * dt)
    den = ar * ar + ai * ai
    nr = lbr - 1.0
    cr = (nr * ar + lbi * ai) / den
    ci = (lbi * ar - nr * ai) / den
    return lbr, lbi, cr, ci


def _s5_kernel(x_ref, ar_r, ai_r, ldt_r, ar_c, ai_c, ldt_c, btr_ref, bti_ref, ctr_ref, cti_ref,
               y_ref, sext, toep, zre, zim, hre, him, *, nrows, nchunk, nbatch, nctx):
    nlat = nchunk - nctx
    rowj = lax.broadcasted_iota(jnp.int32, (S5_TC, PAIR), 0) // SSM_GROUP
    tau = lax.broadcasted_iota(jnp.int32, (PAIR, S5_TC), 1) // SSM_GROUP
    lane_g = lax.broadcasted_iota(jnp.int32, (1, PAIR), 1) // SSM_STATE
    row_g = lax.broadcasted_iota(jnp.int32, (PAIR, 1), 0) // SSM_STATE

    for d in range(2):
        fwd = d == 0
        lbr, lbi, cr, ci = _discretize(ar_r[d], ai_r[d], ldt_r[d])
        lbr_c, lbi_c, _, _ = _discretize(ar_c[d], ai_c[d], ldt_c[d])
        bbr, bbi = _cmul(cr, ci, btr_ref[d], bti_ref[d])
        bbr_t = jnp.concatenate([bbr] * S5_T, axis=0)
        bbi_t = jnp.concatenate([bbi] * S5_T, axis=0)
        pwr, pwi = _cpow_by_bits(lbr, lbi, (S5_T - 1 - rowj) if fwd else rowj, 4)
        bsr, bsi = _cmul(pwr, pwi, bbr_t, bbi_t)
        per, pei = _cpow_by_bits(lbr_c, lbi_c, tau if fwd else (S5_T - 1 - tau), 4)
        wcr, wci = _cmul(per, pei, ctr_ref[d], cti_ref[d])
        ser, sei = _cpow_by_bits(lbr_c, lbi_c, (tau + 1) if fwd else (S5_T - tau), 5)
        csr, csi = _cmul(ser, sei, ctr_ref[d], cti_ref[d])
        tr, ti = lbr, lbi
        for _ in range(int(math.log2(S5_T))):
            tr, ti = _cmul(tr, ti, tr, ti)

        for gg in range(2):
            lm = lane_g == gg
            rm = row_g == gg
            strip = (jnp.dot(jnp.where(lm, bbr, 0.0), wcr, preferred_element_type=F32,
                             precision=lax.Precision.HIGHEST)
                     - jnp.dot(jnp.where(lm, bbi, 0.0), wci, preferred_element_type=F32,
                               precision=lax.Precision.HIGHEST))
            if fwd:
                sext[:, 0:S5_TC] = jnp.zeros((SSM_GROUP, S5_TC), F32)
                sext[:, S5_TC:2 * S5_TC] = strip
                for j in range(S5_T):
                    lo = S5_TC - SSM_GROUP * j
                    toep[j * SSM_GROUP:(j + 1) * SSM_GROUP, :] = sext[:, lo:lo + S5_TC].astype(BF16)
            else:
                sext[:, 0:S5_TC] = strip
                sext[:, S5_TC:2 * S5_TC] = jnp.zeros((SSM_GROUP, S5_TC), F32)
                for j in range(S5_T):
                    lo = SSM_GROUP * (S5_T - 1 - j)
                    toep[j * SSM_GROUP:(j + 1) * SSM_GROUP, :] = sext[:, lo:lo + S5_TC].astype(BF16)
            x = x_ref[gg]
            yi = jnp.dot(x, toep[...], preferred_element_type=F32)
            if fwd:
                y_ref[gg] = yi
            else:
                y_ref[gg] += yi
            z_r = jnp.dot(x, jnp.where(lm, bsr, 0.0).astype(BF16), preferred_element_type=F32)
            z_i = jnp.dot(x, jnp.where(lm, bsi, 0.0).astype(BF16), preferred_element_type=F32)
            if gg == 0:
                zre[...] = z_r
                zim[...] = z_i
            else:
                zre[...] += z_r
                zim[...] += z_i

        hre[...] = jnp.zeros(hre.shape, F32)
        him[...] = jnp.zeros(him.shape, F32)

        def body(n, carry):
            if fwd:
                ch = jnp.where(n < nctx, nlat + n, n - nctx)
            else:
                ch = nchunk - 1 - n
            new = []
            for b in range(nbatch):
                h_r, h_i = carry[2 * b], carry[2 * b + 1]
                r = b * nrows + ch
                hre[pl.ds(r, 1), :] = h_r
                him[pl.ds(r, 1), :] = h_i
                n_r, n_i = _cmul(tr, ti, h_r, h_i)
                new.append(n_r + zre[pl.ds(r, 1), :])
                new.append(n_i + zim[pl.ds(r, 1), :])
            return tuple(new)

        zero = jnp.zeros((1, PAIR), F32)
        lax.fori_loop(0, nchunk, body, (zero,) * (2 * nbatch), unroll=8)

        h_r = hre[...].astype(BF16)
        h_i = him[...].astype(BF16)
        for gg in range(2):
            rm = row_g == gg
            y_ref[gg] += (jnp.dot(h_r, jnp.where(rm, csr, 0.0).astype(BF16), preferred_element_type=F32)
                          - jnp.dot(h_i, jnp.where(rm, csi, 0.0).astype(BF16), preferred_element_type=F32))


def _s5(xg, a_re, a_im, log_dt, b_re, b_im, c_re, c_im, nbatch, nvalid):
    g, nr, _ = xg.shape
    gp = g // 2
    pair_row = lambda a: a.reshape(2, gp, 1, PAIR)
    pair_col = lambda a: a.reshape(2, gp, PAIR, 1)
    ldt = jnp.broadcast_to(log_dt[..., None], a_re.shape)
    bt = lambda a: jnp.transpose(a.reshape(2, gp, 2, SSM_STATE, SSM_GROUP), (0, 1, 4, 2, 3)).reshape(
        2, gp, SSM_GROUP, PAIR)
    ct = lambda a: jnp.tile(jnp.transpose(a.reshape(2, gp, 2, SSM_GROUP, SSM_STATE), (0, 1, 2, 4, 3)).reshape(
        2, gp, PAIR, SSM_GROUP), (1, 1, 1, S5_T))
    blk = lambda *s: pl.BlockSpec((2, None) + s, lambda i: (0, i, 0, 0))
    kern = functools.partial(_s5_kernel, nrows=nr // nbatch, nchunk=nvalid // S5_T, nbatch=nbatch,
                             nctx=CTX_LEN // S5_T)
    return pl.pallas_call(
        kern,
        grid=(gp,),
        in_specs=[pl.BlockSpec((2, nr, S5_TC), lambda i: (i, 0, 0)),
                  blk(1, PAIR), blk(1, PAIR), blk(1, PAIR),
                  blk(PAIR, 1), blk(PAIR, 1), blk(PAIR, 1),
                  blk(SSM_GROUP, PAIR), blk(SSM_GROUP, PAIR),
                  blk(PAIR, S5_TC), blk(PAIR, S5_TC)],
        out_specs=pl.BlockSpec((2, nr, S5_TC), lambda i: (i, 0, 0)),
        out_shape=jax.ShapeDtypeStruct((g, nr, S5_TC), F32),
        scratch_shapes=[pltpu.VMEM((SSM_GROUP, 2 * S5_TC), F32),
                        pltpu.VMEM((S5_TC, S5_TC), BF16),
                        pltpu.VMEM((nr, PAIR), F32),
                        pltpu.VMEM((nr, PAIR), F32),
                        pltpu.VMEM((nr, PAIR), F32),
                        pltpu.VMEM((nr, PAIR), F32)],
        compiler_params=_cparams(("parallel",)),
        name="s5_scan",
    )(xg, pair_row(a_re), pair_row(a_im), pair_row(ldt), pair_col(a_re), pair_col(a_im), pair_col(ldt),
      bt(b_re), bt(b_im), ct(c_re), ct(c_im))


def _tail_kernel(x_ref, on_ref, yg_ref, u_ref, gate_ref, mod_ref, d_ref, wglu_ref, bglu_ref,
                 wbs_ref, wba_ref, wout_ref, g1_ref, g2_ref, w1_ref, b1_ref, w2_ref, b2_ref, g3_ref,
                 o_ref, ysc):
    ys = _chunks_to_rows(yg_ref, ysc) + u_ref[...] * d_ref[...]
    gl = jax.nn.gelu(ys)
    z = gl * jax.nn.sigmoid(jnp.dot(gl.astype(BF16), wglu_ref[...], preferred_element_type=F32)
                            + bglu_ref[...])
    ps = jnp.dot(z.astype(BF16), wbs_ref[...], preferred_element_type=F32)
    pa = jnp.dot(on_ref[...], wba_ref[...], preferred_element_type=F32)
    gate = gate_ref[...].astype(F32)
    mix = gate[:, 0:D_MODEL] * pa + gate[:, D_MODEL:2 * D_MODEL] * ps
    m2 = jnp.dot(mix.astype(BF16), wout_ref[...], preferred_element_type=F32)
    x1 = x_ref[...] + mod_ref[:, 2 * D_MODEL:3 * D_MODEL] * _rms(m2, g1_ref[...])
    h = _rms(x1, g2_ref[...]) * (1.0 + mod_ref[:, 4 * D_MODEL:5 * D_MODEL]) \
        + mod_ref[:, 3 * D_MODEL:4 * D_MODEL]
    f = jnp.dot(h.astype(BF16), w1_ref[...], preferred_element_type=F32) + b1_ref[...]
    f = jnp.square(jnp.maximum(f, 0.0))
    o = jnp.dot(f.astype(BF16), w2_ref[...], preferred_element_type=F32) + b2_ref[...]
    o_ref[...] = x1 + mod_ref[:, 5 * D_MODEL:6 * D_MODEL] * _rms(o, g3_ref[...])


def _tail(xcat, o_n, yg, u, gates, modsel, ssm_d, w_glu, b_glu, w_br_s, w_br_a, w_out, g1,
          g2, w1, b1, w2, b2, g3, nt):
    b, n, _ = xcat.shape
    nt_all = n // TM
    row = lambda bi, i: (bi, i, 0)
    return pl.pallas_call(
        _tail_kernel,
        grid=(b, nt),
        in_specs=[pl.BlockSpec((None, TM, D_MODEL), row),
                  pl.BlockSpec((None, TM, ATTN_W), row),
                  pl.BlockSpec((SSM_GROUPS, TM // S5_T, S5_TC), lambda bi, i: (0, bi * nt_all + i, 0)),
                  pl.BlockSpec((None, TM, SSM_W), row),
                  pl.BlockSpec((None, TM, 2 * D_MODEL), row),
                  _mod_spec(0, nt_all - 1),
                  _const_spec((1, SSM_W)),
                  _const_spec((SSM_W, SSM_W)),
                  _const_spec((1, SSM_W)),
                  _const_spec((SSM_W, D_MODEL)),
                  _const_spec((ATTN_W, D_MODEL)),
                  _const_spec((D_MODEL, D_MODEL)),
                  _const_spec((1, D_MODEL)),
                  _const_spec((1, D_MODEL)),
                  _const_spec((D_MODEL, D_FF)),
                  _const_spec((1, D_FF)),
                  _const_spec((D_FF, D_MODEL)),
                  _const_spec((1, D_MODEL)),
                  _const_spec((1, D_MODEL))],
        out_specs=pl.BlockSpec((None, TM, D_MODEL), row),
        out_shape=jax.ShapeDtypeStruct((b, nt * TM, D_MODEL), F32),
        scratch_shapes=[pltpu.VMEM((S5_SLABS, TM, LANES), F32)],
        compiler_params=_cparams(("parallel", "arbitrary")),
        name="merge_mlp",
    )(xcat, o_n, yg, u, gates, modsel, ssm_d, w_glu, b_glu, w_br_s, w_br_a, w_out, g1,
      g2, w1, b1, w2, b2, g3)


def _rope_tables(n_tokens):
    rows = n_tokens // GRID_W
    row = jnp.repeat(jnp.arange(rows, dtype=jnp.int32), GRID_W).astype(F32)
    col = jnp.tile(jnp.arange(GRID_W, dtype=jnp.int32), rows).astype(F32)
    inv_freq = ROPE_BASE ** (-jnp.arange(N_FREQ, dtype=F32) / N_FREQ)
    ang = jnp.stack([row[:, None] * inv_freq, col[:, None] * inv_freq], axis=1)
    cos, sin = jnp.cos(ang), jnp.sin(ang)
    cos_l = jnp.tile(jnp.concatenate([cos[:, 0], cos[:, 0], cos[:, 1], cos[:, 1]], axis=1), (1, 2))
    sin_l = jnp.tile(jnp.concatenate([-sin[:, 0], sin[:, 0], -sin[:, 1], sin[:, 1]], axis=1), (1, 2))
    cos_t = jnp.concatenate([cos_l, jnp.ones((TM, LANES), F32)], axis=0)
    sin_t = jnp.concatenate([sin_l, jnp.zeros((TM, LANES), F32)], axis=0)
    return cos_t, sin_t


def kernel(x, c, ctx, c_ctx, ada_w, ada_b, norm_g, w_in, gate_b, lam_qk, subln_g, w_br_a,
           ssm_a_re, ssm_a_im, ssm_b_re, ssm_b_im, ssm_c_re, ssm_c_im, ssm_log_dt, ssm_d,
           w_glu, b_glu, w_br_s, w_out, w_mlp1, b_mlp1, w_mlp2, b_mlp2):
    b, seq, _ = x.shape
    nvalid = seq + CTX_LEN
    assert b + 1 <= 8 and ctx.shape[1] == CTX_LEN and seq % TM == 0 and seq % TQ == 0 and nvalid % TK == 0
    cos_t, sin_t = _rope_tables(seq)
    xcat = jnp.concatenate([x, ctx, jnp.zeros((b, ROWS_PAD, D_MODEL), F32)], axis=1)
    cc = jnp.zeros((8, D_MODEL), F32).at[:b].set(c).at[b].set(c_ctx)
    nt_all = (nvalid + ROWS_PAD) // TM
    for i in range(DEPTH):
        last = i == DEPTH - 1
        lam_init = 0.8 - 0.6 * math.exp(-0.3 * i)
        mod = _modulation(cc, ada_w[i], ada_b[i])
        modsel = jnp.stack([mod[:b], jnp.broadcast_to(mod[b], (b, 6 * D_MODEL))], axis=1)
        modsel = modsel.reshape(b, 2, 1, 6 * D_MODEL)
        ng = norm_g[i]
        q, k, v, u, gates, xg, nrm = _in_proj(xcat, modsel, ng[0].reshape(1, -1), w_in[i].astype(BF16),
                                              gate_b[i].reshape(1, -1), cos_t, sin_t)
        o_n = _attention_layer(q, k, v, nrm, lam_qk[i], subln_g[i], lam_init, seq, with_ctx=not last)
        yg = _s5(xg, ssm_a_re[i], ssm_a_im[i], ssm_log_dt[i], ssm_b_re[i], ssm_b_im[i],
                 ssm_c_re[i], ssm_c_im[i], b, nvalid)
        xcat = _tail(xcat, o_n, yg, u, gates, modsel, ssm_d[i].reshape(1, -1),
                     w_glu[i].astype(BF16), b_glu[i].reshape(1, -1), w_br_s[i].astype(BF16),
                     w_br_a[i].astype(BF16), w_out[i].astype(BF16), ng[1].reshape(1, -1),
                     ng[2].reshape(1, -1), w_mlp1[i].astype(BF16), b_mlp1[i].reshape(1, -1),
                     w_mlp2[i].astype(BF16), b_mlp2[i].reshape(1, -1), ng[3].reshape(1, -1),
                     nt_all - 1 if last else nt_all)
    return xcat
```

```python
import functools
import math

import jax
import jax.numpy as jnp
from jax import lax
from jax.experimental import pallas as pl
from jax.experimental.pallas import tpu as pltpu

F32 = jnp.float32
BF16 = jnp.bfloat16

D_MODEL = 1024
DEPTH = 2
GRID_W = 64
CTX_LEN = 256
N_HEADS = 4
HEAD_DIM = 64
V_DIM = 128
QK_W = N_HEADS * 2 * HEAD_DIM
ATTN_W = 512
SSM_W = 512
SSM_GROUP = 16
SSM_GROUPS = 32
SSM_STATE = 64
D_FF = 4096
IN_COLS = 4096
N_FREQ = 16
ROPE_BASE = 10000.0
EPS = 1e-6
LOG2E = 1.4426950408889634

LANES = 128
SUBLANES = 8
TM = 512
ROWS_PAD = TM - CTX_LEN
GATHER_ROWS = 16
TQ = 1024
TK = 2816
S5_T = 16
S5_TC = S5_T * SSM_GROUP
S5_SLABS = SSM_W // LANES
S5_GPS = LANES // SSM_GROUP
PAIR = 2 * SSM_STATE
S5_NP = 4
VMEM_LIMIT = 56 * 1024 * 1024

NEG_BIG = -1e30
BOUND_SLACK = 1.03
BOUND_LIMIT = 50.0

NT_DIMS = (((1,), (1,)), ((), ()))


def _cparams(sem):
    return pltpu.CompilerParams(dimension_semantics=sem, vmem_limit_bytes=VMEM_LIMIT)


def _layer_spec(layer, shape):
    nd = len(shape)
    return pl.BlockSpec((None,) + shape, lambda *_: (layer,) + (0,) * nd, pipeline_mode=pl.Buffered(1))


def _mod_spec(layer, ctx_tile):
    return pl.BlockSpec((None, None, None, 1, 6 * D_MODEL),
                        lambda bi, i: (layer, bi, i // ctx_tile, 0, 0))


def _stream_specs(n_lat_tiles):
    return [pl.BlockSpec((None, TM, D_MODEL), lambda bi, i: (bi, jnp.minimum(i, n_lat_tiles - 1), 0)),
            pl.BlockSpec((None, TM, D_MODEL), lambda bi, i: (bi, 0, 0))]


def _stream_tile(xl_ref, xc_ref, n_lat_tiles):
    return jnp.where(pl.program_id(1) >= n_lat_tiles, xc_ref[...], xl_ref[...])


def _rms(x, g):
    ms = jnp.mean(x * x, axis=-1, keepdims=True)
    return x * lax.rsqrt(ms + EPS) * g


def _mod_kernel(c_ref, w_ref, b_ref, o_ref):
    c = c_ref[...]
    s = c * jax.nn.sigmoid(c)
    o_ref[...] = jnp.dot(s.astype(BF16), w_ref[...].astype(BF16),
                         preferred_element_type=F32) + b_ref[...]


def _modulation(cc, ada_w, ada_b):
    depth, _, n = ada_w.shape
    bn = 1024
    return pl.pallas_call(
        _mod_kernel,
        grid=(depth, n // bn),
        in_specs=[pl.BlockSpec((8, D_MODEL), lambda l, j: (0, 0)),
                  pl.BlockSpec((None, D_MODEL, bn), lambda l, j: (l, 0, j)),
                  pl.BlockSpec((None, 1, bn), lambda l, j: (l, 0, j))],
        out_specs=pl.BlockSpec((None, 8, bn), lambda l, j: (l, 0, j)),
        out_shape=jax.ShapeDtypeStruct((depth, 8, n), F32),
        compiler_params=_cparams(("arbitrary", "arbitrary")),
        name="adaln_mod",
    )(cc, ada_w, ada_b.reshape(depth, 1, n))


def _lane_window(off):
    lane = lax.broadcasted_iota(jnp.int32, (GATHER_ROWS, LANES), 1)
    return (lane >= off) & (lane < off + SSM_GROUP)


def _rows_to_chunks(u, usc, xg_ref):
    for sl in range(S5_SLABS):
        usc[sl] = u[:, sl * LANES:(sl + 1) * LANES]
    for part in range(TM // (S5_T * GATHER_ROWS)):
        r0 = part * GATHER_ROWS
        for sl in range(S5_SLABS):
            acc = [[None] * (S5_TC // LANES) for _ in range(S5_GPS)]
            for t in range(S5_T):
                v = usc[sl, pl.ds(r0 * S5_T + t, GATHER_ROWS, stride=S5_T), :]
                lt, off = divmod(t * SSM_GROUP, LANES)
                win = _lane_window(off)
                for gl in range(S5_GPS):
                    shift = (off - gl * SSM_GROUP) % LANES
                    moved = pltpu.roll(v, shift, 1) if shift else v
                    prev = acc[gl][lt]
                    acc[gl][lt] = jnp.where(win, moved, 0.0 if prev is None else prev)
            for gl in range(S5_GPS):
                for lt in range(S5_TC // LANES):
                    xg_ref[sl * S5_GPS + gl, r0:r0 + GATHER_ROWS, lt * LANES:(lt + 1) * LANES] = (
                        acc[gl][lt].astype(xg_ref.dtype))


def _chunks_to_rows(yg_ref, ysc):
    for part in range(TM // (S5_T * GATHER_ROWS)):
        r0 = part * GATHER_ROWS
        for sl in range(S5_SLABS):
            tiles = [[yg_ref[sl * S5_GPS + gl, r0:r0 + GATHER_ROWS, lt * LANES:(lt + 1) * LANES]
                      for lt in range(S5_TC // LANES)] for gl in range(S5_GPS)]
            for t in range(S5_T):
                lt, off = divmod(t * SSM_GROUP, LANES)
                v = None
                for gl in range(S5_GPS):
                    shift = (gl * SSM_GROUP - off) % LANES
                    src = tiles[gl][lt]
                    moved = pltpu.roll(src, shift, 1) if shift else src
                    v = jnp.where(_lane_window(gl * SSM_GROUP), moved, 0.0 if v is None else v)
                ysc[sl, pl.ds(r0 * S5_T + t, GATHER_ROWS, stride=S5_T), :] = v
    return jnp.concatenate([ysc[sl] for sl in range(S5_SLABS)], axis=1)


def _map_segments():
    li = lax.broadcasted_iota(jnp.int32, (LANES, LANES), 0) // HEAD_DIM
    lj = lax.broadcasted_iota(jnp.int32, (LANES, LANES), 1) // HEAD_DIM
    return (li == lj).astype(BF16)


def _in_kernel(xl_ref, xc_ref, mod_ref, g_ref, w_ref, gb_ref, cos_ref, sin_ref,
               q_ref, k_ref, v_ref, u_ref, gate_ref, xg_ref, nrm_ref, usc, *, n_lat_tiles):
    x = _stream_tile(xl_ref, xc_ref, n_lat_tiles)
    h = _rms(x, g_ref[0:1]) * (1.0 + mod_ref[:, D_MODEL:2 * D_MODEL]) + mod_ref[:, 0:D_MODEL]
    hb = h.astype(BF16)
    cos = cos_ref[...]
    sin = sin_ref[...]
    lane = lax.broadcasted_iota(jnp.int32, (TM, LANES), 1)
    first_half = (lane & 31) < 16
    sel = (lax.broadcasted_iota(jnp.int32, (8, LANES), 0)
           == lax.broadcasted_iota(jnp.int32, (8, LANES), 1) // HEAD_DIM).astype(BF16)
    slot = lax.broadcasted_iota(jnp.int32, (8, LANES), 1)
    nrm = jnp.zeros((8, LANES), F32)

    def rope_store(col0, out_ref, scale, slot0, nrm):
        t = jnp.dot(hb, w_ref[:, col0:col0 + QK_W], preferred_element_type=F32)
        for j in range(QK_W // LANES):
            tj = t[:, j * LANES:(j + 1) * LANES]
            partner = jnp.where(first_half, pltpu.roll(tj, LANES - 16, 1), pltpu.roll(tj, 16, 1))
            r = tj * cos + partner * sin
            if scale != 1.0:
                r = r * scale
            out_ref[:, j * LANES:(j + 1) * LANES] = r.astype(out_ref.dtype)
            n2 = lax.dot_general(sel, (r * r).astype(BF16), NT_DIMS, preferred_element_type=F32)
            nrm = jnp.where(slot == slot0 + j, jnp.max(n2, axis=1, keepdims=True), nrm)
        return nrm

    nrm = rope_store(0, q_ref, HEAD_DIM ** -0.5 * LOG2E, 0, nrm)
    nrm = rope_store(QK_W, k_ref, 1.0, N_HEADS, nrm)
    nrm_ref[...] = nrm
    v_ref[...] = jnp.dot(hb, w_ref[:, 1024:1536], preferred_element_type=F32).astype(v_ref.dtype)
    u = jnp.dot(hb, w_ref[:, 1536:2048], preferred_element_type=F32)
    u_ref[...] = u
    _rows_to_chunks(u, usc, xg_ref)
    g = jnp.dot(hb, w_ref[:, 2048:IN_COLS], preferred_element_type=F32) + gb_ref[...]
    gate_ref[...] = jax.nn.sigmoid(g).astype(gate_ref.dtype)


def _in_proj(layer, xl, xc, modsel, norm_g, w_in_b, gate_b, cos_t, sin_t, n_lat_tiles):
    b = xl.shape[0]
    nt = n_lat_tiles + 1
    n = nt * TM
    row = lambda bi, i: (bi, i, 0)
    return pl.pallas_call(
        functools.partial(_in_kernel, n_lat_tiles=n_lat_tiles),
        grid=(b, nt),
        in_specs=_stream_specs(n_lat_tiles) + [
            _mod_spec(layer, n_lat_tiles),
            _layer_spec(layer, (4, D_MODEL)),
            _layer_spec(layer, (D_MODEL, IN_COLS)),
            _layer_spec(layer, (1, 2 * D_MODEL)),
            pl.BlockSpec((TM, LANES), lambda bi, i: (i, 0)),
            pl.BlockSpec((TM, LANES), lambda bi, i: (i, 0))],
        out_specs=[pl.BlockSpec((None, TM, QK_W), row),
                   pl.BlockSpec((None, TM, QK_W), row),
                   pl.BlockSpec((None, TM, ATTN_W), row),
                   pl.BlockSpec((None, TM, SSM_W), row),
                   pl.BlockSpec((None, TM, 2 * D_MODEL), row),
                   pl.BlockSpec((SSM_GROUPS, TM // S5_T, S5_TC), lambda bi, i: (0, bi * nt + i, 0)),
                   pl.BlockSpec((None, None, 8, LANES), lambda bi, i: (bi, i, 0, 0))],
        out_shape=[jax.ShapeDtypeStruct((b, n, QK_W), BF16),
                   jax.ShapeDtypeStruct((b, n, QK_W), BF16),
                   jax.ShapeDtypeStruct((b, n, ATTN_W), BF16),
                   jax.ShapeDtypeStruct((b, n, SSM_W), F32),
                   jax.ShapeDtypeStruct((b, n, 2 * D_MODEL), BF16),
                   jax.ShapeDtypeStruct((SSM_GROUPS, b * n // S5_T, S5_TC), BF16),
                   jax.ShapeDtypeStruct((b, nt, 8, LANES), F32)],
        scratch_shapes=[pltpu.VMEM((S5_SLABS, TM, LANES), F32)],
        compiler_params=_cparams(("parallel", "arbitrary")),
        name="in_proj",
    )(xl, xc, modsel, norm_g, w_in_b, gate_b, cos_t, sin_t)


def _attn_kernel(lamqk_ref, subg_ref, q_ref, k_ref, v_ref, o_ref,
                 qm_sc, m_sc, l_sc, acc_sc, *, lam_init, nk):
    ki = pl.program_id(3)

    @pl.when(ki == 0)
    def _init():
        q = q_ref[...]
        lane = lax.broadcasted_iota(jnp.int32, q.shape, 1)
        zero = jnp.zeros_like(q)
        qm_sc[0] = jnp.where(lane < HEAD_DIM, q, zero)
        qm_sc[1] = jnp.where(lane >= HEAD_DIM, q, zero)
        m_sc[...] = jnp.full(m_sc.shape, NEG_BIG, F32)
        l_sc[...] = jnp.zeros(l_sc.shape, F32)
        acc_sc[...] = jnp.zeros(acc_sc.shape, F32)

    k = k_ref[...]
    v = v_ref[...]
    for c in range(2):
        s = lax.dot_general(qm_sc[c], k, NT_DIMS, preferred_element_type=F32)
        m_prev = m_sc[c]
        m_new = jnp.maximum(m_prev, jnp.max(s, axis=1, keepdims=True))
        alpha = jnp.exp2(m_prev - m_new)
        p = jnp.exp2(s - m_new)
        l_sc[c] = alpha * l_sc[c] + jnp.sum(p, axis=1, keepdims=True)
        acc_sc[c] = alpha * acc_sc[c] + jnp.dot(p.astype(BF16), v, preferred_element_type=F32)
        m_sc[c] = m_new

    @pl.when(ki == nk - 1)
    def _fin():
        lq = lamqk_ref[...]
        lam = (jnp.exp(jnp.sum(lq[0:1] * lq[1:2], axis=1, keepdims=True))
               - jnp.exp(jnp.sum(lq[2:3] * lq[3:4], axis=1, keepdims=True)) + lam_init)
        o = acc_sc[0] / l_sc[0] - lam * (acc_sc[1] / l_sc[1])
        o_ref[...] = (_rms(o, subg_ref[...]) * (1.0 - lam_init)).astype(o_ref.dtype)


def _attn_bounded_kernel(lamqk_ref, subg_ref, kmax_ref, q_ref, k_ref, v_ref, o_ref,
                         qx_sc, kx_sc, l_sc, acc_sc, *, lam_init, nk):
    ki = pl.program_id(3)
    tq = q_ref.shape[0]
    tk = k_ref.shape[0]

    @pl.when(ki == 0)
    def _init():
        q = q_ref[...]
        lane = lax.broadcasted_iota(jnp.int32, (tq, LANES), 1)
        qf = q.astype(F32)
        n2 = jnp.dot((qf * qf).astype(BF16), _map_segments(), preferred_element_type=F32)
        bound = jnp.sqrt(n2) * kmax_ref[...] * BOUND_SLACK
        zero = jnp.zeros_like(q)
        qx_sc[0, :, 0:LANES] = jnp.where(lane < HEAD_DIM, q, zero)
        qx_sc[0, :, LANES:2 * LANES] = jnp.where(lane == 0, -bound, 0.0).astype(BF16)
        qx_sc[1, :, 0:LANES] = jnp.where(lane >= HEAD_DIM, q, zero)
        qx_sc[1, :, LANES:2 * LANES] = jnp.where(lane == 0, -pltpu.roll(bound, HEAD_DIM, 1), 0.0).astype(BF16)
        lane_k = lax.broadcasted_iota(jnp.int32, (tk, LANES), 1)
        kx_sc[:, LANES:2 * LANES] = jnp.where(lane_k == 0, 1.0, 0.0).astype(BF16)
        l_sc[...] = jnp.zeros(l_sc.shape, F32)
        acc_sc[...] = jnp.zeros(acc_sc.shape, F32)

    kx_sc[:, 0:LANES] = k_ref[...]
    kx = kx_sc[...]
    v = v_ref[...]
    for c in range(2):
        s = lax.dot_general(qx_sc[c], kx, NT_DIMS, preferred_element_type=F32)
        p = jnp.exp2(s)
        part = p[:, 0:LANES]
        for j in range(1, tk // LANES):
            part = part + p[:, j * LANES:(j + 1) * LANES]
        l_sc[c] += part
        acc_sc[c] += jnp.dot(p.astype(BF16), v, preferred_element_type=F32)

    @pl.when(ki == nk - 1)
    def _fin():
        lq = lamqk_ref[...]
        lam = (jnp.exp(jnp.sum(lq[0:1] * lq[1:2], axis=1, keepdims=True))
               - jnp.exp(jnp.sum(lq[2:3] * lq[3:4], axis=1, keepdims=True)) + lam_init)
        l0 = jnp.sum(l_sc[0], axis=1, keepdims=True)
        l1 = jnp.sum(l_sc[1], axis=1, keepdims=True)
        o = acc_sc[0] / l0 - lam * (acc_sc[1] / l1)
        o_ref[...] = (_rms(o, subg_ref[...]) * (1.0 - lam_init)).astype(o_ref.dtype)


def _drop_input(kern, idx, *refs):
    return kern(*refs[:idx], *refs[idx + 1:])


def _attention(layer, q, k, v, kmax, lam_qk, subln_g, lam_init, *, bounded, tq, tk, q0, nq, k0, nk,
               o_prev=None):
    b, n, _ = q.shape
    q_idx = lambda bi, h, qi, ki: (bi, qi + q0, h)
    kv_idx = lambda bi, h, qi, ki: (bi, ki + k0, h)
    in_specs = [pl.BlockSpec((None, 4, HEAD_DIM), lambda *_: (layer, 0, 0)),
                pl.BlockSpec((None, 1, V_DIM), lambda *_: (layer, 0, 0)),
                pl.BlockSpec((None, None, 1, LANES), lambda bi, h, qi, ki: (bi, h, 0, 0)),
                pl.BlockSpec((None, tq, LANES), q_idx),
                pl.BlockSpec((None, tk, LANES), kv_idx),
                pl.BlockSpec((None, tk, LANES), kv_idx)]
    args = [lam_qk, subln_g, kmax, q, k, v]
    if bounded:
        kern = functools.partial(_attn_bounded_kernel, lam_init=lam_init, nk=nk)
        scratch = [pltpu.VMEM((2, tq, 2 * LANES), BF16),
                   pltpu.VMEM((tk, 2 * LANES), BF16),
                   pltpu.VMEM((2, tq, LANES), F32),
                   pltpu.VMEM((2, tq, V_DIM), F32)]
    else:
        kern = functools.partial(_drop_input, functools.partial(_attn_kernel, lam_init=lam_init, nk=nk), 2)
        scratch = [pltpu.VMEM((2, tq, LANES), BF16),
                   pltpu.VMEM((2, tq, 1), F32),
                   pltpu.VMEM((2, tq, 1), F32),
                   pltpu.VMEM((2, tq, V_DIM), F32)]
    aliases = {}
    if o_prev is not None:
        in_specs.append(pl.BlockSpec(memory_space=pl.ANY))
        args.append(o_prev)
        aliases = {6: 0}
        kern = functools.partial(_drop_input, kern, 6)
    return pl.pallas_call(
        kern,
        grid=(b, N_HEADS, nq, nk),
        in_specs=in_specs,
        out_specs=pl.BlockSpec((None, tq, LANES), q_idx),
        out_shape=jax.ShapeDtypeStruct((b, n, ATTN_W), BF16),
        scratch_shapes=scratch,
        input_output_aliases=aliases,
        compiler_params=_cparams(("parallel", "parallel", "parallel", "arbitrary")),
        name="diff_attn_bounded" if bounded else "diff_attn",
    )(*args)


def _attention_layer(layer, q, k, v, nrm, lam_qk, subln_g, lam_init, seq, with_ctx):
    b = q.shape[0]
    nkeys = seq + CTX_LEN
    top = jnp.sqrt(jnp.max(nrm[:, :, 0:2, 0:2 * N_HEADS], axis=1))
    qmax, kmax = top[:, :, 0:N_HEADS], top[:, :, N_HEADS:]
    small = jnp.max(qmax * kmax) * BOUND_SLACK < BOUND_LIMIT
    kmax4 = jnp.repeat(jnp.swapaxes(kmax, 1, 2), HEAD_DIM, axis=-1).reshape(b, N_HEADS, 1, LANES)

    def run(bounded):
        o_n = _attention(layer, q, k, v, kmax4, lam_qk, subln_g, lam_init, bounded=bounded,
                         tq=TQ, tk=TK, q0=0, nq=seq // TQ, k0=0, nk=nkeys // TK)
        if with_ctx:
            o_n = _attention(layer, q, k, v, kmax4, lam_qk, subln_g, lam_init, bounded=bounded,
                             tq=TM, tk=CTX_LEN, q0=seq // TM, nq=1, k0=seq // CTX_LEN, nk=1,
                             o_prev=o_n)
        return o_n

    return lax.cond(small, lambda: run(True), lambda: run(False))


def _cmul(ar, ai, br, bi):
    return ar * br - ai * bi, ar * bi + ai * br


def _cpow_by_bits(lr, li, expo, nbits):
    pr = jnp.ones(expo.shape, F32)
    pi = jnp.zeros(expo.shape, F32)
    br, bi = lr, li
    for bit in range(nbits):
        on = ((expo >> bit) & 1) == 1
        fr = jnp.where(on, br, 1.0)
        fi = jnp.where(on, bi, 0.0)
        pr, pi = _cmul(pr, pi, fr, fi)
        if bit + 1 < nbits:
            br, bi = _cmul(br, bi, br, bi)
    return pr, pi


def _discretize(ar, ai, log_dt):
    dt = jnp.exp(log_dt)
    er = jnp.exp(ar * dt)
    lbr = er * jnp.cos(ai * dt)
    lbi = er * jnp.sin(ai * dt)
    den = ar * ar + ai * ai
    nr = lbr - 1.0
    cr = (nr * ar + lbi * ai) / den
    ci = (lbi * ar - nr * ai) / den
    return lbr, lbi, cr, ci


def _s5_kernel(x_ref, ar_ref, ai_ref, ldt_ref, btr_ref, bti_ref, ctr_ref, cti_ref,
               y_ref, sext, toep, csr_sc, csi_sc, zre, zim, hre, him, *, nrows, nchunk, nbatch, nctx):
    nlat = nchunk - nctx
    blk = lax.broadcasted_iota(jnp.int32, (S5_TC, PAIR), 0) // SSM_GROUP
    lane_g = lax.broadcasted_iota(jnp.int32, (1, PAIR), 1) // SSM_STATE

    for d in range(2):
        fwd = d == 0
        step_r, step_i = [], []
        for pi in range(S5_NP):
            lbr, lbi, cr, ci = _discretize(ar_ref[d, pi], ai_ref[d, pi], ldt_ref[d, pi])
            bbr, bbi = _cmul(cr, ci, btr_ref[d, pi], bti_ref[d, pi])
            bbr_t = jnp.concatenate([bbr] * S5_T, axis=0)
            bbi_t = jnp.concatenate([bbi] * S5_T, axis=0)
            ctr_t = jnp.concatenate([ctr_ref[d, pi]] * S5_T, axis=0)
            cti_t = jnp.concatenate([cti_ref[d, pi]] * S5_T, axis=0)
            upr, upi = _cpow_by_bits(lbr, lbi, blk, 4)
            dnr, dni = _cpow_by_bits(lbr, lbi, S5_T - 1 - blk, 4)
            (per, pei), (pwr, pwi) = ((upr, upi), (dnr, dni)) if fwd else ((dnr, dni), (upr, upi))
            bsr, bsi = _cmul(pwr, pwi, bbr_t, bbi_t)
            wcr, wci = _cmul(per, pei, ctr_t, cti_t)
            c1r, c1i = _cmul(wcr, wci, lbr, lbi)
            tr, ti = lbr, lbi
            for _ in range(int(math.log2(S5_T))):
                tr, ti = _cmul(tr, ti, tr, ti)
            step_r.append(tr)
            step_i.append(ti)

            z_r = z_i = None
            for gg in range(2):
                g = 2 * pi + gg
                lm = lane_g == gg
                strip = (lax.dot_general(jnp.where(lm, bbr, 0.0), wcr, NT_DIMS, preferred_element_type=F32,
                                         precision=lax.Precision.HIGHEST)
                         - lax.dot_general(jnp.where(lm, bbi, 0.0), wci, NT_DIMS, preferred_element_type=F32,
                                           precision=lax.Precision.HIGHEST))
                if fwd:
                    sext[:, 0:S5_TC] = jnp.zeros((SSM_GROUP, S5_TC), F32)
                    sext[:, S5_TC:2 * S5_TC] = strip
                    for j in range(S5_T):
                        lo = S5_TC - SSM_GROUP * j
                        toep[j * SSM_GROUP:(j + 1) * SSM_GROUP, :] = sext[:, lo:lo + S5_TC].astype(BF16)
                else:
                    sext[:, 0:S5_TC] = strip
                    sext[:, S5_TC:2 * S5_TC] = jnp.zeros((SSM_GROUP, S5_TC), F32)
                    for j in range(S5_T):
                        lo = SSM_GROUP * (S5_T - 1 - j)
                        toep[j * SSM_GROUP:(j + 1) * SSM_GROUP, :] = sext[:, lo:lo + S5_TC].astype(BF16)
                x = x_ref[g]
                yi = jnp.dot(x, toep[...], preferred_element_type=F32)
                if fwd:
                    y_ref[g] = yi
                else:
                    y_ref[g] += yi
                zr = jnp.dot(x, jnp.where(lm, bsr, 0.0).astype(BF16), preferred_element_type=F32)
                zi = jnp.dot(x, jnp.where(lm, bsi, 0.0).astype(BF16), preferred_element_type=F32)
                z_r = zr if z_r is None else z_r + zr
                z_i = zi if z_i is None else z_i + zi
                csr_sc[g] = jnp.where(lm, c1r, 0.0).astype(BF16)
                csi_sc[g] = jnp.where(lm, c1i, 0.0).astype(BF16)
            for b in range(nbatch):
                s = pi * nbatch + b
                zre[pl.ds(s, nrows, stride=SUBLANES), :] = z_r[b * nrows:(b + 1) * nrows]
                zim[pl.ds(s, nrows, stride=SUBLANES), :] = z_i[b * nrows:(b + 1) * nrows]

        sub = lax.broadcasted_iota(jnp.int32, (SUBLANES, PAIR), 0) // nbatch
        tr = jnp.zeros((SUBLANES, PAIR), F32)
        ti = jnp.zeros((SUBLANES, PAIR), F32)
        for pi in range(S5_NP):
            tr = jnp.where(sub == pi, step_r[pi], tr)
            ti = jnp.where(sub == pi, step_i[pi], ti)
        hre[...] = jnp.zeros(hre.shape, F32)
        him[...] = jnp.zeros(him.shape, F32)

        def body(n, carry):
            if fwd:
                ch = jnp.where(n < nctx, nlat + n, n - nctx)
            else:
                ch = nchunk - 1 - n
            h_r, h_i = carry
            r = pl.multiple_of(ch * SUBLANES, SUBLANES)
            hre[pl.ds(r, SUBLANES), :] = h_r
            him[pl.ds(r, SUBLANES), :] = h_i
            n_r, n_i = _cmul(tr, ti, h_r, h_i)
            return n_r + zre[pl.ds(r, SUBLANES), :], n_i + zim[pl.ds(r, SUBLANES), :]

        zero = jnp.zeros((SUBLANES, PAIR), F32)
        lax.fori_loop(0, nchunk, body, (zero, zero), unroll=8)

        for pi in range(S5_NP):
            for b in range(nbatch):
                s = pi * nbatch + b
                h_r = hre[pl.ds(s, nrows, stride=SUBLANES), :].astype(BF16)
                h_i = him[pl.ds(s, nrows, stride=SUBLANES), :].astype(BF16)
                for gg in range(2):
                    g = 2 * pi + gg
                    y_ref[g, b * nrows:(b + 1) * nrows, :] += (
                        lax.dot_general(h_r, csr_sc[g], NT_DIMS, preferred_element_type=F32)
                        - lax.dot_general(h_i, csi_sc[g], NT_DIMS, preferred_element_type=F32))


def _s5_params(a_re, a_im, log_dt, b_re, b_im, c_re, c_im):
    depth = a_re.shape[0]
    gp = SSM_GROUPS // 2
    row = lambda a: a.reshape(depth, 2, gp, 1, PAIR)
    ldt = jnp.broadcast_to(log_dt[..., None], a_re.shape)
    bt = lambda a: jnp.transpose(a.reshape(depth, 2, gp, 2, SSM_STATE, SSM_GROUP),
                                 (0, 1, 2, 5, 3, 4)).reshape(depth, 2, gp, SSM_GROUP, PAIR)
    ct = lambda a: jnp.transpose(a.reshape(depth, 2, gp, 2, SSM_GROUP, SSM_STATE),
                                 (0, 1, 2, 4, 3, 5)).reshape(depth, 2, gp, SSM_GROUP, PAIR)
    return row(a_re), row(a_im), row(ldt), bt(b_re), bt(b_im), ct(c_re), ct(c_im)


def _s5(layer, xg, params, nbatch, nvalid):
    g, nr, _ = xg.shape
    gstep = 2 * S5_NP
    assert S5_NP * nbatch == SUBLANES
    vec = pl.BlockSpec((None, 2, S5_NP, 1, PAIR), lambda i: (layer, 0, i, 0, 0))
    mat = pl.BlockSpec((None, 2, S5_NP, SSM_GROUP, PAIR), lambda i: (layer, 0, i, 0, 0))
    kern = functools.partial(_s5_kernel, nrows=nr // nbatch, nchunk=nvalid // S5_T, nbatch=nbatch,
                             nctx=CTX_LEN // S5_T)
    return pl.pallas_call(
        kern,
        grid=(g // gstep,),
        in_specs=[pl.BlockSpec((gstep, nr, S5_TC), lambda i: (i, 0, 0)),
                  vec, vec, vec, mat, mat, mat, mat],
        out_specs=pl.BlockSpec((gstep, nr, S5_TC), lambda i: (i, 0, 0)),
        out_shape=jax.ShapeDtypeStruct((g, nr, S5_TC), F32),
        scratch_shapes=[pltpu.VMEM((SSM_GROUP, 2 * S5_TC), F32),
                        pltpu.VMEM((S5_TC, S5_TC), BF16),
                        pltpu.VMEM((gstep, S5_TC, PAIR), BF16),
                        pltpu.VMEM((gstep, S5_TC, PAIR), BF16),
                        pltpu.VMEM((nr // nbatch * SUBLANES, PAIR), F32),
                        pltpu.VMEM((nr // nbatch * SUBLANES, PAIR), F32),
                        pltpu.VMEM((nr // nbatch * SUBLANES, PAIR), F32),
                        pltpu.VMEM((nr // nbatch * SUBLANES, PAIR), F32)],
        compiler_params=_cparams(("parallel",)),
        name="s5_scan",
    )(xg, *params)


def _tail_kernel(xl_ref, xc_ref, on_ref, yg_ref, u_ref, gate_ref, mod_ref, d_ref, wglu_ref, bglu_ref,
                 wbs_ref, wba_ref, wout_ref, ng_ref, w1_ref, b1_ref, w2_ref, b2_ref,
                 o_ref, ysc, *, n_lat_tiles):
    ys = _chunks_to_rows(yg_ref, ysc) + u_ref[...] * d_ref[...]
    gl = jax.nn.gelu(ys)
    z = gl * jax.nn.sigmoid(jnp.dot(gl.astype(BF16), wglu_ref[...], preferred_element_type=F32)
                            + bglu_ref[...])
    ps = jnp.dot(z.astype(BF16), wbs_ref[...], preferred_element_type=F32)
    pa = jnp.dot(on_ref[...], wba_ref[...], preferred_element_type=F32)
    gate = gate_ref[...].astype(F32)
    mix = gate[:, 0:D_MODEL] * pa + gate[:, D_MODEL:2 * D_MODEL] * ps
    m2 = jnp.dot(mix.astype(BF16), wout_ref[...], preferred_element_type=F32)
    x1 = _stream_tile(xl_ref, xc_ref, n_lat_tiles) + mod_ref[:, 2 * D_MODEL:3 * D_MODEL] * _rms(m2, ng_ref[1:2])
    h = _rms(x1, ng_ref[2:3]) * (1.0 + mod_ref[:, 4 * D_MODEL:5 * D_MODEL]) \
        + mod_ref[:, 3 * D_MODEL:4 * D_MODEL]
    f = jnp.dot(h.astype(BF16), w1_ref[...], preferred_element_type=F32) + b1_ref[...]
    f = jnp.square(jnp.maximum(f, 0.0))
    o = jnp.dot(f.astype(BF16), w2_ref[...], preferred_element_type=F32) + b2_ref[...]
    o_ref[...] = x1 + mod_ref[:, 5 * D_MODEL:6 * D_MODEL] * _rms(o, ng_ref[3:4])


def _tail(layer, xl, xc, o_n, yg, u, gates, modsel, ssm_d, w_glu, b_glu, w_br_s, w_br_a, w_out, norm_g,
          w1, b1, w2, b2, n_lat_tiles, nt):
    b = xl.shape[0]
    nt_all = n_lat_tiles + 1
    row = lambda bi, i: (bi, i, 0)
    return pl.pallas_call(
        functools.partial(_tail_kernel, n_lat_tiles=n_lat_tiles),
        grid=(b, nt),
        in_specs=_stream_specs(n_lat_tiles) + [
            pl.BlockSpec((None, TM, ATTN_W), row),
            pl.BlockSpec((SSM_GROUPS, TM // S5_T, S5_TC), lambda bi, i: (0, bi * nt_all + i, 0)),
            pl.BlockSpec((None, TM, SSM_W), row),
            pl.BlockSpec((None, TM, 2 * D_MODEL), row),
            _mod_spec(layer, n_lat_tiles),
            _layer_spec(layer, (1, SSM_W)),
            _layer_spec(layer, (SSM_W, SSM_W)),
            _layer_spec(layer, (1, SSM_W)),
            _layer_spec(layer, (SSM_W, D_MODEL)),
            _layer_spec(layer, (ATTN_W, D_MODEL)),
            _layer_spec(layer, (D_MODEL, D_MODEL)),
            _layer_spec(layer, (4, D_MODEL)),
            _layer_spec(layer, (D_MODEL, D_FF)),
            _layer_spec(layer, (1, D_FF)),
            _layer_spec(layer, (D_FF, D_MODEL)),
            _layer_spec(layer, (1, D_MODEL))],
        out_specs=pl.BlockSpec((None, TM, D_MODEL), row),
        out_shape=jax.ShapeDtypeStruct((b, nt * TM, D_MODEL), F32),
        scratch_shapes=[pltpu.VMEM((S5_SLABS, TM, LANES), F32)],
        compiler_params=_cparams(("parallel", "arbitrary")),
        name="merge_mlp",
    )(xl, xc, o_n, yg, u, gates, modsel, ssm_d, w_glu, b_glu, w_br_s, w_br_a, w_out, norm_g,
      w1, b1, w2, b2)


def _rope_tables(n_tokens):
    rows = n_tokens // GRID_W
    row = jnp.repeat(jnp.arange(rows, dtype=jnp.int32), GRID_W).astype(F32)
    col = jnp.tile(jnp.arange(GRID_W, dtype=jnp.int32), rows).astype(F32)
    inv_freq = ROPE_BASE ** (-jnp.arange(N_FREQ, dtype=F32) / N_FREQ)
    ang = jnp.stack([row[:, None] * inv_freq, col[:, None] * inv_freq], axis=1)
    cos, sin = jnp.cos(ang), jnp.sin(ang)
    cos_l = jnp.tile(jnp.concatenate([cos[:, 0], cos[:, 0], cos[:, 1], cos[:, 1]], axis=1), (1, 2))
    sin_l = jnp.tile(jnp.concatenate([-sin[:, 0], sin[:, 0], -sin[:, 1], sin[:, 1]], axis=1), (1, 2))
    cos_t = jnp.concatenate([cos_l, jnp.ones((TM, LANES), F32)], axis=0)
    sin_t = jnp.concatenate([sin_l, jnp.zeros((TM, LANES), F32)], axis=0)
    return cos_t, sin_t


def kernel(x, c, ctx, c_ctx, ada_w, ada_b, norm_g, w_in, gate_b, lam_qk, subln_g, w_br_a,
           ssm_a_re, ssm_a_im, ssm_b_re, ssm_b_im, ssm_c_re, ssm_c_im, ssm_log_dt, ssm_d,
           w_glu, b_glu, w_br_s, w_out, w_mlp1, b_mlp1, w_mlp2, b_mlp2):
    b, seq, _ = x.shape
    nvalid = seq + CTX_LEN
    assert b + 1 <= 8 and ctx.shape[1] == CTX_LEN and seq % TM == 0 and seq % TQ == 0 and nvalid % TK == 0
    n_lat_tiles = seq // TM
    cos_t, sin_t = _rope_tables(seq)
    xl, xc = x, jnp.concatenate([ctx, jnp.zeros((b, ROWS_PAD, D_MODEL), F32)], axis=1)
    cc = jnp.zeros((8, D_MODEL), F32).at[:b].set(c).at[b].set(c_ctx)
    mod = _modulation(cc, ada_w, ada_b)
    modsel = jnp.stack([mod[:, :b], jnp.broadcast_to(mod[:, b:b + 1], (DEPTH, b, 6 * D_MODEL))], axis=2)
    modsel = modsel.reshape(DEPTH, b, 2, 1, 6 * D_MODEL)
    s5_params = _s5_params(ssm_a_re, ssm_a_im, ssm_log_dt, ssm_b_re, ssm_b_im, ssm_c_re, ssm_c_im)
    row3 = lambda a: a.reshape(DEPTH, 1, -1)
    w_in_b, w_glu_b, w_br_s_b, w_br_a_b, w_out_b, w1_b, w2_b = (
        w.astype(BF16) for w in (w_in, w_glu, w_br_s, w_br_a, w_out, w_mlp1, w_mlp2))
    for i in range(DEPTH):
        last = i == DEPTH - 1
        lam_init = 0.8 - 0.6 * math.exp(-0.3 * i)
        q, k, v, u, gates, xg, nrm = _in_proj(i, xl, xc, modsel, norm_g, w_in_b, row3(gate_b),
                                              cos_t, sin_t, n_lat_tiles)
        o_n = _attention_layer(i, q, k, v, nrm, lam_qk, row3(subln_g), lam_init, seq, with_ctx=not last)
        yg = _s5(i, xg, s5_params, b, nvalid)
        xl = _tail(i, xl, xc, o_n, yg, u, gates, modsel, row3(ssm_d), w_glu_b, row3(b_glu), w_br_s_b,
                   w_br_a_b, w_out_b, norm_g, w1_b, row3(b_mlp1), w2_b, row3(b_mlp2),
                   n_lat_tiles, n_lat_tiles if last else n_lat_tiles + 1)
        if not last:
            xc = xl[:, seq:]
    return xl
```

```python
import functools
import math

import jax
import jax.numpy as jnp
from jax import lax
from jax.experimental import pallas as pl
from jax.experimental.pallas import tpu as pltpu

F32 = jnp.float32
BF16 = jnp.bfloat16

D_MODEL = 1024
DEPTH = 2
GRID_W = 64
CTX_LEN = 256
N_HEADS = 4
HEAD_DIM = 64
V_DIM = 128
QK_W = N_HEADS * 2 * HEAD_DIM
ATTN_W = 512
SSM_W = 512
SSM_GROUP = 16
SSM_GROUPS = 32
SSM_STATE = 64
D_FF = 4096
IN_COLS = 4096
N_FREQ = 16
ROPE_BASE = 10000.0
EPS = 1e-6
LOG2E = 1.4426950408889634

LANES = 128
SUBLANES = 8
TM = 512
ROWS_PAD = TM - CTX_LEN
GATHER_ROWS = 16
TQ = 1024
TK = 2816
S5_T = 16
S5_TC = S5_T * SSM_GROUP
S5_SLABS = SSM_W // LANES
S5_GPS = LANES // SSM_GROUP
PAIR = 2 * SSM_STATE
S5_NP = 4
VMEM_LIMIT = 56 * 1024 * 1024

NEG_BIG = -1e30
BOUND_SLACK = 1.03
BOUND_LIMIT = 50.0

NT_DIMS = (((1,), (1,)), ((), ()))


def _cparams(sem):
    return pltpu.CompilerParams(dimension_semantics=sem, vmem_limit_bytes=VMEM_LIMIT)


def _layer_spec(layer, shape):
    nd = len(shape)
    return pl.BlockSpec((None,) + shape, lambda *_: (layer,) + (0,) * nd, pipeline_mode=pl.Buffered(1))


def _mod_spec(layer, ctx_tile):
    return pl.BlockSpec((None, None, None, 1, 6 * D_MODEL),
                        lambda bi, i: (layer, bi, i // ctx_tile, 0, 0))


def _stream_specs(n_lat_tiles):
    return [pl.BlockSpec((None, TM, D_MODEL), lambda bi, i: (bi, jnp.minimum(i, n_lat_tiles - 1), 0)),
            pl.BlockSpec((None, TM, D_MODEL), lambda bi, i: (bi, 0, 0))]


def _stream_tile(xl_ref, xc_ref, n_lat_tiles):
    return jnp.where(pl.program_id(1) >= n_lat_tiles, xc_ref[...], xl_ref[...])


def _rms(x, g):
    ms = jnp.mean(x * x, axis=-1, keepdims=True)
    return x * lax.rsqrt(ms + EPS) * g


def _mod_kernel(c_ref, w_ref, b_ref, o_ref):
    c = c_ref[...]
    s = c * jax.nn.sigmoid(c)
    o_ref[...] = jnp.dot(s.astype(BF16), w_ref[...].astype(BF16),
                         preferred_element_type=F32) + b_ref[...]


def _modulation(cc, ada_w, ada_b):
    depth, _, n = ada_w.shape
    bn = 1024
    return pl.pallas_call(
        _mod_kernel,
        grid=(depth, n // bn),
        in_specs=[pl.BlockSpec((8, D_MODEL), lambda l, j: (0, 0)),
                  pl.BlockSpec((None, D_MODEL, bn), lambda l, j: (l, 0, j)),
                  pl.BlockSpec((None, 1, bn), lambda l, j: (l, 0, j))],
        out_specs=pl.BlockSpec((None, 8, bn), lambda l, j: (l, 0, j)),
        out_shape=jax.ShapeDtypeStruct((depth, 8, n), F32),
        compiler_params=_cparams(("arbitrary", "arbitrary")),
        name="adaln_mod",
    )(cc, ada_w, ada_b.reshape(depth, 1, n))


def _lane_window(off):
    lane = lax.broadcasted_iota(jnp.int32, (GATHER_ROWS, LANES), 1)
    return (lane >= off) & (lane < off + SSM_GROUP)


def _rows_to_chunks(u, usc, xg_ref):
    for sl in range(S5_SLABS):
        usc[sl] = u[:, sl * LANES:(sl + 1) * LANES]
    for part in range(TM // (S5_T * GATHER_ROWS)):
        r0 = part * GATHER_ROWS
        for sl in range(S5_SLABS):
            acc = [[None] * (S5_TC // LANES) for _ in range(S5_GPS)]
            for t in range(S5_T):
                v = usc[sl, pl.ds(r0 * S5_T + t, GATHER_ROWS, stride=S5_T), :]
                lt, off = divmod(t * SSM_GROUP, LANES)
                win = _lane_window(off)
                for gl in range(S5_GPS):
                    shift = (off - gl * SSM_GROUP) % LANES
                    moved = pltpu.roll(v, shift, 1) if shift else v
                    prev = acc[gl][lt]
                    acc[gl][lt] = jnp.where(win, moved, 0.0 if prev is None else prev)
            for gl in range(S5_GPS):
                for lt in range(S5_TC // LANES):
                    xg_ref[sl * S5_GPS + gl, r0:r0 + GATHER_ROWS, lt * LANES:(lt + 1) * LANES] = (
                        acc[gl][lt].astype(xg_ref.dtype))


def _chunks_to_rows(yg_ref, ysc):
    for part in range(TM // (S5_T * GATHER_ROWS)):
        r0 = part * GATHER_ROWS
        for sl in range(S5_SLABS):
            tiles = [[yg_ref[sl * S5_GPS + gl, r0:r0 + GATHER_ROWS, lt * LANES:(lt + 1) * LANES]
                      for lt in range(S5_TC // LANES)] for gl in range(S5_GPS)]
            for t in range(S5_T):
                lt, off = divmod(t * SSM_GROUP, LANES)
                v = None
                for gl in range(S5_GPS):
                    shift = (gl * SSM_GROUP - off) % LANES
                    src = tiles[gl][lt]
                    moved = pltpu.roll(src, shift, 1) if shift else src
                    v = jnp.where(_lane_window(gl * SSM_GROUP), moved, 0.0 if v is None else v)
                ysc[sl, pl.ds(r0 * S5_T + t, GATHER_ROWS, stride=S5_T), :] = v
    return jnp.concatenate([ysc[sl] for sl in range(S5_SLABS)], axis=1)


def _map_segments():
    li = lax.broadcasted_iota(jnp.int32, (LANES, LANES), 0) // HEAD_DIM
    lj = lax.broadcasted_iota(jnp.int32, (LANES, LANES), 1) // HEAD_DIM
    return (li == lj).astype(BF16)


def _in_kernel(xl_ref, xc_ref, mod_ref, g_ref, w_ref, gb_ref, cos_ref, sin_ref,
               q_ref, k_ref, v_ref, u_ref, gate_ref, xg_ref, nrm_ref, usc, *, n_lat_tiles):
    x = _stream_tile(xl_ref, xc_ref, n_lat_tiles)
    h = _rms(x, g_ref[0:1]) * (1.0 + mod_ref[:, D_MODEL:2 * D_MODEL]) + mod_ref[:, 0:D_MODEL]
    hb = h.astype(BF16)
    cos = cos_ref[...]
    sin = sin_ref[...]
    lane = lax.broadcasted_iota(jnp.int32, (TM, LANES), 1)
    first_half = (lane & 31) < 16
    sel = (lax.broadcasted_iota(jnp.int32, (8, LANES), 0)
           == lax.broadcasted_iota(jnp.int32, (8, LANES), 1) // HEAD_DIM).astype(BF16)
    slot = lax.broadcasted_iota(jnp.int32, (8, LANES), 1)
    nrm = jnp.zeros((8, LANES), F32)

    def rope_store(col0, out_ref, scale, slot0, nrm):
        t = jnp.dot(hb, w_ref[:, col0:col0 + QK_W], preferred_element_type=F32)
        for j in range(QK_W // LANES):
            tj = t[:, j * LANES:(j + 1) * LANES]
            partner = jnp.where(first_half, pltpu.roll(tj, LANES - 16, 1), pltpu.roll(tj, 16, 1))
            r = tj * cos + partner * sin
            if scale != 1.0:
                r = r * scale
            out_ref[:, j * LANES:(j + 1) * LANES] = r.astype(out_ref.dtype)
            n2 = lax.dot_general(sel, (r * r).astype(BF16), NT_DIMS, preferred_element_type=F32)
            nrm = jnp.where(slot == slot0 + j, jnp.max(n2, axis=1, keepdims=True), nrm)
        return nrm

    nrm = rope_store(0, q_ref, HEAD_DIM ** -0.5 * LOG2E, 0, nrm)
    nrm = rope_store(QK_W, k_ref, 1.0, N_HEADS, nrm)
    nrm_ref[...] = nrm
    v_ref[...] = jnp.dot(hb, w_ref[:, 1024:1536], preferred_element_type=F32).astype(v_ref.dtype)
    u = jnp.dot(hb, w_ref[:, 1536:2048], preferred_element_type=F32)
    u_ref[...] = u
    _rows_to_chunks(u, usc, xg_ref)
    g = jnp.dot(hb, w_ref[:, 2048:IN_COLS], preferred_element_type=F32) + gb_ref[...]
    gate_ref[...] = jax.nn.sigmoid(g).astype(gate_ref.dtype)


def _in_proj(layer, xl, xc, modsel, norm_g, w_in_b, gate_b, cos_t, sin_t, n_lat_tiles):
    b = xl.shape[0]
    nt = n_lat_tiles + 1
    n = nt * TM
    row = lambda bi, i: (bi, i, 0)
    return pl.pallas_call(
        functools.partial(_in_kernel, n_lat_tiles=n_lat_tiles),
        grid=(b, nt),
        in_specs=_stream_specs(n_lat_tiles) + [
            _mod_spec(layer, n_lat_tiles),
            _layer_spec(layer, (4, D_MODEL)),
            _layer_spec(layer, (D_MODEL, IN_COLS)),
            _layer_spec(layer, (1, 2 * D_MODEL)),
            pl.BlockSpec((TM, LANES), lambda bi, i: (i, 0)),
            pl.BlockSpec((TM, LANES), lambda bi, i: (i, 0))],
        out_specs=[pl.BlockSpec((None, TM, QK_W), row),
                   pl.BlockSpec((None, TM, QK_W), row),
                   pl.BlockSpec((None, TM, ATTN_W), row),
                   pl.BlockSpec((None, TM, SSM_W), row),
                   pl.BlockSpec((None, TM, 2 * D_MODEL), row),
                   pl.BlockSpec((SSM_GROUPS, TM // S5_T, S5_TC), lambda bi, i: (0, bi * nt + i, 0)),
                   pl.BlockSpec((None, None, 8, LANES), lambda bi, i: (bi, i, 0, 0))],
        out_shape=[jax.ShapeDtypeStruct((b, n, QK_W), BF16),
                   jax.ShapeDtypeStruct((b, n, QK_W), BF16),
                   jax.ShapeDtypeStruct((b, n, ATTN_W), BF16),
                   jax.ShapeDtypeStruct((b, n, SSM_W), F32),
                   jax.ShapeDtypeStruct((b, n, 2 * D_MODEL), BF16),
                   jax.ShapeDtypeStruct((SSM_GROUPS, b * n // S5_T, S5_TC), BF16),
                   jax.ShapeDtypeStruct((b, nt, 8, LANES), F32)],
        scratch_shapes=[pltpu.VMEM((S5_SLABS, TM, LANES), F32)],
        compiler_params=_cparams(("parallel", "arbitrary")),
        name="in_proj",
    )(xl, xc, modsel, norm_g, w_in_b, gate_b, cos_t, sin_t)


def _attn_kernel(lamqk_ref, subg_ref, q_ref, k_ref, v_ref, o_ref,
                 qm_sc, m_sc, l_sc, acc_sc, *, lam_init, nk):
    ki = pl.program_id(3)

    @pl.when(ki == 0)
    def _init():
        q = q_ref[...]
        lane = lax.broadcasted_iota(jnp.int32, q.shape, 1)
        zero = jnp.zeros_like(q)
        qm_sc[0] = jnp.where(lane < HEAD_DIM, q, zero)
        qm_sc[1] = jnp.where(lane >= HEAD_DIM, q, zero)
        m_sc[...] = jnp.full(m_sc.shape, NEG_BIG, F32)
        l_sc[...] = jnp.zeros(l_sc.shape, F32)
        acc_sc[...] = jnp.zeros(acc_sc.shape, F32)

    k = k_ref[...]
    v = v_ref[...]
    for c in range(2):
        s = lax.dot_general(qm_sc[c], k, NT_DIMS, preferred_element_type=F32)
        m_prev = m_sc[c]
        m_new = jnp.maximum(m_prev, jnp.max(s, axis=1, keepdims=True))
        alpha = jnp.exp2(m_prev - m_new)
        p = jnp.exp2(s - m_new)
        l_sc[c] = alpha * l_sc[c] + jnp.sum(p, axis=1, keepdims=True)
        acc_sc[c] = alpha * acc_sc[c] + jnp.dot(p.astype(BF16), v, preferred_element_type=F32)
        m_sc[c] = m_new

    @pl.when(ki == nk - 1)
    def _fin():
        lq = lamqk_ref[...]
        lam = (jnp.exp(jnp.sum(lq[0:1] * lq[1:2], axis=1, keepdims=True))
               - jnp.exp(jnp.sum(lq[2:3] * lq[3:4], axis=1, keepdims=True)) + lam_init)
        o = acc_sc[0] / l_sc[0] - lam * (acc_sc[1] / l_sc[1])
        o_ref[...] = (_rms(o, subg_ref[...]) * (1.0 - lam_init)).astype(o_ref.dtype)


def _attn_bounded_kernel(lamqk_ref, subg_ref, kmax_ref, q_ref, k_ref, v_ref, o_ref,
                         qx_sc, kx_sc, l_sc, acc_sc, *, lam_init, nk):
    ki = pl.program_id(3)
    tq = q_ref.shape[0]
    tk = k_ref.shape[0]

    @pl.when(ki == 0)
    def _init():
        q = q_ref[...]
        lane = lax.broadcasted_iota(jnp.int32, (tq, LANES), 1)
        qf = q.astype(F32)
        n2 = jnp.dot((qf * qf).astype(BF16), _map_segments(), preferred_element_type=F32)
        bound = jnp.sqrt(n2) * kmax_ref[...] * BOUND_SLACK
        zero = jnp.zeros_like(q)
        qx_sc[0, :, 0:LANES] = jnp.where(lane < HEAD_DIM, q, zero)
        qx_sc[0, :, LANES:2 * LANES] = jnp.where(lane == 0, -bound, 0.0).astype(BF16)
        qx_sc[1, :, 0:LANES] = jnp.where(lane >= HEAD_DIM, q, zero)
        qx_sc[1, :, LANES:2 * LANES] = jnp.where(lane == 0, -pltpu.roll(bound, HEAD_DIM, 1), 0.0).astype(BF16)
        lane_k = lax.broadcasted_iota(jnp.int32, (tk, LANES), 1)
        kx_sc[:, LANES:2 * LANES] = jnp.where(lane_k == 0, 1.0, 0.0).astype(BF16)
        l_sc[...] = jnp.zeros(l_sc.shape, F32)
        acc_sc[...] = jnp.zeros(acc_sc.shape, F32)

    kx_sc[:, 0:LANES] = k_ref[...]
    kx = kx_sc[...]
    v = v_ref[...]
    for c in range(2):
        s = lax.dot_general(qx_sc[c], kx, NT_DIMS, preferred_element_type=F32)
        p = jnp.exp2(s)
        part = p[:, 0:LANES]
        for j in range(1, tk // LANES):
            part = part + p[:, j * LANES:(j + 1) * LANES]
        l_sc[c] += part
        acc_sc[c] += jnp.dot(p.astype(BF16), v, preferred_element_type=F32)

    @pl.when(ki == nk - 1)
    def _fin():
        lq = lamqk_ref[...]
        lam = (jnp.exp(jnp.sum(lq[0:1] * lq[1:2], axis=1, keepdims=True))
               - jnp.exp(jnp.sum(lq[2:3] * lq[3:4], axis=1, keepdims=True)) + lam_init)
        l0 = jnp.sum(l_sc[0], axis=1, keepdims=True)
        l1 = jnp.sum(l_sc[1], axis=1, keepdims=True)
        o = acc_sc[0] / l0 - lam * (acc_sc[1] / l1)
        o_ref[...] = (_rms(o, subg_ref[...]) * (1.0 - lam_init)).astype(o_ref.dtype)


def _drop_input(kern, idx, *refs):
    return kern(*refs[:idx], *refs[idx + 1:])


def _attention(layer, q, k, v, kmax, lam_qk, subln_g, lam_init, *, bounded, tq, tk, q0, nq, k0, nk,
               o_prev=None):
    b, n, _ = q.shape
    q_idx = lambda bi, h, qi, ki: (bi, qi + q0, h)
    kv_idx = lambda bi, h, qi, ki: (bi, ki + k0, h)
    in_specs = [pl.BlockSpec((None, 4, HEAD_DIM), lambda *_: (layer, 0, 0)),
                pl.BlockSpec((None, 1, V_DIM), lambda *_: (layer, 0, 0)),
                pl.BlockSpec((None, None, 1, LANES), lambda bi, h, qi, ki: (bi, h, 0, 0)),
                pl.BlockSpec((None, tq, LANES), q_idx),
                pl.BlockSpec((None, tk, LANES), kv_idx),
                pl.BlockSpec((None, tk, LANES), kv_idx)]
    args = [lam_qk, subln_g, kmax, q, k, v]
    if bounded:
        kern = functools.partial(_attn_bounded_kernel, lam_init=lam_init, nk=nk)
        scratch = [pltpu.VMEM((2, tq, 2 * LANES), BF16),
                   pltpu.VMEM((tk, 2 * LANES), BF16),
                   pltpu.VMEM((2, tq, LANES), F32),
                   pltpu.VMEM((2, tq, V_DIM), F32)]
    else:
        kern = functools.partial(_drop_input, functools.partial(_attn_kernel, lam_init=lam_init, nk=nk), 2)
        scratch = [pltpu.VMEM((2, tq, LANES), BF16),
                   pltpu.VMEM((2, tq, 1), F32),
                   pltpu.VMEM((2, tq, 1), F32),
                   pltpu.VMEM((2, tq, V_DIM), F32)]
    aliases = {}
    if o_prev is not None:
        in_specs.append(pl.BlockSpec(memory_space=pl.ANY))
        args.append(o_prev)
        aliases = {6: 0}
        kern = functools.partial(_drop_input, kern, 6)
    return pl.pallas_call(
        kern,
        grid=(b, N_HEADS, nq, nk),
        in_specs=in_specs,
        out_specs=pl.BlockSpec((None, tq, LANES), q_idx),
        out_shape=jax.ShapeDtypeStruct((b, n, ATTN_W), BF16),
        scratch_shapes=scratch,
        input_output_aliases=aliases,
        compiler_params=_cparams(("parallel", "parallel", "parallel", "arbitrary")),
        name="diff_attn_bounded" if bounded else "diff_attn",
    )(*args)


def _attn_pair_kernel(lamqk_ref, subg_ref, kmax_ref, q_ref, k_ref, v_ref, o_ref, *, lam_init, tk):
    tq = q_ref.shape[0]
    nkeys = k_ref.shape[0]
    lane = lax.broadcasted_iota(jnp.int32, (tq, LANES), 1)
    lane_k = lax.broadcasted_iota(jnp.int32, (tk, LANES), 1)
    one_tile = jnp.where(lane_k == 0, 1.0, 0.0).astype(BF16)
    lq = lamqk_ref[...]
    lam = (jnp.exp(jnp.sum(lq[0:1] * lq[1:2], axis=1, keepdims=True))
           - jnp.exp(jnp.sum(lq[2:3] * lq[3:4], axis=1, keepdims=True)) + lam_init)
    for hh in range(2):
        cols = slice(hh * LANES, (hh + 1) * LANES)
        q = q_ref[:, cols]
        qf = q.astype(F32)
        n2 = jnp.dot((qf * qf).astype(BF16), _map_segments(), preferred_element_type=F32)
        bound = jnp.sqrt(n2) * kmax_ref[hh] * BOUND_SLACK
        zero = jnp.zeros_like(q)
        qx = [jnp.concatenate([jnp.where(lane < HEAD_DIM, q, zero),
                               jnp.where(lane == 0, -bound, 0.0).astype(BF16)], axis=1),
              jnp.concatenate([jnp.where(lane >= HEAD_DIM, q, zero),
                               jnp.where(lane == 0, -pltpu.roll(bound, HEAD_DIM, 1), 0.0).astype(BF16)],
                              axis=1)]
        l = [jnp.zeros((tq, LANES), F32)] * 2
        acc = [jnp.zeros((tq, V_DIM), F32)] * 2
        for kb in range(nkeys // tk):
            rows = slice(kb * tk, (kb + 1) * tk)
            kx = jnp.concatenate([k_ref[rows, cols], one_tile], axis=1)
            v = v_ref[rows, cols]
            for c in range(2):
                s = lax.dot_general(qx[c], kx, NT_DIMS, preferred_element_type=F32)
                p = jnp.exp2(s)
                part = p[:, 0:LANES]
                for j in range(1, tk // LANES):
                    part = part + p[:, j * LANES:(j + 1) * LANES]
                l[c] = l[c] + part
                acc[c] = acc[c] + jnp.dot(p.astype(BF16), v, preferred_element_type=F32)
        l0 = jnp.sum(l[0], axis=1, keepdims=True)
        l1 = jnp.sum(l[1], axis=1, keepdims=True)
        o = acc[0] / l0 - lam * (acc[1] / l1)
        o_ref[:, cols] = (_rms(o, subg_ref[...]) * (1.0 - lam_init)).astype(o_ref.dtype)


def _attention_pairs(layer, q, k, v, kmax, lam_qk, subln_g, lam_init, *, tq, tk, nq, nkeys):
    b, n, _ = q.shape
    q_idx = lambda bi, hp, qi: (bi, qi, hp)
    kv_idx = lambda bi, hp, qi: (bi, 0, hp)
    return pl.pallas_call(
        functools.partial(_attn_pair_kernel, lam_init=lam_init, tk=tk),
        grid=(b, N_HEADS // 2, nq),
        in_specs=[pl.BlockSpec((None, 4, HEAD_DIM), lambda *_: (layer, 0, 0)),
                  pl.BlockSpec((None, 1, V_DIM), lambda *_: (layer, 0, 0)),
                  pl.BlockSpec((None, 2, 1, LANES), lambda bi, hp, qi: (bi, hp, 0, 0)),
                  pl.BlockSpec((None, tq, 2 * LANES), q_idx),
                  pl.BlockSpec((None, nkeys, 2 * LANES), kv_idx, pipeline_mode=pl.Buffered(1)),
                  pl.BlockSpec((None, nkeys, 2 * LANES), kv_idx, pipeline_mode=pl.Buffered(1))],
        out_specs=pl.BlockSpec((None, tq, 2 * LANES), q_idx),
        out_shape=jax.ShapeDtypeStruct((b, n, ATTN_W), BF16),
        compiler_params=_cparams(("parallel", "parallel", "arbitrary")),
        name="diff_attn_pairs",
    )(lam_qk, subln_g, kmax, q, k, v)


def _attention_layer(layer, q, k, v, nrm, lam_qk, subln_g, lam_init, seq, with_ctx):
    b = q.shape[0]
    nkeys = seq + CTX_LEN
    top = jnp.sqrt(jnp.max(nrm[:, :, 0:2, 0:2 * N_HEADS], axis=1))
    qmax, kmax = top[:, :, 0:N_HEADS], top[:, :, N_HEADS:]
    small = jnp.max(qmax * kmax) * BOUND_SLACK < BOUND_LIMIT
    kmax4 = jnp.repeat(jnp.swapaxes(kmax, 1, 2), HEAD_DIM, axis=-1).reshape(b, N_HEADS, 1, LANES)

    def run(bounded):
        if bounded:
            o_n = _attention_pairs(layer, q, k, v, kmax4, lam_qk, subln_g, lam_init,
                                   tq=TQ, tk=TK, nq=seq // TQ, nkeys=nkeys)
        else:
            o_n = _attention(layer, q, k, v, kmax4, lam_qk, subln_g, lam_init, bounded=False,
                             tq=TQ, tk=TK, q0=0, nq=seq // TQ, k0=0, nk=nkeys // TK)
        if with_ctx:
            o_n = _attention(layer, q, k, v, kmax4, lam_qk, subln_g, lam_init, bounded=bounded,
                             tq=TM, tk=CTX_LEN, q0=seq // TM, nq=1, k0=seq // CTX_LEN, nk=1,
                             o_prev=o_n)
        return o_n

    return lax.cond(small, lambda: run(True), lambda: run(False))


def _cmul(ar, ai, br, bi):
    return ar * br - ai * bi, ar * bi + ai * br


def _cpow_by_bits(lr, li, expo, nbits):
    pr = jnp.ones(expo.shape, F32)
    pi = jnp.zeros(expo.shape, F32)
    br, bi = lr, li
    for bit in range(nbits):
        on = ((expo >> bit) & 1) == 1
        fr = jnp.where(on, br, 1.0)
        fi = jnp.where(on, bi, 0.0)
        pr, pi = _cmul(pr, pi, fr, fi)
        if bit + 1 < nbits:
            br, bi = _cmul(br, bi, br, bi)
    return pr, pi


def _discretize(ar, ai, log_dt):
    dt = jnp.exp(log_dt)
    er = jnp.exp(ar * dt)
    lbr = er * jnp.cos(ai * dt)
    lbi = er * jnp.sin(ai * dt)
    den = ar * ar + ai * ai
    nr = lbr - 1.0
    cr = (nr * ar + lbi * ai) / den
    ci = (lbi * ar - nr * ai) / den
    return lbr, lbi, cr, ci


def _s5_kernel(x_ref, ar_ref, ai_ref, ldt_ref, btr_ref, bti_ref, ctr_ref, cti_ref,
               y_ref, sext, toep, csr_sc, csi_sc, zre, zim, hre, him, *, nrows, nchunk, nbatch, nctx):
    nlat = nchunk - nctx
    blk = lax.broadcasted_iota(jnp.int32, (S5_TC, PAIR), 0) // SSM_GROUP
    lane_g = lax.broadcasted_iota(jnp.int32, (1, PAIR), 1) // SSM_STATE

    for d in range(2):
        fwd = d == 0
        step_r, step_i = [], []
        for pi in range(S5_NP):
            lbr, lbi, cr, ci = _discretize(ar_ref[d, pi], ai_ref[d, pi], ldt_ref[d, pi])
            bbr, bbi = _cmul(cr, ci, btr_ref[d, pi], bti_ref[d, pi])
            bbr_t = jnp.concatenate([bbr] * S5_T, axis=0)
            bbi_t = jnp.concatenate([bbi] * S5_T, axis=0)
            ctr_t = jnp.concatenate([ctr_ref[d, pi]] * S5_T, axis=0)
            cti_t = jnp.concatenate([cti_ref[d, pi]] * S5_T, axis=0)
            upr, upi = _cpow_by_bits(lbr, lbi, blk, 4)
            dnr, dni = _cpow_by_bits(lbr, lbi, S5_T - 1 - blk, 4)
            (per, pei), (pwr, pwi) = ((upr, upi), (dnr, dni)) if fwd else ((dnr, dni), (upr, upi))
            bsr, bsi = _cmul(pwr, pwi, bbr_t, bbi_t)
            wcr, wci = _cmul(per, pei, ctr_t, cti_t)
            c1r, c1i = _cmul(wcr, wci, lbr, lbi)
            tr, ti = lbr, lbi
            for _ in range(int(math.log2(S5_T))):
                tr, ti = _cmul(tr, ti, tr, ti)
            step_r.append(tr)
            step_i.append(ti)

            z_r = z_i = None
            for gg in range(2):
                g = 2 * pi + gg
                lm = lane_g == gg
                strip = (lax.dot_general(jnp.where(lm, bbr, 0.0), wcr, NT_DIMS, preferred_element_type=F32,
                                         precision=lax.Precision.HIGHEST)
                         - lax.dot_general(jnp.where(lm, bbi, 0.0), wci, NT_DIMS, preferred_element_type=F32,
                                           precision=lax.Precision.HIGHEST))
                if fwd:
                    sext[:, 0:S5_TC] = jnp.zeros((SSM_GROUP, S5_TC), F32)
                    sext[:, S5_TC:2 * S5_TC] = strip
                    for j in range(S5_T):
                        lo = S5_TC - SSM_GROUP * j
                        toep[j * SSM_GROUP:(j + 1) * SSM_GROUP, :] = sext[:, lo:lo + S5_TC].astype(BF16)
                else:
                    sext[:, 0:S5_TC] = strip
                    sext[:, S5_TC:2 * S5_TC] = jnp.zeros((SSM_GROUP, S5_TC), F32)
                    for j in range(S5_T):
                        lo = SSM_GROUP * (S5_T - 1 - j)
                        toep[j * SSM_GROUP:(j + 1) * SSM_GROUP, :] = sext[:, lo:lo + S5_TC].astype(BF16)
                x = x_ref[g]
                yi = jnp.dot(x, toep[...], preferred_element_type=F32)
                if fwd:
                    y_ref[g] = yi
                else:
                    y_ref[g] += yi
                zr = jnp.dot(x, jnp.where(lm, bsr, 0.0).astype(BF16), preferred_element_type=F32)
                zi = jnp.dot(x, jnp.where(lm, bsi, 0.0).astype(BF16), preferred_element_type=F32)
                z_r = zr if z_r is None else z_r + zr
                z_i = zi if z_i is None else z_i + zi
                csr_sc[g] = jnp.where(lm, c1r, 0.0).astype(BF16)
                csi_sc[g] = jnp.where(lm, c1i, 0.0).astype(BF16)
            for b in range(nbatch):
                s = pi * nbatch + b
                zre[pl.ds(s, nrows, stride=SUBLANES), :] = z_r[b * nrows:(b + 1) * nrows]
                zim[pl.ds(s, nrows, stride=SUBLANES), :] = z_i[b * nrows:(b + 1) * nrows]

        sub = lax.broadcasted_iota(jnp.int32, (SUBLANES, PAIR), 0) // nbatch
        tr = jnp.zeros((SUBLANES, PAIR), F32)
        ti = jnp.zeros((SUBLANES, PAIR), F32)
        for pi in range(S5_NP):
            tr = jnp.where(sub == pi, step_r[pi], tr)
            ti = jnp.where(sub == pi, step_i[pi], ti)
        hre[...] = jnp.zeros(hre.shape, F32)
        him[...] = jnp.zeros(him.shape, F32)

        def body(n, carry):
            if fwd:
                ch = jnp.where(n < nctx, nlat + n, n - nctx)
            else:
                ch = nchunk - 1 - n
            h_r, h_i = carry
            r = pl.multiple_of(ch * SUBLANES, SUBLANES)
            hre[pl.ds(r, SUBLANES), :] = h_r
            him[pl.ds(r, SUBLANES), :] = h_i
            n_r, n_i = _cmul(tr, ti, h_r, h_i)
            return n_r + zre[pl.ds(r, SUBLANES), :], n_i + zim[pl.ds(r, SUBLANES), :]

        zero = jnp.zeros((SUBLANES, PAIR), F32)
        lax.fori_loop(0, nchunk, body, (zero, zero), unroll=8)

        for pi in range(S5_NP):
            for b in range(nbatch):
                s = pi * nbatch + b
                h_r = hre[pl.ds(s, nrows, stride=SUBLANES), :].astype(BF16)
                h_i = him[pl.ds(s, nrows, stride=SUBLANES), :].astype(BF16)
                for gg in range(2):
                    g = 2 * pi + gg
                    y_ref[g, b * nrows:(b + 1) * nrows, :] += (
                        lax.dot_general(h_r, csr_sc[g], NT_DIMS, preferred_element_type=F32)
                        - lax.dot_general(h_i, csi_sc[g], NT_DIMS, preferred_element_type=F32))


def _s5_params(a_re, a_im, log_dt, b_re, b_im, c_re, c_im):
    depth = a_re.shape[0]
    gp = SSM_GROUPS // 2
    row = lambda a: a.reshape(depth, 2, gp, 1, PAIR)
    ldt = jnp.broadcast_to(log_dt[..., None], a_re.shape)
    bt = lambda a: jnp.transpose(a.reshape(depth, 2, gp, 2, SSM_STATE, SSM_GROUP),
                                 (0, 1, 2, 5, 3, 4)).reshape(depth, 2, gp, SSM_GROUP, PAIR)
    ct = lambda a: jnp.transpose(a.reshape(depth, 2, gp, 2, SSM_GROUP, SSM_STATE),
                                 (0, 1, 2, 4, 3, 5)).reshape(depth, 2, gp, SSM_GROUP, PAIR)
    return row(a_re), row(a_im), row(ldt), bt(b_re), bt(b_im), ct(c_re), ct(c_im)


def _s5(layer, xg, params, nbatch, nvalid):
    g, nr, _ = xg.shape
    gstep = 2 * S5_NP
    assert S5_NP * nbatch == SUBLANES
    vec = pl.BlockSpec((None, 2, S5_NP, 1, PAIR), lambda i: (layer, 0, i, 0, 0))
    mat = pl.BlockSpec((None, 2, S5_NP, SSM_GROUP, PAIR), lambda i: (layer, 0, i, 0, 0))
    kern = functools.partial(_s5_kernel, nrows=nr // nbatch, nchunk=nvalid // S5_T, nbatch=nbatch,
                             nctx=CTX_LEN // S5_T)
    return pl.pallas_call(
        kern,
        grid=(g // gstep,),
        in_specs=[pl.BlockSpec((gstep, nr, S5_TC), lambda i: (i, 0, 0)),
                  vec, vec, vec, mat, mat, mat, mat],
        out_specs=pl.BlockSpec((gstep, nr, S5_TC), lambda i: (i, 0, 0)),
        out_shape=jax.ShapeDtypeStruct((g, nr, S5_TC), F32),
        scratch_shapes=[pltpu.VMEM((SSM_GROUP, 2 * S5_TC), F32),
                        pltpu.VMEM((S5_TC, S5_TC), BF16),
                        pltpu.VMEM((gstep, S5_TC, PAIR), BF16),
                        pltpu.VMEM((gstep, S5_TC, PAIR), BF16),
                        pltpu.VMEM((nr // nbatch * SUBLANES, PAIR), F32),
                        pltpu.VMEM((nr // nbatch * SUBLANES, PAIR), F32),
                        pltpu.VMEM((nr // nbatch * SUBLANES, PAIR), F32),
                        pltpu.VMEM((nr // nbatch * SUBLANES, PAIR), F32)],
        compiler_params=_cparams(("parallel",)),
        name="s5_scan",
    )(xg, *params)


def _tail_kernel(xl_ref, xc_ref, on_ref, yg_ref, u_ref, gate_ref, mod_ref, d_ref, wglu_ref, bglu_ref,
                 wbs_ref, wba_ref, wout_ref, ng_ref, w1_ref, b1_ref, w2_ref, b2_ref,
                 o_ref, ysc, *, n_lat_tiles):
    ys = _chunks_to_rows(yg_ref, ysc) + u_ref[...] * d_ref[...]
    gl = jax.nn.gelu(ys)
    z = gl * jax.nn.sigmoid(jnp.dot(gl.astype(BF16), wglu_ref[...], preferred_element_type=F32)
                            + bglu_ref[...])
    ps = jnp.dot(z.astype(BF16), wbs_ref[...], preferred_element_type=F32)
    pa = jnp.dot(on_ref[...], wba_ref[...], preferred_element_type=F32)
    gate = gate_ref[...].astype(F32)
    mix = gate[:, 0:D_MODEL] * pa + gate[:, D_MODEL:2 * D_MODEL] * ps
    m2 = jnp.dot(mix.astype(BF16), wout_ref[...], preferred_element_type=F32)
    x1 = _stream_tile(xl_ref, xc_ref, n_lat_tiles) + mod_ref[:, 2 * D_MODEL:3 * D_MODEL] * _rms(m2, ng_ref[1:2])
    h = _rms(x1, ng_ref[2:3]) * (1.0 + mod_ref[:, 4 * D_MODEL:5 * D_MODEL]) \
        + mod_ref[:, 3 * D_MODEL:4 * D_MODEL]
    f = jnp.dot(h.astype(BF16), w1_ref[...], preferred_element_type=F32) + b1_ref[...]
    f = jnp.square(jnp.maximum(f, 0.0))
    o = jnp.dot(f.astype(BF16), w2_ref[...], preferred_element_type=F32) + b2_ref[...]
    o_ref[...] = x1 + mod_ref[:, 5 * D_MODEL:6 * D_MODEL] * _rms(o, ng_ref[3:4])


def _tail(layer, xl, xc, o_n, yg, u, gates, modsel, ssm_d, w_glu, b_glu, w_br_s, w_br_a, w_out, norm_g,
          w1, b1, w2, b2, n_lat_tiles, nt):
    b = xl.shape[0]
    nt_all = n_lat_tiles + 1
    row = lambda bi, i: (bi, i, 0)
    return pl.pallas_call(
        functools.partial(_tail_kernel, n_lat_tiles=n_lat_tiles),
        grid=(b, nt),
        in_specs=_stream_specs(n_lat_tiles) + [
            pl.BlockSpec((None, TM, ATTN_W), row),
            pl.BlockSpec((SSM_GROUPS, TM // S5_T, S5_TC), lambda bi, i: (0, bi * nt_all + i, 0)),
            pl.BlockSpec((None, TM, SSM_W), row),
            pl.BlockSpec((None, TM, 2 * D_MODEL), row),
            _mod_spec(layer, n_lat_tiles),
            _layer_spec(layer, (1, SSM_W)),
            _layer_spec(layer, (SSM_W, SSM_W)),
            _layer_spec(layer, (1, SSM_W)),
            _layer_spec(layer, (SSM_W, D_MODEL)),
            _layer_spec(layer, (ATTN_W, D_MODEL)),
            _layer_spec(layer, (D_MODEL, D_MODEL)),
            _layer_spec(layer, (4, D_MODEL)),
            _layer_spec(layer, (D_MODEL, D_FF)),
            _layer_spec(layer, (1, D_FF)),
            _layer_spec(layer, (D_FF, D_MODEL)),
            _layer_spec(layer, (1, D_MODEL))],
        out_specs=pl.BlockSpec((None, TM, D_MODEL), row),
        out_shape=jax.ShapeDtypeStruct((b, nt * TM, D_MODEL), F32),
        scratch_shapes=[pltpu.VMEM((S5_SLABS, TM, LANES), F32)],
        compiler_params=_cparams(("parallel", "arbitrary")),
        name="merge_mlp",
    )(xl, xc, o_n, yg, u, gates, modsel, ssm_d, w_glu, b_glu, w_br_s, w_br_a, w_out, norm_g,
      w1, b1, w2, b2)


def _rope_tables(n_tokens):
    rows = n_tokens // GRID_W
    row = jnp.repeat(jnp.arange(rows, dtype=jnp.int32), GRID_W).astype(F32)
    col = jnp.tile(jnp.arange(GRID_W, dtype=jnp.int32), rows).astype(F32)
    inv_freq = ROPE_BASE ** (-jnp.arange(N_FREQ, dtype=F32) / N_FREQ)
    ang = jnp.stack([row[:, None] * inv_freq, col[:, None] * inv_freq], axis=1)
    cos, sin = jnp.cos(ang), jnp.sin(ang)
    cos_l = jnp.tile(jnp.concatenate([cos[:, 0], cos[:, 0], cos[:, 1], cos[:, 1]], axis=1), (1, 2))
    sin_l = jnp.tile(jnp.concatenate([-sin[:, 0], sin[:, 0], -sin[:, 1], sin[:, 1]], axis=1), (1, 2))
    cos_t = jnp.concatenate([cos_l, jnp.ones((TM, LANES), F32)], axis=0)
    sin_t = jnp.concatenate([sin_l, jnp.zeros((TM, LANES), F32)], axis=0)
    return cos_t, sin_t


def kernel(x, c, ctx, c_ctx, ada_w, ada_b, norm_g, w_in, gate_b, lam_qk, subln_g, w_br_a,
           ssm_a_re, ssm_a_im, ssm_b_re, ssm_b_im, ssm_c_re, ssm_c_im, ssm_log_dt, ssm_d,
           w_glu, b_glu, w_br_s, w_out, w_mlp1, b_mlp1, w_mlp2, b_mlp2):
    b, seq, _ = x.shape
    nvalid = seq + CTX_LEN
    assert b + 1 <= 8 and ctx.shape[1] == CTX_LEN and seq % TM == 0 and seq % TQ == 0 and nvalid % TK == 0
    n_lat_tiles = seq // TM
    cos_t, sin_t = _rope_tables(seq)
    xl, xc = x, jnp.concatenate([ctx, jnp.zeros((b, ROWS_PAD, D_MODEL), F32)], axis=1)
    cc = jnp.zeros((8, D_MODEL), F32).at[:b].set(c).at[b].set(c_ctx)
    mod = _modulation(cc, ada_w, ada_b)
    modsel = jnp.stack([mod[:, :b], jnp.broadcast_to(mod[:, b:b + 1], (DEPTH, b, 6 * D_MODEL))], axis=2)
    modsel = modsel.reshape(DEPTH, b, 2, 1, 6 * D_MODEL)
    s5_params = _s5_params(ssm_a_re, ssm_a_im, ssm_log_dt, ssm_b_re, ssm_b_im, ssm_c_re, ssm_c_im)
    row3 = lambda a: a.reshape(DEPTH, 1, -1)
    w_in_b, w_glu_b, w_br_s_b, w_br_a_b, w_out_b, w1_b, w2_b = (
        w.astype(BF16) for w in (w_in, w_glu, w_br_s, w_br_a, w_out, w_mlp1, w_mlp2))
    for i in range(DEPTH):
        last = i == DEPTH - 1
        lam_init = 0.8 - 0.6 * math.exp(-0.3 * i)
        q, k, v, u, gates, xg, nrm = _in_proj(i, xl, xc, modsel, norm_g, w_in_b, row3(gate_b),
                                              cos_t, sin_t, n_lat_tiles)
        o_n = _attention_layer(i, q, k, v, nrm, lam_qk, row3(subln_g), lam_init, seq, with_ctx=not last)
        yg = _s5(i, xg, s5_params, b, nvalid)
        xl = _tail(i, xl, xc, o_n, yg, u, gates, modsel, row3(ssm_d), w_glu_b, row3(b_glu), w_br_s_b,
                   w_br_a_b, w_out_b, norm_g, w1_b, row3(b_mlp1), w2_b, row3(b_mlp2),
                   n_lat_tiles, n_lat_tiles if last else n_lat_tiles + 1)
        if not last:
            xc = xl[:, seq:]
    return xl
```

```python
import functools
import math

import jax
import jax.numpy as jnp
from jax import lax
from jax.experimental import pallas as pl
from jax.experimental.pallas import tpu as pltpu

F32 = jnp.float32
BF16 = jnp.bfloat16

D_MODEL = 1024
DEPTH = 2
GRID_W = 64
CTX_LEN = 256
N_HEADS = 4
HEAD_DIM = 64
V_DIM = 128
QK_W = N_HEADS * 2 * HEAD_DIM
ATTN_W = 512
SSM_W = 512
SSM_GROUP = 16
SSM_GROUPS = 32
SSM_STATE = 64
D_FF = 4096
IN_COLS = 4096
N_FREQ = 16
ROPE_BASE = 10000.0
EPS = 1e-6
LOG2E = 1.4426950408889634

LANES = 128
SUBLANES = 8
TM = 512
ROWS_PAD = TM - CTX_LEN
GATHER_ROWS = 16
TQ = 1024
TK = 2816
S5_T = 16
S5_TC = S5_T * SSM_GROUP
S5_SLABS = SSM_W // LANES
S5_GPS = LANES // SSM_GROUP
PAIR = 2 * SSM_STATE
S5_NP = 4
FF_CHUNK = 1024
VMEM_LIMIT = 56 * 1024 * 1024

NEG_BIG = -1e30
BOUND_SLACK = 1.03
BOUND_LIMIT = 50.0

NT_DIMS = (((1,), (1,)), ((), ()))


def _cparams(sem):
    return pltpu.CompilerParams(dimension_semantics=sem, vmem_limit_bytes=VMEM_LIMIT)


def _layer_spec(layer, shape):
    nd = len(shape)
    return pl.BlockSpec((None,) + shape, lambda *_: (layer,) + (0,) * nd, pipeline_mode=pl.Buffered(1))


def _mod_spec(layer, ctx_tile):
    return pl.BlockSpec((None, None, None, 1, 6 * D_MODEL),
                        lambda bi, i: (layer, bi, i // ctx_tile, 0, 0))


def _stream_specs(n_lat_tiles):
    return [pl.BlockSpec((None, TM, D_MODEL), lambda bi, i: (bi, jnp.minimum(i, n_lat_tiles - 1), 0)),
            pl.BlockSpec((None, TM, D_MODEL), lambda bi, i: (bi, 0, 0))]


def _stream_tile(xl_ref, xc_ref, n_lat_tiles):
    return jnp.where(pl.program_id(1) >= n_lat_tiles, xc_ref[...], xl_ref[...])


def _rms(x, g):
    ms = jnp.mean(x * x, axis=-1, keepdims=True)
    return x * lax.rsqrt(ms + EPS) * g


def _sigmoid(x):
    return 0.5 * jnp.tanh(0.5 * x) + 0.5


def _mod_kernel(c_ref, w_ref, b_ref, o_ref):
    c = c_ref[...]
    s = c * jax.nn.sigmoid(c)
    o_ref[...] = jnp.dot(s.astype(BF16), w_ref[...].astype(BF16),
                         preferred_element_type=F32) + b_ref[...]


def _modulation(cc, ada_w, ada_b):
    depth, _, n = ada_w.shape
    bn = 1024
    return pl.pallas_call(
        _mod_kernel,
        grid=(depth, n // bn),
        in_specs=[pl.BlockSpec((8, D_MODEL), lambda l, j: (0, 0)),
                  pl.BlockSpec((None, D_MODEL, bn), lambda l, j: (l, 0, j)),
                  pl.BlockSpec((None, 1, bn), lambda l, j: (l, 0, j))],
        out_specs=pl.BlockSpec((None, 8, bn), lambda l, j: (l, 0, j)),
        out_shape=jax.ShapeDtypeStruct((depth, 8, n), F32),
        compiler_params=_cparams(("arbitrary", "arbitrary")),
        name="adaln_mod",
    )(cc, ada_w, ada_b.reshape(depth, 1, n))


def _lane_window(off):
    lane = lax.broadcasted_iota(jnp.int32, (GATHER_ROWS, LANES), 1)
    return (lane >= off) & (lane < off + SSM_GROUP)


def _rows_to_chunks(u, usc, xg_ref):
    for sl in range(S5_SLABS):
        usc[sl] = u[:, sl * LANES:(sl + 1) * LANES]
    for part in range(TM // (S5_T * GATHER_ROWS)):
        r0 = part * GATHER_ROWS
        for sl in range(S5_SLABS):
            acc = [[None] * (S5_TC // LANES) for _ in range(S5_GPS)]
            for t in range(S5_T):
                v = usc[sl, pl.ds(r0 * S5_T + t, GATHER_ROWS, stride=S5_T), :]
                lt, off = divmod(t * SSM_GROUP, LANES)
                win = _lane_window(off)
                for gl in range(S5_GPS):
                    shift = (off - gl * SSM_GROUP) % LANES
                    moved = pltpu.roll(v, shift, 1) if shift else v
                    prev = acc[gl][lt]
                    acc[gl][lt] = jnp.where(win, moved, 0.0 if prev is None else prev)
            for gl in range(S5_GPS):
                for lt in range(S5_TC // LANES):
                    xg_ref[sl * S5_GPS + gl, r0:r0 + GATHER_ROWS, lt * LANES:(lt + 1) * LANES] = (
                        acc[gl][lt].astype(xg_ref.dtype))


def _chunks_to_rows(yg_ref, ysc):
    for part in range(TM // (S5_T * GATHER_ROWS)):
        r0 = part * GATHER_ROWS
        for sl in range(S5_SLABS):
            tiles = [[yg_ref[sl * S5_GPS + gl, r0:r0 + GATHER_ROWS, lt * LANES:(lt + 1) * LANES]
                      for lt in range(S5_TC // LANES)] for gl in range(S5_GPS)]
            for t in range(S5_T):
                lt, off = divmod(t * SSM_GROUP, LANES)
                v = None
                for gl in range(S5_GPS):
                    shift = (gl * SSM_GROUP - off) % LANES
                    src = tiles[gl][lt]
                    moved = pltpu.roll(src, shift, 1) if shift else src
                    v = jnp.where(_lane_window(gl * SSM_GROUP), moved, 0.0 if v is None else v)
                ysc[sl, pl.ds(r0 * S5_T + t, GATHER_ROWS, stride=S5_T), :] = v
    return jnp.concatenate([ysc[sl] for sl in range(S5_SLABS)], axis=1)


def _map_segments():
    li = lax.broadcasted_iota(jnp.int32, (LANES, LANES), 0) // HEAD_DIM
    lj = lax.broadcasted_iota(jnp.int32, (LANES, LANES), 1) // HEAD_DIM
    return (li == lj).astype(BF16)


def _in_kernel(xl_ref, xc_ref, mod_ref, g_ref, w_ref, gb_ref, cos_ref, sin_ref,
               q_ref, k_ref, v_ref, u_ref, gate_ref, xg_ref, nrm_ref, usc, *, n_lat_tiles):
    x = _stream_tile(xl_ref, xc_ref, n_lat_tiles)
    h = _rms(x, g_ref[0:1]) * (1.0 + mod_ref[:, D_MODEL:2 * D_MODEL]) + mod_ref[:, 0:D_MODEL]
    hb = h.astype(BF16)
    cos = cos_ref[...]
    sin = sin_ref[...]
    lane = lax.broadcasted_iota(jnp.int32, (TM, LANES), 1)
    first_half = (lane & 31) < 16
    sel = (lax.broadcasted_iota(jnp.int32, (8, LANES), 0)
           == lax.broadcasted_iota(jnp.int32, (8, LANES), 1) // HEAD_DIM).astype(BF16)
    slot = lax.broadcasted_iota(jnp.int32, (8, LANES), 1)
    nrm = jnp.zeros((8, LANES), F32)

    def rope_store(col0, out_ref, scale, slot0, nrm):
        t = jnp.dot(hb, w_ref[:, col0:col0 + QK_W], preferred_element_type=F32)
        for j in range(QK_W // LANES):
            tj = t[:, j * LANES:(j + 1) * LANES]
            partner = jnp.where(first_half, pltpu.roll(tj, LANES - 16, 1), pltpu.roll(tj, 16, 1))
            r = tj * cos + partner * sin
            if scale != 1.0:
                r = r * scale
            out_ref[:, j * LANES:(j + 1) * LANES] = r.astype(out_ref.dtype)
            n2 = lax.dot_general(sel, (r * r).astype(BF16), NT_DIMS, preferred_element_type=F32)
            nrm = jnp.where(slot == slot0 + j, jnp.max(n2, axis=1, keepdims=True), nrm)
        return nrm

    nrm = rope_store(0, q_ref, HEAD_DIM ** -0.5 * LOG2E, 0, nrm)
    nrm = rope_store(QK_W, k_ref, 1.0, N_HEADS, nrm)
    nrm_ref[...] = nrm
    v_ref[...] = jnp.dot(hb, w_ref[:, 1024:1536], preferred_element_type=F32).astype(v_ref.dtype)
    u = jnp.dot(hb, w_ref[:, 1536:2048], preferred_element_type=F32)
    u_ref[...] = u
    _rows_to_chunks(u, usc, xg_ref)
    g = jnp.dot(hb, w_ref[:, 2048:IN_COLS], preferred_element_type=F32) + gb_ref[...]
    gate_ref[...] = _sigmoid(g).astype(gate_ref.dtype)


def _in_proj(layer, xl, xc, modsel, norm_g, w_in_b, gate_b, cos_t, sin_t, n_lat_tiles):
    b = xl.shape[0]
    nt = n_lat_tiles + 1
    n = nt * TM
    row = lambda bi, i: (bi, i, 0)
    return pl.pallas_call(
        functools.partial(_in_kernel, n_lat_tiles=n_lat_tiles),
        grid=(b, nt),
        in_specs=_stream_specs(n_lat_tiles) + [
            _mod_spec(layer, n_lat_tiles),
            _layer_spec(layer, (4, D_MODEL)),
            _layer_spec(layer, (D_MODEL, IN_COLS)),
            _layer_spec(layer, (1, 2 * D_MODEL)),
            pl.BlockSpec((TM, LANES), lambda bi, i: (i, 0)),
            pl.BlockSpec((TM, LANES), lambda bi, i: (i, 0))],
        out_specs=[pl.BlockSpec((None, TM, QK_W), row),
                   pl.BlockSpec((None, TM, QK_W), row),
                   pl.BlockSpec((None, TM, ATTN_W), row),
                   pl.BlockSpec((None, TM, SSM_W), row),
                   pl.BlockSpec((None, TM, 2 * D_MODEL), row),
                   pl.BlockSpec((SSM_GROUPS, TM // S5_T, S5_TC), lambda bi, i: (0, bi * nt + i, 0)),
                   pl.BlockSpec((None, None, 8, LANES), lambda bi, i: (bi, i, 0, 0))],
        out_shape=[jax.ShapeDtypeStruct((b, n, QK_W), BF16),
                   jax.ShapeDtypeStruct((b, n, QK_W), BF16),
                   jax.ShapeDtypeStruct((b, n, ATTN_W), BF16),
                   jax.ShapeDtypeStruct((b, n, SSM_W), F32),
                   jax.ShapeDtypeStruct((b, n, 2 * D_MODEL), BF16),
                   jax.ShapeDtypeStruct((SSM_GROUPS, b * n // S5_T, S5_TC), BF16),
                   jax.ShapeDtypeStruct((b, nt, 8, LANES), F32)],
        scratch_shapes=[pltpu.VMEM((S5_SLABS, TM, LANES), F32)],
        compiler_params=_cparams(("parallel", "arbitrary")),
        name="in_proj",
    )(xl, xc, modsel, norm_g, w_in_b, gate_b, cos_t, sin_t)


def _attn_kernel(lamqk_ref, subg_ref, q_ref, k_ref, v_ref, o_ref,
                 qm_sc, m_sc, l_sc, acc_sc, *, lam_init, nk):
    ki = pl.program_id(3)

    @pl.when(ki == 0)
    def _init():
        q = q_ref[...]
        lane = lax.broadcasted_iota(jnp.int32, q.shape, 1)
        zero = jnp.zeros_like(q)
        qm_sc[0] = jnp.where(lane < HEAD_DIM, q, zero)
        qm_sc[1] = jnp.where(lane >= HEAD_DIM, q, zero)
        m_sc[...] = jnp.full(m_sc.shape, NEG_BIG, F32)
        l_sc[...] = jnp.zeros(l_sc.shape, F32)
        acc_sc[...] = jnp.zeros(acc_sc.shape, F32)

    k = k_ref[...]
    v = v_ref[...]
    for c in range(2):
        s = lax.dot_general(qm_sc[c], k, NT_DIMS, preferred_element_type=F32)
        m_prev = m_sc[c]
        m_new = jnp.maximum(m_prev, jnp.max(s, axis=1, keepdims=True))
        alpha = jnp.exp2(m_prev - m_new)
        p = jnp.exp2(s - m_new)
        l_sc[c] = alpha * l_sc[c] + jnp.sum(p, axis=1, keepdims=True)
        acc_sc[c] = alpha * acc_sc[c] + jnp.dot(p.astype(BF16), v, preferred_element_type=F32)
        m_sc[c] = m_new

    @pl.when(ki == nk - 1)
    def _fin():
        lq = lamqk_ref[...]
        lam = (jnp.exp(jnp.sum(lq[0:1] * lq[1:2], axis=1, keepdims=True))
               - jnp.exp(jnp.sum(lq[2:3] * lq[3:4], axis=1, keepdims=True)) + lam_init)
        o = acc_sc[0] / l_sc[0] - lam * (acc_sc[1] / l_sc[1])
        o_ref[...] = (_rms(o, subg_ref[...]) * (1.0 - lam_init)).astype(o_ref.dtype)


def _attn_bounded_kernel(lamqk_ref, subg_ref, kmax_ref, q_ref, k_ref, v_ref, o_ref,
                         qx_sc, kx_sc, l_sc, acc_sc, *, lam_init, nk):
    ki = pl.program_id(3)
    tq = q_ref.shape[0]
    tk = k_ref.shape[0]

    @pl.when(ki == 0)
    def _init():
        q = q_ref[...]
        lane = lax.broadcasted_iota(jnp.int32, (tq, LANES), 1)
        qf = q.astype(F32)
        n2 = jnp.dot((qf * qf).astype(BF16), _map_segments(), preferred_element_type=F32)
        bound = jnp.sqrt(n2) * kmax_ref[...] * BOUND_SLACK
        zero = jnp.zeros_like(q)
        qx_sc[0, :, 0:LANES] = jnp.where(lane < HEAD_DIM, q, zero)
        qx_sc[0, :, LANES:2 * LANES] = jnp.where(lane == 0, -bound, 0.0).astype(BF16)
        qx_sc[1, :, 0:LANES] = jnp.where(lane >= HEAD_DIM, q, zero)
        qx_sc[1, :, LANES:2 * LANES] = jnp.where(lane == 0, -pltpu.roll(bound, HEAD_DIM, 1), 0.0).astype(BF16)
        lane_k = lax.broadcasted_iota(jnp.int32, (tk, LANES), 1)
        kx_sc[:, LANES:2 * LANES] = jnp.where(lane_k == 0, 1.0, 0.0).astype(BF16)
        l_sc[...] = jnp.zeros(l_sc.shape, F32)
        acc_sc[...] = jnp.zeros(acc_sc.shape, F32)

    kx_sc[:, 0:LANES] = k_ref[...]
    kx = kx_sc[...]
    v = v_ref[...]
    for c in range(2):
        s = lax.dot_general(qx_sc[c], kx, NT_DIMS, preferred_element_type=F32)
        p = jnp.exp2(s)
        part = p[:, 0:LANES]
        for j in range(1, tk // LANES):
            part = part + p[:, j * LANES:(j + 1) * LANES]
        l_sc[c] += part
        acc_sc[c] += jnp.dot(p.astype(BF16), v, preferred_element_type=F32)

    @pl.when(ki == nk - 1)
    def _fin():
        lq = lamqk_ref[...]
        lam = (jnp.exp(jnp.sum(lq[0:1] * lq[1:2], axis=1, keepdims=True))
               - jnp.exp(jnp.sum(lq[2:3] * lq[3:4], axis=1, keepdims=True)) + lam_init)
        l0 = jnp.sum(l_sc[0], axis=1, keepdims=True)
        l1 = jnp.sum(l_sc[1], axis=1, keepdims=True)
        o = acc_sc[0] / l0 - lam * (acc_sc[1] / l1)
        o_ref[...] = (_rms(o, subg_ref[...]) * (1.0 - lam_init)).astype(o_ref.dtype)


def _drop_input(kern, idx, *refs):
    return kern(*refs[:idx], *refs[idx + 1:])


def _attention(layer, q, k, v, kmax, lam_qk, subln_g, lam_init, *, bounded, tq, tk, q0, nq, k0, nk,
               o_prev=None):
    b, n, _ = q.shape
    q_idx = lambda bi, h, qi, ki: (bi, qi + q0, h)
    kv_idx = lambda bi, h, qi, ki: (bi, ki + k0, h)
    in_specs = [pl.BlockSpec((None, 4, HEAD_DIM), lambda *_: (layer, 0, 0)),
                pl.BlockSpec((None, 1, V_DIM), lambda *_: (layer, 0, 0)),
                pl.BlockSpec((None, None, 1, LANES), lambda bi, h, qi, ki: (bi, h, 0, 0)),
                pl.BlockSpec((None, tq, LANES), q_idx),
                pl.BlockSpec((None, tk, LANES), kv_idx),
                pl.BlockSpec((None, tk, LANES), kv_idx)]
    args = [lam_qk, subln_g, kmax, q, k, v]
    if bounded:
        kern = functools.partial(_attn_bounded_kernel, lam_init=lam_init, nk=nk)
        scratch = [pltpu.VMEM((2, tq, 2 * LANES), BF16),
                   pltpu.VMEM((tk, 2 * LANES), BF16),
                   pltpu.VMEM((2, tq, LANES), F32),
                   pltpu.VMEM((2, tq, V_DIM), F32)]
    else:
        kern = functools.partial(_drop_input, functools.partial(_attn_kernel, lam_init=lam_init, nk=nk), 2)
        scratch = [pltpu.VMEM((2, tq, LANES), BF16),
                   pltpu.VMEM((2, tq, 1), F32),
                   pltpu.VMEM((2, tq, 1), F32),
                   pltpu.VMEM((2, tq, V_DIM), F32)]
    aliases = {}
    if o_prev is not None:
        in_specs.append(pl.BlockSpec(memory_space=pl.ANY))
        args.append(o_prev)
        aliases = {6: 0}
        kern = functools.partial(_drop_input, kern, 6)
    return pl.pallas_call(
        kern,
        grid=(b, N_HEADS, nq, nk),
        in_specs=in_specs,
        out_specs=pl.BlockSpec((None, tq, LANES), q_idx),
        out_shape=jax.ShapeDtypeStruct((b, n, ATTN_W), BF16),
        scratch_shapes=scratch,
        input_output_aliases=aliases,
        compiler_params=_cparams(("parallel", "parallel", "parallel", "arbitrary")),
        name="diff_attn_bounded" if bounded else "diff_attn",
    )(*args)


def _attn_pair_kernel(lamqk_ref, subg_ref, kmax_ref, q_ref, k_ref, v_ref, o_ref, *, lam_init, tk):
    tq = q_ref.shape[0]
    nkeys = k_ref.shape[0]
    lane = lax.broadcasted_iota(jnp.int32, (tq, LANES), 1)
    lane_k = lax.broadcasted_iota(jnp.int32, (tk, LANES), 1)
    one_tile = jnp.where(lane_k == 0, 1.0, 0.0).astype(BF16)
    lq = lamqk_ref[...]
    lam = (jnp.exp(jnp.sum(lq[0:1] * lq[1:2], axis=1, keepdims=True))
           - jnp.exp(jnp.sum(lq[2:3] * lq[3:4], axis=1, keepdims=True)) + lam_init)
    for hh in range(2):
        cols = slice(hh * LANES, (hh + 1) * LANES)
        q = q_ref[:, cols]
        qf = q.astype(F32)
        n2 = jnp.dot((qf * qf).astype(BF16), _map_segments(), preferred_element_type=F32)
        bound = jnp.sqrt(n2) * kmax_ref[hh] * BOUND_SLACK
        zero = jnp.zeros_like(q)
        qx = [jnp.concatenate([jnp.where(lane < HEAD_DIM, q, zero),
                               jnp.where(lane == 0, -bound, 0.0).astype(BF16)], axis=1),
              jnp.concatenate([jnp.where(lane >= HEAD_DIM, q, zero),
                               jnp.where(lane == 0, -pltpu.roll(bound, HEAD_DIM, 1), 0.0).astype(BF16)],
                              axis=1)]
        l = [jnp.zeros((tq, LANES), F32)] * 2
        acc = [jnp.zeros((tq, V_DIM), F32)] * 2
        for kb in range(nkeys // tk):
            rows = slice(kb * tk, (kb + 1) * tk)
            kx = jnp.concatenate([k_ref[rows, cols], one_tile], axis=1)
            v = v_ref[rows, cols]
            for c in range(2):
                s = lax.dot_general(qx[c], kx, NT_DIMS, preferred_element_type=F32)
                p = jnp.exp2(s)
                part = p[:, 0:LANES]
                for j in range(1, tk // LANES):
                    part = part + p[:, j * LANES:(j + 1) * LANES]
                l[c] = l[c] + part
                acc[c] = acc[c] + jnp.dot(p.astype(BF16), v, preferred_element_type=F32)
        l0 = jnp.sum(l[0], axis=1, keepdims=True)
        l1 = jnp.sum(l[1], axis=1, keepdims=True)
        o = acc[0] / l0 - lam * (acc[1] / l1)
        o_ref[:, cols] = (_rms(o, subg_ref[...]) * (1.0 - lam_init)).astype(o_ref.dtype)


def _attention_pairs(layer, q, k, v, kmax, lam_qk, subln_g, lam_init, *, tq, tk, nq, nkeys):
    b, n, _ = q.shape
    q_idx = lambda bi, hp, qi: (bi, qi, hp)
    kv_idx = lambda bi, hp, qi: (bi, 0, hp)
    return pl.pallas_call(
        functools.partial(_attn_pair_kernel, lam_init=lam_init, tk=tk),
        grid=(b, N_HEADS // 2, nq),
        in_specs=[pl.BlockSpec((None, 4, HEAD_DIM), lambda *_: (layer, 0, 0)),
                  pl.BlockSpec((None, 1, V_DIM), lambda *_: (layer, 0, 0)),
                  pl.BlockSpec((None, 2, 1, LANES), lambda bi, hp, qi: (bi, hp, 0, 0)),
                  pl.BlockSpec((None, tq, 2 * LANES), q_idx),
                  pl.BlockSpec((None, nkeys, 2 * LANES), kv_idx, pipeline_mode=pl.Buffered(1)),
                  pl.BlockSpec((None, nkeys, 2 * LANES), kv_idx, pipeline_mode=pl.Buffered(1))],
        out_specs=pl.BlockSpec((None, tq, 2 * LANES), q_idx),
        out_shape=jax.ShapeDtypeStruct((b, n, ATTN_W), BF16),
        compiler_params=_cparams(("parallel", "parallel", "arbitrary")),
        name="diff_attn_pairs",
    )(lam_qk, subln_g, kmax, q, k, v)


def _attention_layer(layer, q, k, v, nrm, lam_qk, subln_g, lam_init, seq, with_ctx):
    b = q.shape[0]
    nkeys = seq + CTX_LEN
    top = jnp.sqrt(jnp.max(nrm[:, :, 0:2, 0:2 * N_HEADS], axis=1))
    qmax, kmax = top[:, :, 0:N_HEADS], top[:, :, N_HEADS:]
    small = jnp.max(qmax * kmax) * BOUND_SLACK < BOUND_LIMIT
    kmax4 = jnp.repeat(jnp.swapaxes(kmax, 1, 2), HEAD_DIM, axis=-1).reshape(b, N_HEADS, 1, LANES)

    def run(bounded):
        if bounded:
            o_n = _attention_pairs(layer, q, k, v, kmax4, lam_qk, subln_g, lam_init,
                                   tq=TQ, tk=TK, nq=seq // TQ, nkeys=nkeys)
        else:
            o_n = _attention(layer, q, k, v, kmax4, lam_qk, subln_g, lam_init, bounded=False,
                             tq=TQ, tk=TK, q0=0, nq=seq // TQ, k0=0, nk=nkeys // TK)
        if with_ctx:
            o_n = _attention(layer, q, k, v, kmax4, lam_qk, subln_g, lam_init, bounded=bounded,
                             tq=TM, tk=CTX_LEN, q0=seq // TM, nq=1, k0=seq // CTX_LEN, nk=1,
                             o_prev=o_n)
        return o_n

    return lax.cond(small, lambda: run(True), lambda: run(False))


def _cmul(ar, ai, br, bi):
    return ar * br - ai * bi, ar * bi + ai * br


def _cpow_by_bits(lr, li, expo, nbits):
    pr = jnp.ones(expo.shape, F32)
    pi = jnp.zeros(expo.shape, F32)
    br, bi = lr, li
    for bit in range(nbits):
        on = ((expo >> bit) & 1) == 1
        fr = jnp.where(on, br, 1.0)
        fi = jnp.where(on, bi, 0.0)
        pr, pi = _cmul(pr, pi, fr, fi)
        if bit + 1 < nbits:
            br, bi = _cmul(br, bi, br, bi)
    return pr, pi


def _discretize(ar, ai, log_dt):
    dt = jnp.exp(log_dt)
    er = jnp.exp(ar * dt)
    lbr = er * jnp.cos(ai * dt)
    lbi = er * jnp.sin(ai * dt)
    den = ar * ar + ai * ai
    nr = lbr - 1.0
    cr = (nr * ar + lbi * ai) / den
    ci = (lbi * ar - nr * ai) / den
    return lbr, lbi, cr, ci


def _s5_kernel(x_ref, ar_ref, ai_ref, ldt_ref, btr_ref, bti_ref, ctr_ref, cti_ref,
               y_ref, sext, toep, csr_sc, csi_sc, zre, zim, hre, him, *, nrows, nchunk, nbatch, nctx):
    nlat = nchunk - nctx
    blk = lax.broadcasted_iota(jnp.int32, (S5_TC, PAIR), 0) // SSM_GROUP
    lane_g = lax.broadcasted_iota(jnp.int32, (1, PAIR), 1) // SSM_STATE

    for d in range(2):
        fwd = d == 0
        step_r, step_i = [], []
        for pi in range(S5_NP):
            lbr, lbi, cr, ci = _discretize(ar_ref[d, pi], ai_ref[d, pi], ldt_ref[d, pi])
            bbr, bbi = _cmul(cr, ci, btr_ref[d, pi], bti_ref[d, pi])
            bbr_t = jnp.concatenate([bbr] * S5_T, axis=0)
            bbi_t = jnp.concatenate([bbi] * S5_T, axis=0)
            ctr_t = jnp.concatenate([ctr_ref[d, pi]] * S5_T, axis=0)
            cti_t = jnp.concatenate([cti_ref[d, pi]] * S5_T, axis=0)
            upr, upi = _cpow_by_bits(lbr, lbi, blk, 4)
            dnr, dni = _cpow_by_bits(lbr, lbi, S5_T - 1 - blk, 4)
            (per, pei), (pwr, pwi) = ((upr, upi), (dnr, dni)) if fwd else ((dnr, dni), (upr, upi))
            bsr, bsi = _cmul(pwr, pwi, bbr_t, bbi_t)
            wcr, wci = _cmul(per, pei, ctr_t, cti_t)
            c1r, c1i = _cmul(wcr, wci, lbr, lbi)
            tr, ti = lbr, lbi
            for _ in range(int(math.log2(S5_T))):
                tr, ti = _cmul(tr, ti, tr, ti)
            step_r.append(tr)
            step_i.append(ti)

            z_r = z_i = None
            for gg in range(2):
                g = 2 * pi + gg
                lm = lane_g == gg
                strip = (lax.dot_general(jnp.where(lm, bbr, 0.0), wcr, NT_DIMS, preferred_element_type=F32,
                                         precision=lax.Precision.HIGHEST)
                         - lax.dot_general(jnp.where(lm, bbi, 0.0), wci, NT_DIMS, preferred_element_type=F32,
                                           precision=lax.Precision.HIGHEST))
                if fwd:
                    sext[:, 0:S5_TC] = jnp.zeros((SSM_GROUP, S5_TC), F32)
                    sext[:, S5_TC:2 * S5_TC] = strip
                    for j in range(S5_T):
                        lo = S5_TC - SSM_GROUP * j
                        toep[j * SSM_GROUP:(j + 1) * SSM_GROUP, :] = sext[:, lo:lo + S5_TC].astype(BF16)
                else:
                    sext[:, 0:S5_TC] = strip
                    sext[:, S5_TC:2 * S5_TC] = jnp.zeros((SSM_GROUP, S5_TC), F32)
                    for j in range(S5_T):
                        lo = SSM_GROUP * (S5_T - 1 - j)
                        toep[j * SSM_GROUP:(j + 1) * SSM_GROUP, :] = sext[:, lo:lo + S5_TC].astype(BF16)
                x = x_ref[g]
                yi = jnp.dot(x, toep[...], preferred_element_type=F32)
                if fwd:
                    y_ref[g] = yi
                else:
                    y_ref[g] += yi
                zr = jnp.dot(x, jnp.where(lm, bsr, 0.0).astype(BF16), preferred_element_type=F32)
                zi = jnp.dot(x, jnp.where(lm, bsi, 0.0).astype(BF16), preferred_element_type=F32)
                z_r = zr if z_r is None else z_r + zr
                z_i = zi if z_i is None else z_i + zi
                csr_sc[g] = jnp.where(lm, c1r, 0.0).astype(BF16)
                csi_sc[g] = jnp.where(lm, c1i, 0.0).astype(BF16)
            for b in range(nbatch):
                s = pi * nbatch + b
                zre[pl.ds(s, nrows, stride=SUBLANES), :] = z_r[b * nrows:(b + 1) * nrows]
                zim[pl.ds(s, nrows, stride=SUBLANES), :] = z_i[b * nrows:(b + 1) * nrows]

        sub = lax.broadcasted_iota(jnp.int32, (SUBLANES, PAIR), 0) // nbatch
        tr = jnp.zeros((SUBLANES, PAIR), F32)
        ti = jnp.zeros((SUBLANES, PAIR), F32)
        for pi in range(S5_NP):
            tr = jnp.where(sub == pi, step_r[pi], tr)
            ti = jnp.where(sub == pi, step_i[pi], ti)
        hre[...] = jnp.zeros(hre.shape, F32)
        him[...] = jnp.zeros(him.shape, F32)

        def body(n, carry):
            if fwd:
                ch = jnp.where(n < nctx, nlat + n, n - nctx)
            else:
                ch = nchunk - 1 - n
            h_r, h_i = carry
            r = pl.multiple_of(ch * SUBLANES, SUBLANES)
            hre[pl.ds(r, SUBLANES), :] = h_r
            him[pl.ds(r, SUBLANES), :] = h_i
            n_r, n_i = _cmul(tr, ti, h_r, h_i)
            return n_r + zre[pl.ds(r, SUBLANES), :], n_i + zim[pl.ds(r, SUBLANES), :]

        zero = jnp.zeros((SUBLANES, PAIR), F32)
        lax.fori_loop(0, nchunk, body, (zero, zero), unroll=8)

        for pi in range(S5_NP):
            for b in range(nbatch):
                s = pi * nbatch + b
                h_r = hre[pl.ds(s, nrows, stride=SUBLANES), :].astype(BF16)
                h_i = him[pl.ds(s, nrows, stride=SUBLANES), :].astype(BF16)
                for gg in range(2):
                    g = 2 * pi + gg
                    y_ref[g, b * nrows:(b + 1) * nrows, :] += (
                        lax.dot_general(h_r, csr_sc[g], NT_DIMS, preferred_element_type=F32)
                        - lax.dot_general(h_i, csi_sc[g], NT_DIMS, preferred_element_type=F32))


def _s5_params(a_re, a_im, log_dt, b_re, b_im, c_re, c_im):
    depth = a_re.shape[0]
    gp = SSM_GROUPS // 2
    row = lambda a: a.reshape(depth, 2, gp, 1, PAIR)
    ldt = jnp.broadcast_to(log_dt[..., None], a_re.shape)
    bt = lambda a: jnp.transpose(a.reshape(depth, 2, gp, 2, SSM_STATE, SSM_GROUP),
                                 (0, 1, 2, 5, 3, 4)).reshape(depth, 2, gp, SSM_GROUP, PAIR)
    ct = lambda a: jnp.transpose(a.reshape(depth, 2, gp, 2, SSM_GROUP, SSM_STATE),
                                 (0, 1, 2, 4, 3, 5)).reshape(depth, 2, gp, SSM_GROUP, PAIR)
    return row(a_re), row(a_im), row(ldt), bt(b_re), bt(b_im), ct(c_re), ct(c_im)


def _s5(layer, xg, params, nbatch, nvalid):
    g, nr, _ = xg.shape
    gstep = 2 * S5_NP
    assert S5_NP * nbatch == SUBLANES
    vec = pl.BlockSpec((None, 2, S5_NP, 1, PAIR), lambda i: (layer, 0, i, 0, 0))
    mat = pl.BlockSpec((None, 2, S5_NP, SSM_GROUP, PAIR), lambda i: (layer, 0, i, 0, 0))
    kern = functools.partial(_s5_kernel, nrows=nr // nbatch, nchunk=nvalid // S5_T, nbatch=nbatch,
                             nctx=CTX_LEN // S5_T)
    return pl.pallas_call(
        kern,
        grid=(g // gstep,),
        in_specs=[pl.BlockSpec((gstep, nr, S5_TC), lambda i: (i, 0, 0)),
                  vec, vec, vec, mat, mat, mat, mat],
        out_specs=pl.BlockSpec((gstep, nr, S5_TC), lambda i: (i, 0, 0)),
        out_shape=jax.ShapeDtypeStruct((g, nr, S5_TC), F32),
        scratch_shapes=[pltpu.VMEM((SSM_GROUP, 2 * S5_TC), F32),
                        pltpu.VMEM((S5_TC, S5_TC), BF16),
                        pltpu.VMEM((gstep, S5_TC, PAIR), BF16),
                        pltpu.VMEM((gstep, S5_TC, PAIR), BF16),
                        pltpu.VMEM((nr // nbatch * SUBLANES, PAIR), F32),
                        pltpu.VMEM((nr // nbatch * SUBLANES, PAIR), F32),
                        pltpu.VMEM((nr // nbatch * SUBLANES, PAIR), F32),
                        pltpu.VMEM((nr // nbatch * SUBLANES, PAIR), F32)],
        compiler_params=_cparams(("parallel",)),
        name="s5_scan",
    )(xg, *params)


def _tail_kernel(xl_ref, xc_ref, on_ref, yg_ref, u_ref, gate_ref, mod_ref, d_ref, wglu_ref, bglu_ref,
                 wbs_ref, wba_ref, wout_ref, ng_ref, w1_ref, b1_ref, w2_ref, b2_ref,
                 o_ref, ysc, *, n_lat_tiles):
    pa = jnp.dot(on_ref[...], wba_ref[...], preferred_element_type=F32)
    ys = _chunks_to_rows(yg_ref, ysc) + u_ref[...] * d_ref[...]
    gl = jax.nn.gelu(ys)
    z = gl * _sigmoid(jnp.dot(gl.astype(BF16), wglu_ref[...], preferred_element_type=F32) + bglu_ref[...])
    ps = jnp.dot(z.astype(BF16), wbs_ref[...], preferred_element_type=F32)
    gate = gate_ref[...].astype(F32)
    mix = gate[:, 0:D_MODEL] * pa + gate[:, D_MODEL:2 * D_MODEL] * ps
    m2 = jnp.dot(mix.astype(BF16), wout_ref[...], preferred_element_type=F32)
    x1 =_stream_tile(xl_ref, xc_ref, n_lat_tiles) + mod_ref[:, 2 * D_MODEL:3 * D_MODEL] * _rms(m2, ng_ref[1:2])
    h = _rms(x1, ng_ref[2:3]) * (1.0 + mod_ref[:, 4 * D_MODEL:5 * D_MODEL]) \
        + mod_ref[:, 3 * D_MODEL:4 * D_MODEL]
    hb = h.astype(BF16)
    o = b2_ref[...]
    for c0 in range(0, D_FF, FF_CHUNK):
        f = jnp.dot(hb, w1_ref[:, c0:c0 + FF_CHUNK], preferred_element_type=F32) + b1_ref[:, c0:c0 + FF_CHUNK]
        f = jnp.square(jnp.maximum(f, 0.0))
        o = o + jnp.dot(f.astype(BF16), w2_ref[c0:c0 + FF_CHUNK, :], preferred_element_type=F32)
    o_ref[...] = x1 + mod_ref[:, 5 * D_MODEL:6 * D_MODEL] * _rms(o, ng_ref[3:4])


def _tail(layer, xl, xc, o_n, yg, u, gates, modsel, ssm_d, w_glu, b_glu, w_br_s, w_br_a, w_out, norm_g,
          w1, b1, w2, b2, n_lat_tiles, nt):
    b = xl.shape[0]
    nt_all = n_lat_tiles + 1
    row = lambda bi, i: (bi, i, 0)
    return pl.pallas_call(
        functools.partial(_tail_kernel, n_lat_tiles=n_lat_tiles),
        grid=(b, nt),
        in_specs=_stream_specs(n_lat_tiles) + [
            pl.BlockSpec((None, TM, ATTN_W), row),
            pl.BlockSpec((SSM_GROUPS, TM // S5_T, S5_TC), lambda bi, i: (0, bi * nt_all + i, 0)),
            pl.BlockSpec((None, TM, SSM_W), row),
            pl.BlockSpec((None, TM, 2 * D_MODEL), row),
            _mod_spec(layer, n_lat_tiles),
            _layer_spec(layer, (1, SSM_W)),
            _layer_spec(layer, (SSM_W, SSM_W)),
            _layer_spec(layer, (1, SSM_W)),
            _layer_spec(layer, (SSM_W, D_MODEL)),
            _layer_spec(layer, (ATTN_W, D_MODEL)),
            _layer_spec(layer, (D_MODEL, D_MODEL)),
            _layer_spec(layer, (4, D_MODEL)),
            _layer_spec(layer, (D_MODEL, D_FF)),
            _layer_spec(layer, (1, D_FF)),
            _layer_spec(layer, (D_FF, D_MODEL)),
            _layer_spec(layer, (1, D_MODEL))],
        out_specs=pl.BlockSpec((None, TM, D_MODEL), row),
        out_shape=jax.ShapeDtypeStruct((b, nt * TM, D_MODEL), F32),
        scratch_shapes=[pltpu.VMEM((S5_SLABS, TM, LANES), F32)],
        compiler_params=_cparams(("parallel", "arbitrary")),
        name="merge_mlp",
    )(xl, xc, o_n, yg, u, gates, modsel, ssm_d, w_glu, b_glu, w_br_s, w_br_a, w_out, norm_g,
      w1, b1, w2, b2)


def _rope_tables(n_tokens):
    rows = n_tokens // GRID_W
    row = jnp.repeat(jnp.arange(rows, dtype=jnp.int32), GRID_W).astype(F32)
    col = jnp.tile(jnp.arange(GRID_W, dtype=jnp.int32), rows).astype(F32)
    inv_freq = ROPE_BASE ** (-jnp.arange(N_FREQ, dtype=F32) / N_FREQ)
    ang = jnp.stack([row[:, None] * inv_freq, col[:, None] * inv_freq], axis=1)
    cos, sin = jnp.cos(ang), jnp.sin(ang)
    cos_l = jnp.tile(jnp.concatenate([cos[:, 0], cos[:, 0], cos[:, 1], cos[:, 1]], axis=1), (1, 2))
    sin_l = jnp.tile(jnp.concatenate([-sin[:, 0], sin[:, 0], -sin[:, 1], sin[:, 1]], axis=1), (1, 2))
    cos_t = jnp.concatenate([cos_l, jnp.ones((TM, LANES), F32)], axis=0)
    sin_t = jnp.concatenate([sin_l, jnp.zeros((TM, LANES), F32)], axis=0)
    return cos_t, sin_t


def kernel(x, c, ctx, c_ctx, ada_w, ada_b, norm_g, w_in, gate_b, lam_qk, subln_g, w_br_a,
           ssm_a_re, ssm_a_im, ssm_b_re, ssm_b_im, ssm_c_re, ssm_c_im, ssm_log_dt, ssm_d,
           w_glu, b_glu, w_br_s, w_out, w_mlp1, b_mlp1, w_mlp2, b_mlp2):
    b, seq, _ = x.shape
    nvalid = seq + CTX_LEN
    assert b + 1 <= 8 and ctx.shape[1] == CTX_LEN and seq % TM == 0 and seq % TQ == 0 and nvalid % TK == 0
    n_lat_tiles = seq // TM
    cos_t, sin_t = _rope_tables(seq)
    xl, xc = x, jnp.concatenate([ctx, jnp.zeros((b, ROWS_PAD, D_MODEL), F32)], axis=1)
    cc = jnp.zeros((8, D_MODEL), F32).at[:b].set(c).at[b].set(c_ctx)
    mod = _modulation(cc, ada_w, ada_b)
    modsel = jnp.stack([mod[:, :b], jnp.broadcast_to(mod[:, b:b + 1], (DEPTH, b, 6 * D_MODEL))], axis=2)
    modsel = modsel.reshape(DEPTH, b, 2, 1, 6 * D_MODEL)
    s5_params = _s5_params(ssm_a_re, ssm_a_im, ssm_log_dt, ssm_b_re, ssm_b_im, ssm_c_re, ssm_c_im)
    row3 = lambda a: a.reshape(DEPTH, 1, -1)
    w_in_b, w_glu_b, w_br_s_b, w_br_a_b, w_out_b, w1_b, w2_b = (
        w.astype(BF16) for w in (w_in, w_glu, w_br_s, w_br_a, w_out, w_mlp1, w_mlp2))
    for i in range(DEPTH):
        last = i == DEPTH - 1
        lam_init = 0.8 - 0.6 * math.exp(-0.3 * i)
        q, k, v, u, gates, xg, nrm = _in_proj(i, xl, xc, modsel, norm_g, w_in_b, row3(gate_b),
                                              cos_t, sin_t, n_lat_tiles)
        o_n = _attention_layer(i, q, k, v, nrm, lam_qk, row3(subln_g), lam_init, seq, with_ctx=not last)
        yg = _s5(i, xg, s5_params, b, nvalid)
        xl = _tail(i, xl, xc, o_n, yg, u, gates, modsel, row3(ssm_d), w_glu_b, row3(b_glu), w_br_s_b,
                   w_br_a_b, w_out_b, norm_g, w1_b, row3(b_mlp1), w2_b, row3(b_mlp2),
                   n_lat_tiles, n_lat_tiles if last else n_lat_tiles + 1)
        if not last:
            xc = xl[:, seq:]
    return xl
```

```python
import functools
import math

import jax
import jax.numpy as jnp
from jax import lax
from jax.experimental import pallas as pl
from jax.experimental.pallas import tpu as pltpu

F32 = jnp.float32
BF16 = jnp.bfloat16

D_MODEL = 1024
DEPTH = 2
GRID_W = 64
CTX_LEN = 256
N_HEADS = 4
HEAD_DIM = 64
V_DIM = 128
QK_W = N_HEADS * 2 * HEAD_DIM
ATTN_W = 512
SSM_W = 512
SSM_GROUP = 16
SSM_GROUPS = 32
SSM_STATE = 64
D_FF = 4096
IN_COLS = 4096
N_FREQ = 16
ROPE_BASE = 10000.0
EPS = 1e-6
LOG2E = 1.4426950408889634

LANES = 128
SUBLANES = 8
TM = 512
ROWS_PAD = TM - CTX_LEN
GATHER_ROWS = 16
TQ = 1024
TK = 2816
S5_T = 16
S5_TC = S5_T * SSM_GROUP
S5_SLABS = SSM_W // LANES
S5_GPS = LANES // SSM_GROUP
PAIR = 2 * SSM_STATE
S5_NP = 4
FF_CHUNK = 1024
VMEM_LIMIT = 56 * 1024 * 1024

NEG_BIG = -1e30
BOUND_SLACK = 1.03
BOUND_LIMIT = 50.0

NT_DIMS = (((1,), (1,)), ((), ()))


def _cparams(sem):
    return pltpu.CompilerParams(dimension_semantics=sem, vmem_limit_bytes=VMEM_LIMIT)


def _layer_spec(layer, shape):
    nd = len(shape)
    return pl.BlockSpec((None,) + shape, lambda *_: (layer,) + (0,) * nd, pipeline_mode=pl.Buffered(1))


def _mod_spec(layer, ctx_tile):
    return pl.BlockSpec((None, None, None, 1, 6 * D_MODEL),
                        lambda bi, i: (layer, bi, i // ctx_tile, 0, 0))


def _stream_specs(n_lat_tiles):
    return [pl.BlockSpec((None, TM, D_MODEL), lambda bi, i: (bi, jnp.minimum(i, n_lat_tiles - 1), 0)),
            pl.BlockSpec((None, TM, D_MODEL), lambda bi, i: (bi, 0, 0))]


def _stream_tile(xl_ref, xc_ref, n_lat_tiles, tile_axis=1):
    return jnp.where(pl.program_id(tile_axis) >= n_lat_tiles, xc_ref[...], xl_ref[...])


def _rms(x, g):
    ms = jnp.mean(x * x, axis=-1, keepdims=True)
    return x * lax.rsqrt(ms + EPS) * g


def _sigmoid(x):
    return 0.5 * jnp.tanh(0.5 * x) + 0.5


def _mod_kernel(c_ref, w_ref, b_ref, o_ref):
    c = c_ref[...]
    s = c * jax.nn.sigmoid(c)
    o_ref[...] = jnp.dot(s.astype(BF16), w_ref[...].astype(BF16),
                         preferred_element_type=F32) + b_ref[...]


def _modulation(cc, ada_w, ada_b):
    depth, _, n = ada_w.shape
    bn = 1024
    return pl.pallas_call(
        _mod_kernel,
        grid=(depth, n // bn),
        in_specs=[pl.BlockSpec((8, D_MODEL), lambda l, j: (0, 0)),
                  pl.BlockSpec((None, D_MODEL, bn), lambda l, j: (l, 0, j)),
                  pl.BlockSpec((None, 1, bn), lambda l, j: (l, 0, j))],
        out_specs=pl.BlockSpec((None, 8, bn), lambda l, j: (l, 0, j)),
        out_shape=jax.ShapeDtypeStruct((depth, 8, n), F32),
        compiler_params=_cparams(("arbitrary", "arbitrary")),
        name="adaln_mod",
    )(cc, ada_w, ada_b.reshape(depth, 1, n))


def _lane_window(off):
    lane = lax.broadcasted_iota(jnp.int32, (GATHER_ROWS, LANES), 1)
    return (lane >= off) & (lane < off + SSM_GROUP)


def _rows_to_chunks(u, usc, xg_ref):
    for sl in range(S5_SLABS):
        usc[sl] = u[:, sl * LANES:(sl + 1) * LANES]
    for part in range(TM // (S5_T * GATHER_ROWS)):
        r0 = part * GATHER_ROWS
        for sl in range(S5_SLABS):
            acc = [[None] * (S5_TC // LANES) for _ in range(S5_GPS)]
            for t in range(S5_T):
                v = usc[sl, pl.ds(r0 * S5_T + t, GATHER_ROWS, stride=S5_T), :]
                lt, off = divmod(t * SSM_GROUP, LANES)
                win = _lane_window(off)
                for gl in range(S5_GPS):
                    shift = (off - gl * SSM_GROUP) % LANES
                    moved = pltpu.roll(v, shift, 1) if shift else v
                    prev = acc[gl][lt]
                    acc[gl][lt] = jnp.where(win, moved, 0.0 if prev is None else prev)
            for gl in range(S5_GPS):
                for lt in range(S5_TC // LANES):
                    xg_ref[sl * S5_GPS + gl, r0:r0 + GATHER_ROWS, lt * LANES:(lt + 1) * LANES] = (
                        acc[gl][lt].astype(xg_ref.dtype))


def _chunks_to_rows(yg_ref, ysc):
    for part in range(TM // (S5_T * GATHER_ROWS)):
        r0 = part * GATHER_ROWS
        for sl in range(S5_SLABS):
            tiles = [[yg_ref[sl * S5_GPS + gl, r0:r0 + GATHER_ROWS, lt * LANES:(lt + 1) * LANES]
                      for lt in range(S5_TC // LANES)] for gl in range(S5_GPS)]
            for t in range(S5_T):
                lt, off = divmod(t * SSM_GROUP, LANES)
                v = None
                for gl in range(S5_GPS):
                    shift = (gl * SSM_GROUP - off) % LANES
                    src = tiles[gl][lt]
                    moved = pltpu.roll(src, shift, 1) if shift else src
                    v = jnp.where(_lane_window(gl * SSM_GROUP), moved, 0.0 if v is None else v)
                ysc[sl, pl.ds(r0 * S5_T + t, GATHER_ROWS, stride=S5_T), :] = v
    return jnp.concatenate([ysc[sl] for sl in range(S5_SLABS)], axis=1)


def _map_segments():
    li = lax.broadcasted_iota(jnp.int32, (LANES, LANES), 0) // HEAD_DIM
    lj = lax.broadcasted_iota(jnp.int32, (LANES, LANES), 1) // HEAD_DIM
    return (li == lj).astype(BF16)


def _in_kernel(xl_ref, xc_ref, mod_ref, g_ref, w_ref, gb_ref, cos_ref, sin_ref,
               q_ref, k_ref, v_ref, u_ref, gate_ref, xg_ref, nrm_ref, usc, *, n_lat_tiles):
    for bi in range(xl_ref.shape[0]):
        _in_rows(_stream_tile(xl_ref.at[bi], xc_ref.at[bi], n_lat_tiles, tile_axis=0),
                 mod_ref.at[bi], g_ref, w_ref, gb_ref, cos_ref, sin_ref,
                 q_ref.at[bi], k_ref.at[bi], v_ref.at[bi], u_ref.at[bi], gate_ref.at[bi],
                 xg_ref.at[:, bi], nrm_ref.at[bi], usc.at[bi])


def _in_rows(x, mod_ref, g_ref, w_ref, gb_ref, cos_ref, sin_ref,
             q_ref, k_ref, v_ref, u_ref, gate_ref, xg_ref, nrm_ref, usc):
    h = _rms(x, g_ref[0:1]) * (1.0 + mod_ref[:, D_MODEL:2 * D_MODEL]) + mod_ref[:, 0:D_MODEL]
    hb = h.astype(BF16)
    cos = cos_ref[...]
    sin = sin_ref[...]
    lane = lax.broadcasted_iota(jnp.int32, (TM, LANES), 1)
    first_half = (lane & 31) < 16
    sel = (lax.broadcasted_iota(jnp.int32, (8, LANES), 0)
           == lax.broadcasted_iota(jnp.int32, (8, LANES), 1) // HEAD_DIM).astype(BF16)
    slot = lax.broadcasted_iota(jnp.int32, (8, LANES), 1)
    nrm = jnp.zeros((8, LANES), F32)

    def rope_store(col0, out_ref, scale, slot0, nrm):
        t = jnp.dot(hb, w_ref[:, col0:col0 + QK_W], preferred_element_type=F32)
        for j in range(QK_W // LANES):
            tj = t[:, j * LANES:(j + 1) * LANES]
            partner = jnp.where(first_half, pltpu.roll(tj, LANES - 16, 1), pltpu.roll(tj, 16, 1))
            r = tj * cos + partner * sin
            if scale != 1.0:
                r = r * scale
            out_ref[:, j * LANES:(j + 1) * LANES] = r.astype(out_ref.dtype)
            n2 = lax.dot_general(sel, (r * r).astype(BF16), NT_DIMS, preferred_element_type=F32)
            nrm = jnp.where(slot == slot0 + j, jnp.max(n2, axis=1, keepdims=True), nrm)
        return nrm

    nrm = rope_store(0, q_ref, HEAD_DIM ** -0.5 * LOG2E, 0, nrm)
    nrm = rope_store(QK_W, k_ref, 1.0, N_HEADS, nrm)
    nrm_ref[...] = nrm
    v_ref[...] = jnp.dot(hb, w_ref[:, 1024:1536], preferred_element_type=F32).astype(v_ref.dtype)
    u = jnp.dot(hb, w_ref[:, 1536:2048], preferred_element_type=F32)
    u_ref[...] = u
    _rows_to_chunks(u, usc, xg_ref)
    g = jnp.dot(hb, w_ref[:, 2048:IN_COLS], preferred_element_type=F32) + gb_ref[...]
    gate_ref[...] = _sigmoid(g).astype(gate_ref.dtype)


def _in_proj(layer, xl, xc, modsel, norm_g, w_in_b, gate_b, cos_t, sin_t, n_lat_tiles):
    b = xl.shape[0]
    nt = n_lat_tiles + 1
    n = nt * TM
    row = lambda i: (0, i, 0)
    outs = pl.pallas_call(
        functools.partial(_in_kernel, n_lat_tiles=n_lat_tiles),
        grid=(nt,),
        in_specs=[pl.BlockSpec((b, TM, D_MODEL), lambda i: (0, jnp.minimum(i, n_lat_tiles - 1), 0)),
                  pl.BlockSpec((b, TM, D_MODEL), lambda i: (0, 0, 0)),
                  pl.BlockSpec((None, b, None, 1, 6 * D_MODEL), lambda i: (layer, 0, i // n_lat_tiles, 0, 0)),
                  _layer_spec(layer, (4, D_MODEL)),
                  _layer_spec(layer, (D_MODEL, IN_COLS)),
                  _layer_spec(layer, (1, 2 * D_MODEL)),
                  pl.BlockSpec((TM, LANES), lambda i: (i, 0)),
                  pl.BlockSpec((TM, LANES), lambda i: (i, 0))],
        out_specs=[pl.BlockSpec((b, TM, QK_W), row),
                   pl.BlockSpec((b, TM, QK_W), row),
                   pl.BlockSpec((b, TM, ATTN_W), row),
                   pl.BlockSpec((b, TM, SSM_W), row),
                   pl.BlockSpec((b, TM, 2 * D_MODEL), row),
                   pl.BlockSpec((SSM_GROUPS, b, TM // S5_T, S5_TC), lambda i: (0, 0, i, 0)),
                   pl.BlockSpec((b, None, 8, LANES), lambda i: (0, i, 0, 0))],
        out_shape=[jax.ShapeDtypeStruct((b, n, QK_W), BF16),
                   jax.ShapeDtypeStruct((b, n, QK_W), BF16),
                   jax.ShapeDtypeStruct((b, n, ATTN_W), BF16),
                   jax.ShapeDtypeStruct((b, n, SSM_W), F32),
                   jax.ShapeDtypeStruct((b, n, 2 * D_MODEL), BF16),
                   jax.ShapeDtypeStruct((SSM_GROUPS, b, n // S5_T, S5_TC), BF16),
                   jax.ShapeDtypeStruct((b, nt, 8, LANES), F32)],
        scratch_shapes=[pltpu.VMEM((b, S5_SLABS, TM, LANES), F32)],
        compiler_params=_cparams(("arbitrary",)),
        name="in_proj",
    )(xl, xc, modsel, norm_g, w_in_b, gate_b, cos_t, sin_t)
    q, k, v, u, gates, xg, nrm = outs
    return q, k, v, u, gates, xg.reshape(SSM_GROUPS, b * n // S5_T, S5_TC), nrm


def _attn_kernel(lamqk_ref, subg_ref, q_ref, k_ref, v_ref, o_ref,
                 qm_sc, m_sc, l_sc, acc_sc, *, lam_init, nk):
    ki = pl.program_id(3)

    @pl.when(ki == 0)
    def _init():
        q = q_ref[...]
        lane = lax.broadcasted_iota(jnp.int32, q.shape, 1)
        zero = jnp.zeros_like(q)
        qm_sc[0] = jnp.where(lane < HEAD_DIM, q, zero)
        qm_sc[1] = jnp.where(lane >= HEAD_DIM, q, zero)
        m_sc[...] = jnp.full(m_sc.shape, NEG_BIG, F32)
        l_sc[...] = jnp.zeros(l_sc.shape, F32)
        acc_sc[...] = jnp.zeros(acc_sc.shape, F32)

    k = k_ref[...]
    v = v_ref[...]
    for c in range(2):
        s = lax.dot_general(qm_sc[c], k, NT_DIMS, preferred_element_type=F32)
        m_prev = m_sc[c]
        m_new = jnp.maximum(m_prev, jnp.max(s, axis=1, keepdims=True))
        alpha = jnp.exp2(m_prev - m_new)
        p = jnp.exp2(s - m_new)
        l_sc[c] = alpha * l_sc[c] + jnp.sum(p, axis=1, keepdims=True)
        acc_sc[c] = alpha * acc_sc[c] + jnp.dot(p.astype(BF16), v, preferred_element_type=F32)
        m_sc[c] = m_new

    @pl.when(ki == nk - 1)
    def _fin():
        lq = lamqk_ref[...]
        lam = (jnp.exp(jnp.sum(lq[0:1] * lq[1:2], axis=1, keepdims=True))
               - jnp.exp(jnp.sum(lq[2:3] * lq[3:4], axis=1, keepdims=True)) + lam_init)
        o = acc_sc[0] / l_sc[0] - lam * (acc_sc[1] / l_sc[1])
        o_ref[...] = (_rms(o, subg_ref[...]) * (1.0 - lam_init)).astype(o_ref.dtype)


def _attn_bounded_kernel(lamqk_ref, subg_ref, kmax_ref, q_ref, k_ref, v_ref, o_ref,
                         qx_sc, kx_sc, l_sc, acc_sc, *, lam_init, nk):
    ki = pl.program_id(3)
    tq = q_ref.shape[0]
    tk = k_ref.shape[0]

    @pl.when(ki == 0)
    def _init():
        q = q_ref[...]
        lane = lax.broadcasted_iota(jnp.int32, (tq, LANES), 1)
        qf = q.astype(F32)
        n2 = jnp.dot((qf * qf).astype(BF16), _map_segments(), preferred_element_type=F32)
        bound = jnp.sqrt(n2) * kmax_ref[...] * BOUND_SLACK
        zero = jnp.zeros_like(q)
        qx_sc[0, :, 0:LANES] = jnp.where(lane < HEAD_DIM, q, zero)
        qx_sc[0, :, LANES:2 * LANES] = jnp.where(lane == 0, -bound, 0.0).astype(BF16)
        qx_sc[1, :, 0:LANES] = jnp.where(lane >= HEAD_DIM, q, zero)
        qx_sc[1, :, LANES:2 * LANES] = jnp.where(lane == 0, -pltpu.roll(bound, HEAD_DIM, 1), 0.0).astype(BF16)
        lane_k = lax.broadcasted_iota(jnp.int32, (tk, LANES), 1)
        kx_sc[:, LANES:2 * LANES] = jnp.where(lane_k == 0, 1.0, 0.0).astype(BF16)
        l_sc[...] = jnp.zeros(l_sc.shape, F32)
        acc_sc[...] = jnp.zeros(acc_sc.shape, F32)

    kx_sc[:, 0:LANES] = k_ref[...]
    kx = kx_sc[...]
    v = v_ref[...]
    for c in range(2):
        s = lax.dot_general(qx_sc[c], kx, NT_DIMS, preferred_element_type=F32)
        p = jnp.exp2(s)
        part = p[:, 0:LANES]
        for j in range(1, tk // LANES):
            part = part + p[:, j * LANES:(j + 1) * LANES]
        l_sc[c] += part
        acc_sc[c] += jnp.dot(p.astype(BF16), v, preferred_element_type=F32)

    @pl.when(ki == nk - 1)
    def _fin():
        lq = lamqk_ref[...]
        lam = (jnp.exp(jnp.sum(lq[0:1] * lq[1:2], axis=1, keepdims=True))
               - jnp.exp(jnp.sum(lq[2:3] * lq[3:4], axis=1, keepdims=True)) + lam_init)
        l0 = jnp.sum(l_sc[0], axis=1, keepdims=True)
        l1 = jnp.sum(l_sc[1], axis=1, keepdims=True)
        o = acc_sc[0] / l0 - lam * (acc_sc[1] / l1)
        o_ref[...] = (_rms(o, subg_ref[...]) * (1.0 - lam_init)).astype(o_ref.dtype)


def _drop_input(kern, idx, *refs):
    return kern(*refs[:idx], *refs[idx + 1:])


def _attention(layer, q, k, v, kmax, lam_qk, subln_g, lam_init, *, bounded, tq, tk, q0, nq, k0, nk,
               o_prev=None):
    b, n, _ = q.shape
    q_idx = lambda bi, h, qi, ki: (bi, qi + q0, h)
    kv_idx = lambda bi, h, qi, ki: (bi, ki + k0, h)
    in_specs = [pl.BlockSpec((None, 4, HEAD_DIM), lambda *_: (layer, 0, 0)),
                pl.BlockSpec((None, 1, V_DIM), lambda *_: (layer, 0, 0)),
                pl.BlockSpec((None, None, 1, LANES), lambda bi, h, qi, ki: (bi, h, 0, 0)),
                pl.BlockSpec((None, tq, LANES), q_idx),
                pl.BlockSpec((None, tk, LANES), kv_idx),
                pl.BlockSpec((None, tk, LANES), kv_idx)]
    args = [lam_qk, subln_g, kmax, q, k, v]
    if bounded:
        kern = functools.partial(_attn_bounded_kernel, lam_init=lam_init, nk=nk)
        scratch = [pltpu.VMEM((2, tq, 2 * LANES), BF16),
                   pltpu.VMEM((tk, 2 * LANES), BF16),
                   pltpu.VMEM((2, tq, LANES), F32),
                   pltpu.VMEM((2, tq, V_DIM), F32)]
    else:
        kern = functools.partial(_drop_input, functools.partial(_attn_kernel, lam_init=lam_init, nk=nk), 2)
        scratch = [pltpu.VMEM((2, tq, LANES), BF16),
                   pltpu.VMEM((2, tq, 1), F32),
                   pltpu.VMEM((2, tq, 1), F32),
                   pltpu.VMEM((2, tq, V_DIM), F32)]
    aliases = {}
    if o_prev is not None:
        in_specs.append(pl.BlockSpec(memory_space=pl.ANY))
        args.append(o_prev)
        aliases = {6: 0}
        kern = functools.partial(_drop_input, kern, 6)
    return pl.pallas_call(
        kern,
        grid=(b, N_HEADS, nq, nk),
        in_specs=in_specs,
        out_specs=pl.BlockSpec((None, tq, LANES), q_idx),
        out_shape=jax.ShapeDtypeStruct((b, n, ATTN_W), BF16),
        scratch_shapes=scratch,
        input_output_aliases=aliases,
        compiler_params=_cparams(("parallel", "parallel", "parallel", "arbitrary")),
        name="diff_attn_bounded" if bounded else "diff_attn",
    )(*args)


def _attn_pair_kernel(lamqk_ref, subg_ref, kmax_ref, q_ref, k_ref, v_ref, o_ref, *, lam_init, tk):
    tq = q_ref.shape[0]
    nkeys = k_ref.shape[0]
    lane = lax.broadcasted_iota(jnp.int32, (tq, LANES), 1)
    lane_k = lax.broadcasted_iota(jnp.int32, (tk, LANES), 1)
    one_tile = jnp.where(lane_k == 0, 1.0, 0.0).astype(BF16)
    lq = lamqk_ref[...]
    lam = (jnp.exp(jnp.sum(lq[0:1] * lq[1:2], axis=1, keepdims=True))
           - jnp.exp(jnp.sum(lq[2:3] * lq[3:4], axis=1, keepdims=True)) + lam_init)
    for hh in range(2):
        cols = slice(hh * LANES, (hh + 1) * LANES)
        q = q_ref[:, cols]
        qf = q.astype(F32)
        n2 = jnp.dot((qf * qf).astype(BF16), _map_segments(), preferred_element_type=F32)
        bound = jnp.sqrt(n2) * kmax_ref[hh] * BOUND_SLACK
        zero = jnp.zeros_like(q)
        qx = [jnp.concatenate([jnp.where(lane < HEAD_DIM, q, zero),
                               jnp.where(lane == 0, -bound, 0.0).astype(BF16)], axis=1),
              jnp.concatenate([jnp.where(lane >= HEAD_DIM, q, zero),
                               jnp.where(lane == 0, -pltpu.roll(bound, HEAD_DIM, 1), 0.0).astype(BF16)],
                              axis=1)]
        l = [jnp.zeros((tq, LANES), F32)] * 2
        acc = [jnp.zeros((tq, V_DIM), F32)] * 2
        for kb in range(nkeys // tk):
            rows = slice(kb * tk, (kb + 1) * tk)
            kx = jnp.concatenate([k_ref[rows, cols], one_tile], axis=1)
            v = v_ref[rows, cols]
            for c in range(2):
                s = lax.dot_general(qx[c], kx, NT_DIMS, preferred_element_type=F32)
                p = jnp.exp2(s)
                part = p[:, 0:LANES]
                for j in range(1, tk // LANES):
                    part = part + p[:, j * LANES:(j + 1) * LANES]
                l[c] = l[c] + part
                acc[c] = acc[c] + jnp.dot(p.astype(BF16), v, preferred_element_type=F32)
        l0 = jnp.sum(l[0], axis=1, keepdims=True)
        l1 = jnp.sum(l[1], axis=1, keepdims=True)
        o = acc[0] / l0 - lam * (acc[1] / l1)
        o_ref[:, cols] = (_rms(o, subg_ref[...]) * (1.0 - lam_init)).astype(o_ref.dtype)


def _attention_pairs(layer, q, k, v, kmax, lam_qk, subln_g, lam_init, *, tq, tk, nq, nkeys):
    b, n, _ = q.shape
    q_idx = lambda bi, hp, qi: (bi, qi, hp)
    kv_idx = lambda bi, hp, qi: (bi, 0, hp)
    return pl.pallas_call(
        functools.partial(_attn_pair_kernel, lam_init=lam_init, tk=tk),
        grid=(b, N_HEADS // 2, nq),
        in_specs=[pl.BlockSpec((None, 4, HEAD_DIM), lambda *_: (layer, 0, 0)),
                  pl.BlockSpec((None, 1, V_DIM), lambda *_: (layer, 0, 0)),
                  pl.BlockSpec((None, 2, 1, LANES), lambda bi, hp, qi: (bi, hp, 0, 0)),
                  pl.BlockSpec((None, tq, 2 * LANES), q_idx),
                  pl.BlockSpec((None, nkeys, 2 * LANES), kv_idx, pipeline_mode=pl.Buffered(1)),
                  pl.BlockSpec((None, nkeys, 2 * LANES), kv_idx, pipeline_mode=pl.Buffered(1))],
        out_specs=pl.BlockSpec((None, tq, 2 * LANES), q_idx),
        out_shape=jax.ShapeDtypeStruct((b, n, ATTN_W), BF16),
        compiler_params=_cparams(("parallel", "parallel", "arbitrary")),
        name="diff_attn_pairs",
    )(lam_qk, subln_g, kmax, q, k, v)


def _attention_layer(layer, q, k, v, nrm, lam_qk, subln_g, lam_init, seq, with_ctx):
    b = q.shape[0]
    nkeys = seq + CTX_LEN
    top = jnp.sqrt(jnp.max(nrm[:, :, 0:2, 0:2 * N_HEADS], axis=1))
    qmax, kmax = top[:, :, 0:N_HEADS], top[:, :, N_HEADS:]
    small = jnp.max(qmax * kmax) * BOUND_SLACK < BOUND_LIMIT
    kmax4 = jnp.repeat(jnp.swapaxes(kmax, 1, 2), HEAD_DIM, axis=-1).reshape(b, N_HEADS, 1, LANES)

    def run(bounded):
        if bounded:
            o_n = _attention_pairs(layer, q, k, v, kmax4, lam_qk, subln_g, lam_init,
                                   tq=TQ, tk=TK, nq=seq // TQ, nkeys=nkeys)
        else:
            o_n = _attention(layer, q, k, v, kmax4, lam_qk, subln_g, lam_init, bounded=False,
                             tq=TQ, tk=TK, q0=0, nq=seq // TQ, k0=0, nk=nkeys // TK)
        if with_ctx:
            o_n = _attention(layer, q, k, v, kmax4, lam_qk, subln_g, lam_init, bounded=bounded,
                             tq=TM, tk=CTX_LEN, q0=seq // TM, nq=1, k0=seq // CTX_LEN, nk=1,
                             o_prev=o_n)
        return o_n

    return lax.cond(small, lambda: run(True), lambda: run(False))


def _cmul(ar, ai, br, bi):
    return ar * br - ai * bi, ar * bi + ai * br


def _cpow_by_bits(lr, li, expo, nbits):
    pr = jnp.ones(expo.shape, F32)
    pi = jnp.zeros(expo.shape, F32)
    br, bi = lr, li
    for bit in range(nbits):
        on = ((expo >> bit) & 1) == 1
        fr = jnp.where(on, br, 1.0)
        fi = jnp.where(on, bi, 0.0)
        pr, pi = _cmul(pr, pi, fr, fi)
        if bit + 1 < nbits:
            br, bi = _cmul(br, bi, br, bi)
    return pr, pi


def _discretize(ar, ai, log_dt):
    dt = jnp.exp(log_dt)
    er = jnp.exp(ar * dt)
    lbr = er * jnp.cos(ai * dt)
    lbi = er * jnp.sin(ai * dt)
    den = ar * ar + ai * ai
    nr = lbr - 1.0
    cr = (nr * ar + lbi * ai) / den
    ci = (lbi * ar - nr * ai) / den
    return lbr, lbi, cr, ci


def _s5_kernel(x_ref, ar_ref, ai_ref, ldt_ref, btr_ref, bti_ref, ctr_ref, cti_ref,
               y_ref, sext, toep, csr_sc, csi_sc, zre, zim, hre, him, *, nrows, nchunk, nbatch, nctx):
    nlat = nchunk - nctx
    blk = lax.broadcasted_iota(jnp.int32, (S5_TC, PAIR), 0) // SSM_GROUP
    lane_g = lax.broadcasted_iota(jnp.int32, (1, PAIR), 1) // SSM_STATE

    for d in range(2):
        fwd = d == 0
        step_r, step_i = [], []
        for pi in range(S5_NP):
            lbr, lbi, cr, ci = _discretize(ar_ref[d, pi], ai_ref[d, pi], ldt_ref[d, pi])
            bbr, bbi = _cmul(cr, ci, btr_ref[d, pi], bti_ref[d, pi])
            bbr_t = jnp.concatenate([bbr] * S5_T, axis=0)
            bbi_t = jnp.concatenate([bbi] * S5_T, axis=0)
            ctr_t = jnp.concatenate([ctr_ref[d, pi]] * S5_T, axis=0)
            cti_t = jnp.concatenate([cti_ref[d, pi]] * S5_T, axis=0)
            upr, upi = _cpow_by_bits(lbr, lbi, blk, 4)
            dnr, dni = _cpow_by_bits(lbr, lbi, S5_T - 1 - blk, 4)
            (per, pei), (pwr, pwi) = ((upr, upi), (dnr, dni)) if fwd else ((dnr, dni), (upr, upi))
            bsr, bsi = _cmul(pwr, pwi, bbr_t, bbi_t)
            wcr, wci = _cmul(per, pei, ctr_t, cti_t)
            c1r, c1i = _cmul(wcr, wci, lbr, lbi)
            tr, ti = lbr, lbi
            for _ in range(int(math.log2(S5_T))):
                tr, ti = _cmul(tr, ti, tr, ti)
            step_r.append(tr)
            step_i.append(ti)

            z_r = z_i = None
            for gg in range(2):
                g = 2 * pi + gg
                lm = lane_g == gg
                strip = (lax.dot_general(jnp.where(lm, bbr, 0.0), wcr, NT_DIMS, preferred_element_type=F32,
                                         precision=lax.Precision.HIGHEST)
                         - lax.dot_general(jnp.where(lm, bbi, 0.0), wci, NT_DIMS, preferred_element_type=F32,
                                           precision=lax.Precision.HIGHEST))
                if fwd:
                    sext[:, 0:S5_TC] = jnp.zeros((SSM_GROUP, S5_TC), F32)
                    sext[:, S5_TC:2 * S5_TC] = strip
                    for j in range(S5_T):
                        lo = S5_TC - SSM_GROUP * j
                        toep[j * SSM_GROUP:(j + 1) * SSM_GROUP, :] = sext[:, lo:lo + S5_TC].astype(BF16)
                else:
                    sext[:, 0:S5_TC] = strip
                    sext[:, S5_TC:2 * S5_TC] = jnp.zeros((SSM_GROUP, S5_TC), F32)
                    for j in range(S5_T):
                        lo = SSM_GROUP * (S5_T - 1 - j)
                        toep[j * SSM_GROUP:(j + 1) * SSM_GROUP, :] = sext[:, lo:lo + S5_TC].astype(BF16)
                x = x_ref[g]
                yi = jnp.dot(x, toep[...], preferred_element_type=F32)
                if fwd:
                    y_ref[g] = yi
                else:
                    y_ref[g] += yi
                zr = jnp.dot(x, jnp.where(lm, bsr, 0.0).astype(BF16), preferred_element_type=F32)
                zi = jnp.dot(x, jnp.where(lm, bsi, 0.0).astype(BF16), preferred_element_type=F32)
                z_r = zr if z_r is None else z_r + zr
                z_i = zi if z_i is None else z_i + zi
                csr_sc[g] = jnp.where(lm, c1r, 0.0).astype(BF16)
                csi_sc[g] = jnp.where(lm, c1i, 0.0).astype(BF16)
            for b in range(nbatch):
                s = pi * nbatch + b
                zre[pl.ds(s, nrows, stride=SUBLANES), :] = z_r[b * nrows:(b + 1) * nrows]
                zim[pl.ds(s, nrows, stride=SUBLANES), :] = z_i[b * nrows:(b + 1) * nrows]

        sub = lax.broadcasted_iota(jnp.int32, (SUBLANES, PAIR), 0) // nbatch
        tr = jnp.zeros((SUBLANES, PAIR), F32)
        ti = jnp.zeros((SUBLANES, PAIR), F32)
        for pi in range(S5_NP):
            tr = jnp.where(sub == pi, step_r[pi], tr)
            ti = jnp.where(sub == pi, step_i[pi], ti)
        hre[...] = jnp.zeros(hre.shape, F32)
        him[...] = jnp.zeros(him.shape, F32)

        def body(n, carry):
            if fwd:
                ch = jnp.where(n < nctx, nlat + n, n - nctx)
            else:
                ch = nchunk - 1 - n
            h_r, h_i = carry
            r = pl.multiple_of(ch * SUBLANES, SUBLANES)
            hre[pl.ds(r, SUBLANES), :] = h_r
            him[pl.ds(r, SUBLANES), :] = h_i
            n_r, n_i = _cmul(tr, ti, h_r, h_i)
            return n_r + zre[pl.ds(r, SUBLANES), :], n_i + zim[pl.ds(r, SUBLANES), :]

        zero = jnp.zeros((SUBLANES, PAIR), F32)
        lax.fori_loop(0, nchunk, body, (zero, zero), unroll=8)

        for pi in range(S5_NP):
            for b in range(nbatch):
                s = pi * nbatch + b
                h_r = hre[pl.ds(s, nrows, stride=SUBLANES), :].astype(BF16)
                h_i = him[pl.ds(s, nrows, stride=SUBLANES), :].astype(BF16)
                for gg in range(2):
                    g = 2 * pi + gg
                    y_ref[g, b * nrows:(b + 1) * nrows, :] += (
                        lax.dot_general(h_r, csr_sc[g], NT_DIMS, preferred_element_type=F32)
                        - lax.dot_general(h_i, csi_sc[g], NT_DIMS, preferred_element_type=F32))


def _s5_params(a_re, a_im, log_dt, b_re, b_im, c_re, c_im):
    depth = a_re.shape[0]
    gp = SSM_GROUPS // 2
    row = lambda a: a.reshape(depth, 2, gp, 1, PAIR)
    ldt = jnp.broadcast_to(log_dt[..., None], a_re.shape)
    bt = lambda a: jnp.transpose(a.reshape(depth, 2, gp, 2, SSM_STATE, SSM_GROUP),
                                 (0, 1, 2, 5, 3, 4)).reshape(depth, 2, gp, SSM_GROUP, PAIR)
    ct = lambda a: jnp.transpose(a.reshape(depth, 2, gp, 2, SSM_GROUP, SSM_STATE),
                                 (0, 1, 2, 4, 3, 5)).reshape(depth, 2, gp, SSM_GROUP, PAIR)
    return row(a_re), row(a_im), row(ldt), bt(b_re), bt(b_im), ct(c_re), ct(c_im)


def _s5(layer, xg, params, nbatch, nvalid):
    g, nr, _ = xg.shape
    gstep = 2 * S5_NP
    assert S5_NP * nbatch == SUBLANES
    vec = pl.BlockSpec((None, 2, S5_NP, 1, PAIR), lambda i: (layer, 0, i, 0, 0))
    mat = pl.BlockSpec((None, 2, S5_NP, SSM_GROUP, PAIR), lambda i: (layer, 0, i, 0, 0))
    kern = functools.partial(_s5_kernel, nrows=nr // nbatch, nchunk=nvalid // S5_T, nbatch=nbatch,
                             nctx=CTX_LEN // S5_T)
    return pl.pallas_call(
        kern,
        grid=(g // gstep,),
        in_specs=[pl.BlockSpec((gstep, nr, S5_TC), lambda i: (i, 0, 0)),
                  vec, vec, vec, mat, mat, mat, mat],
        out_specs=pl.BlockSpec((gstep, nr, S5_TC), lambda i: (i, 0, 0)),
        out_shape=jax.ShapeDtypeStruct((g, nr, S5_TC), F32),
        scratch_shapes=[pltpu.VMEM((SSM_GROUP, 2 * S5_TC), F32),
                        pltpu.VMEM((S5_TC, S5_TC), BF16),
                        pltpu.VMEM((gstep, S5_TC, PAIR), BF16),
                        pltpu.VMEM((gstep, S5_TC, PAIR), BF16),
                        pltpu.VMEM((nr // nbatch * SUBLANES, PAIR), F32),
                        pltpu.VMEM((nr // nbatch * SUBLANES, PAIR), F32),
                        pltpu.VMEM((nr // nbatch * SUBLANES, PAIR), F32),
                        pltpu.VMEM((nr // nbatch * SUBLANES, PAIR), F32)],
        compiler_params=_cparams(("parallel",)),
        name="s5_scan",
    )(xg, *params)


def _tail_kernel(xl_ref, xc_ref, on_ref, yg_ref, u_ref, gate_ref, mod_ref, d_ref, wglu_ref, bglu_ref,
                 wbs_ref, wba_ref, wout_ref, ng_ref, w1_ref, b1_ref, w2_ref, b2_ref,
                 o_ref, ysc, *, n_lat_tiles):
    pa = jnp.dot(on_ref[...], wba_ref[...], preferred_element_type=F32)
    ys = _chunks_to_rows(yg_ref, ysc) + u_ref[...] * d_ref[...]
    gl = jax.nn.gelu(ys)
    z = gl * _sigmoid(jnp.dot(gl.astype(BF16), wglu_ref[...], preferred_element_type=F32) + bglu_ref[...])
    ps = jnp.dot(z.astype(BF16), wbs_ref[...], preferred_element_type=F32)
    gate = gate_ref[...].astype(F32)
    mix = gate[:, 0:D_MODEL] * pa + gate[:, D_MODEL:2 * D_MODEL] * ps
    m2 = jnp.dot(mix.astype(BF16), wout_ref[...], preferred_element_type=F32)
    x1 =_stream_tile(xl_ref, xc_ref, n_lat_tiles) + mod_ref[:, 2 * D_MODEL:3 * D_MODEL] * _rms(m2, ng_ref[1:2])
    h = _rms(x1, ng_ref[2:3]) * (1.0 + mod_ref[:, 4 * D_MODEL:5 * D_MODEL]) \
        + mod_ref[:, 3 * D_MODEL:4 * D_MODEL]
    hb = h.astype(BF16)
    o = b2_ref[...]
    for c0 in range(0, D_FF, FF_CHUNK):
        f = jnp.dot(hb, w1_ref[:, c0:c0 + FF_CHUNK], preferred_element_type=F32) + b1_ref[:, c0:c0 + FF_CHUNK]
        f = jnp.square(jnp.maximum(f, 0.0))
        o = o + jnp.dot(f.astype(BF16), w2_ref[c0:c0 + FF_CHUNK, :], preferred_element_type=F32)
    o_ref[...] = x1 + mod_ref[:, 5 * D_MODEL:6 * D_MODEL] * _rms(o, ng_ref[3:4])


def _tail(layer, xl, xc, o_n, yg, u, gates, modsel, ssm_d, w_glu, b_glu, w_br_s, w_br_a, w_out, norm_g,
          w1, b1, w2, b2, n_lat_tiles, nt):
    b = xl.shape[0]
    nt_all = n_lat_tiles + 1
    row = lambda bi, i: (bi, i, 0)
    return pl.pallas_call(
        functools.partial(_tail_kernel, n_lat_tiles=n_lat_tiles),
        grid=(b, nt),
        in_specs=_stream_specs(n_lat_tiles) + [
            pl.BlockSpec((None, TM, ATTN_W), row),
            pl.BlockSpec((SSM_GROUPS, TM // S5_T, S5_TC), lambda bi, i: (0, bi * nt_all + i, 0)),
            pl.BlockSpec((None, TM, SSM_W), row),
            pl.BlockSpec((None, TM, 2 * D_MODEL), row),
            _mod_spec(layer, n_lat_tiles),
            _layer_spec(layer, (1, SSM_W)),
            _layer_spec(layer, (SSM_W, SSM_W)),
            _layer_spec(layer, (1, SSM_W)),
            _layer_spec(layer, (SSM_W, D_MODEL)),
            _layer_spec(layer, (ATTN_W, D_MODEL)),
            _layer_spec(layer, (D_MODEL, D_MODEL)),
            _layer_spec(layer, (4, D_MODEL)),
            _layer_spec(layer, (D_MODEL, D_FF)),
            _layer_spec(layer, (1, D_FF)),
            _layer_spec(layer, (D_FF, D_MODEL)),
            _layer_spec(layer, (1, D_MODEL))],
        out_specs=pl.BlockSpec((None, TM, D_MODEL), row),
        out_shape=jax.ShapeDtypeStruct((b, nt * TM, D_MODEL), F32),
        scratch_shapes=[pltpu.VMEM((S5_SLABS, TM, LANES), F32)],
        compiler_params=_cparams(("parallel", "arbitrary")),
        name="merge_mlp",
    )(xl, xc, o_n, yg, u, gates, modsel, ssm_d, w_glu, b_glu, w_br_s, w_br_a, w_out, norm_g,
      w1, b1, w2, b2)


def _rope_tables(n_tokens):
    rows = n_tokens // GRID_W
    row = jnp.repeat(jnp.arange(rows, dtype=jnp.int32), GRID_W).astype(F32)
    col = jnp.tile(jnp.arange(GRID_W, dtype=jnp.int32), rows).astype(F32)
    inv_freq = ROPE_BASE ** (-jnp.arange(N_FREQ, dtype=F32) / N_FREQ)
    ang = jnp.stack([row[:, None] * inv_freq, col[:, None] * inv_freq], axis=1)
    cos, sin = jnp.cos(ang), jnp.sin(ang)
    cos_l = jnp.tile(jnp.concatenate([cos[:, 0], cos[:, 0], cos[:, 1], cos[:, 1]], axis=1), (1, 2))
    sin_l = jnp.tile(jnp.concatenate([-sin[:, 0], sin[:, 0], -sin[:, 1], sin[:, 1]], axis=1), (1, 2))
    cos_t = jnp.concatenate([cos_l, jnp.ones((TM, LANES), F32)], axis=0)
    sin_t = jnp.concatenate([sin_l, jnp.zeros((TM, LANES), F32)], axis=0)
    return cos_t, sin_t


def kernel(x, c, ctx, c_ctx, ada_w, ada_b, norm_g, w_in, gate_b, lam_qk, subln_g, w_br_a,
           ssm_a_re, ssm_a_im, ssm_b_re, ssm_b_im, ssm_c_re, ssm_c_im, ssm_log_dt, ssm_d,
           w_glu, b_glu, w_br_s, w_out, w_mlp1, b_mlp1, w_mlp2, b_mlp2):
    b, seq, _ = x.shape
    nvalid = seq + CTX_LEN
    assert b + 1 <= 8 and ctx.shape[1] == CTX_LEN and seq % TM == 0 and seq % TQ == 0 and nvalid % TK == 0
    n_lat_tiles = seq // TM
    cos_t, sin_t = _rope_tables(seq)
    xl, xc = x, jnp.concatenate([ctx, jnp.zeros((b, ROWS_PAD, D_MODEL), F32)], axis=1)
    cc = jnp.zeros((8, D_MODEL), F32).at[:b].set(c).at[b].set(c_ctx)
    mod = _modulation(cc, ada_w, ada_b)
    modsel = jnp.stack([mod[:, :b], jnp.broadcast_to(mod[:, b:b + 1], (DEPTH, b, 6 * D_MODEL))], axis=2)
    modsel = modsel.reshape(DEPTH, b, 2, 1, 6 * D_MODEL)
    s5_params = _s5_params(ssm_a_re, ssm_a_im, ssm_log_dt, ssm_b_re, ssm_b_im, ssm_c_re, ssm_c_im)
    row3 = lambda a: a.reshape(DEPTH, 1, -1)
    w_in_b, w_glu_b, w_br_s_b, w_br_a_b, w_out_b, w1_b, w2_b = (
        w.astype(BF16) for w in (w_in, w_glu, w_br_s, w_br_a, w_out, w_mlp1, w_mlp2))
    for i in range(DEPTH):
        last = i == DEPTH - 1
        lam_init = 0.8 - 0.6 * math.exp(-0.3 * i)
        q, k, v, u, gates, xg, nrm = _in_proj(i, xl, xc, modsel, norm_g, w_in_b, row3(gate_b),
                                              cos_t, sin_t, n_lat_tiles)
        o_n = _attention_layer(i, q, k, v, nrm, lam_qk, row3(subln_g), lam_init, seq, with_ctx=not last)
        yg = _s5(i, xg, s5_params, b, nvalid)
        xl = _tail(i, xl, xc, o_n, yg, u, gates, modsel, row3(ssm_d), w_glu_b, row3(b_glu), w_br_s_b,
                   w_br_a_b, w_out_b, norm_g, w1_b, row3(b_mlp1), w2_b, row3(b_mlp2),
                   n_lat_tiles, n_lat_tiles if last else n_lat_tiles + 1)
        if not last:
            xc = xl[:, seq:]
    return xl
```

```python
import functools
import math

import jax
import jax.numpy as jnp
from jax import lax
from jax.experimental import pallas as pl
from jax.experimental.pallas import tpu as pltpu

F32 = jnp.float32
BF16 = jnp.bfloat16

D_MODEL = 1024
DEPTH = 2
GRID_W = 64
CTX_LEN = 256
N_HEADS = 4
HEAD_DIM = 64
V_DIM = 128
QK_W = N_HEADS * 2 * HEAD_DIM
ATTN_W = 512
SSM_W = 512
SSM_GROUP = 16
SSM_GROUPS = 32
SSM_STATE = 64
D_FF = 4096
IN_COLS = 4096
N_FREQ = 16
ROPE_BASE = 10000.0
EPS = 1e-6
LOG2E = 1.4426950408889634

LANES = 128
SUBLANES = 8
TM = 512
ROWS_PAD = TM - CTX_LEN
GATHER_ROWS = 16
TQ = 1024
TK = 2816
S5_T = 16
S5_TC = S5_T * SSM_GROUP
S5_SLABS = SSM_W // LANES
S5_GPS = LANES // SSM_GROUP
PAIR = 2 * SSM_STATE
S5_NP = 4
FF_CHUNK = 1024
VMEM_LIMIT = 56 * 1024 * 1024

NEG_BIG = -1e30
BOUND_SLACK = 1.03
BOUND_LIMIT = 50.0

NT_DIMS = (((1,), (1,)), ((), ()))


def _cparams(sem):
    return pltpu.CompilerParams(dimension_semantics=sem, vmem_limit_bytes=VMEM_LIMIT)


def _layer_spec(layer, shape):
    nd = len(shape)
    return pl.BlockSpec((None,) + shape, lambda *_: (layer,) + (0,) * nd, pipeline_mode=pl.Buffered(1))


def _mod_spec(layer, ctx_tile):
    return pl.BlockSpec((None, None, None, 1, 6 * D_MODEL),
                        lambda bi, i: (layer, bi, i // ctx_tile, 0, 0))


def _stream_specs(n_lat_tiles):
    return [pl.BlockSpec((None, TM, D_MODEL), lambda bi, i: (bi, jnp.minimum(i, n_lat_tiles - 1), 0)),
            pl.BlockSpec((None, TM, D_MODEL), lambda bi, i: (bi, 0, 0))]


def _stream_tile(xl_ref, xc_ref, n_lat_tiles, tile_axis=1):
    return jnp.where(pl.program_id(tile_axis) >= n_lat_tiles, xc_ref[...], xl_ref[...])


def _rms(x, g):
    ms = jnp.mean(x * x, axis=-1, keepdims=True)
    return x * lax.rsqrt(ms + EPS) * g


def _sigmoid(x):
    return 0.5 * jnp.tanh(0.5 * x) + 0.5


def _mod_kernel(c_ref, w_ref, b_ref, o_ref):
    c = c_ref[...]
    s = c * jax.nn.sigmoid(c)
    o_ref[...] = jnp.dot(s.astype(BF16), w_ref[...].astype(BF16),
                         preferred_element_type=F32) + b_ref[...]


def _modulation(cc, ada_w, ada_b):
    depth, _, n = ada_w.shape
    bn = 1024
    return pl.pallas_call(
        _mod_kernel,
        grid=(depth, n // bn),
        in_specs=[pl.BlockSpec((8, D_MODEL), lambda l, j: (0, 0)),
                  pl.BlockSpec((None, D_MODEL, bn), lambda l, j: (l, 0, j)),
                  pl.BlockSpec((None, 1, bn), lambda l, j: (l, 0, j))],
        out_specs=pl.BlockSpec((None, 8, bn), lambda l, j: (l, 0, j)),
        out_shape=jax.ShapeDtypeStruct((depth, 8, n), F32),
        compiler_params=_cparams(("arbitrary", "arbitrary")),
        name="adaln_mod",
    )(cc, ada_w, ada_b.reshape(depth, 1, n))


def _lane_window(off):
    lane = lax.broadcasted_iota(jnp.int32, (GATHER_ROWS, LANES), 1)
    return (lane >= off) & (lane < off + SSM_GROUP)


def _rows_to_chunks(u, usc, xg_ref):
    for sl in range(S5_SLABS):
        usc[sl] = u[:, sl * LANES:(sl + 1) * LANES]
    for part in range(TM // (S5_T * GATHER_ROWS)):
        r0 = part * GATHER_ROWS
        for sl in range(S5_SLABS):
            acc = [[None] * (S5_TC // LANES) for _ in range(S5_GPS)]
            for t in range(S5_T):
                v = usc[sl, pl.ds(r0 * S5_T + t, GATHER_ROWS, stride=S5_T), :]
                lt, off = divmod(t * SSM_GROUP, LANES)
                win = _lane_window(off)
                for gl in range(S5_GPS):
                    shift = (off - gl * SSM_GROUP) % LANES
                    moved = pltpu.roll(v, shift, 1) if shift else v
                    prev = acc[gl][lt]
                    acc[gl][lt] = jnp.where(win, moved, 0.0 if prev is None else prev)
            for gl in range(S5_GPS):
                for lt in range(S5_TC // LANES):
                    xg_ref[sl * S5_GPS + gl, r0:r0 + GATHER_ROWS, lt * LANES:(lt + 1) * LANES] = (
                        acc[gl][lt].astype(xg_ref.dtype))


def _chunks_to_rows(yg_ref, ysc):
    for part in range(TM // (S5_T * GATHER_ROWS)):
        r0 = part * GATHER_ROWS
        for sl in range(S5_SLABS):
            tiles = [[yg_ref[sl * S5_GPS + gl, r0:r0 + GATHER_ROWS, lt * LANES:(lt + 1) * LANES]
                      for lt in range(S5_TC // LANES)] for gl in range(S5_GPS)]
            for t in range(S5_T):
                lt, off = divmod(t * SSM_GROUP, LANES)
                v = None
                for gl in range(S5_GPS):
                    shift = (gl * SSM_GROUP - off) % LANES
                    src = tiles[gl][lt]
                    moved = pltpu.roll(src, shift, 1) if shift else src
                    v = jnp.where(_lane_window(gl * SSM_GROUP), moved, 0.0 if v is None else v)
                ysc[sl, pl.ds(r0 * S5_T + t, GATHER_ROWS, stride=S5_T), :] = v
    return jnp.concatenate([ysc[sl] for sl in range(S5_SLABS)], axis=1)


def _map_segments():
    li = lax.broadcasted_iota(jnp.int32, (LANES, LANES), 0) // HEAD_DIM
    lj = lax.broadcasted_iota(jnp.int32, (LANES, LANES), 1) // HEAD_DIM
    return (li == lj).astype(BF16)


def _in_kernel(xl_ref, xc_ref, mod_ref, g_ref, w_ref, gb_ref, cos_ref, sin_ref,
               q_ref, k_ref, v_ref, u_ref, gate_ref, xg_ref, nrm_ref, usc, *, n_lat_tiles):
    for bi in range(xl_ref.shape[0]):
        _in_rows(_stream_tile(xl_ref.at[bi], xc_ref.at[bi], n_lat_tiles, tile_axis=0),
                 mod_ref.at[bi], g_ref, w_ref, gb_ref, cos_ref, sin_ref,
                 q_ref.at[bi], k_ref.at[bi], v_ref.at[bi], u_ref.at[bi], gate_ref.at[bi],
                 xg_ref.at[:, bi], nrm_ref.at[bi], usc.at[bi])


def _in_rows(x, mod_ref, g_ref, w_ref, gb_ref, cos_ref, sin_ref,
             q_ref, k_ref, v_ref, u_ref, gate_ref, xg_ref, nrm_ref, usc):
    h = _rms(x, g_ref[0:1]) * (1.0 + mod_ref[:, D_MODEL:2 * D_MODEL]) + mod_ref[:, 0:D_MODEL]
    hb = h.astype(BF16)
    cos = cos_ref[...]
    sin = sin_ref[...]
    lane = lax.broadcasted_iota(jnp.int32, (TM, LANES), 1)
    first_half = (lane & 31) < 16
    sel = (lax.broadcasted_iota(jnp.int32, (8, LANES), 0)
           == lax.broadcasted_iota(jnp.int32, (8, LANES), 1) // HEAD_DIM).astype(BF16)
    slot = lax.broadcasted_iota(jnp.int32, (8, LANES), 1)
    nrm = jnp.zeros((8, LANES), F32)

    def rope_store(col0, out_ref, scale, slot0, nrm):
        t = jnp.dot(hb, w_ref[:, col0:col0 + QK_W], preferred_element_type=F32)
        for j in range(QK_W // LANES):
            tj = t[:, j * LANES:(j + 1) * LANES]
            partner = jnp.where(first_half, pltpu.roll(tj, LANES - 16, 1), pltpu.roll(tj, 16, 1))
            r = tj * cos + partner * sin
            if scale != 1.0:
                r = r * scale
            out_ref[:, j * LANES:(j + 1) * LANES] = r.astype(out_ref.dtype)
            n2 = lax.dot_general(sel, (r * r).astype(BF16), NT_DIMS, preferred_element_type=F32)
            nrm = jnp.where(slot == slot0 + j, jnp.max(n2, axis=1, keepdims=True), nrm)
        return nrm

    nrm = rope_store(0, q_ref, HEAD_DIM ** -0.5 * LOG2E, 0, nrm)
    nrm = rope_store(QK_W, k_ref, 1.0, N_HEADS, nrm)
    nrm_ref[...] = nrm
    v_ref[...] = jnp.dot(hb, w_ref[:, 1024:1536], preferred_element_type=F32).astype(v_ref.dtype)
    u = jnp.dot(hb, w_ref[:, 1536:2048], preferred_element_type=F32)
    u_ref[...] = u
    _rows_to_chunks(u, usc, xg_ref)
    g = jnp.dot(hb, w_ref[:, 2048:IN_COLS], preferred_element_type=F32) + gb_ref[...]
    gate_ref[...] = _sigmoid(g).astype(gate_ref.dtype)


def _in_proj(layer, xl, xc, modsel, norm_g, w_in_b, gate_b, cos_t, sin_t, n_lat_tiles):
    b = xl.shape[0]
    nt = n_lat_tiles + 1
    n = nt * TM
    row = lambda i: (0, i, 0)
    outs = pl.pallas_call(
        functools.partial(_in_kernel, n_lat_tiles=n_lat_tiles),
        grid=(nt,),
        in_specs=[pl.BlockSpec((b, TM, D_MODEL), lambda i: (0, jnp.minimum(i, n_lat_tiles - 1), 0)),
                  pl.BlockSpec((b, TM, D_MODEL), lambda i: (0, 0, 0)),
                  pl.BlockSpec((None, b, None, 1, 6 * D_MODEL), lambda i: (layer, 0, i // n_lat_tiles, 0, 0)),
                  _layer_spec(layer, (4, D_MODEL)),
                  _layer_spec(layer, (D_MODEL, IN_COLS)),
                  _layer_spec(layer, (1, 2 * D_MODEL)),
                  pl.BlockSpec((TM, LANES), lambda i: (i, 0)),
                  pl.BlockSpec((TM, LANES), lambda i: (i, 0))],
        out_specs=[pl.BlockSpec((b, TM, QK_W), row),
                   pl.BlockSpec((b, TM, QK_W), row),
                   pl.BlockSpec((b, TM, ATTN_W), row),
                   pl.BlockSpec((b, TM, SSM_W), row),
                   pl.BlockSpec((b, TM, 2 * D_MODEL), row),
                   pl.BlockSpec((SSM_GROUPS, b, TM // S5_T, S5_TC), lambda i: (0, 0, i, 0)),
                   pl.BlockSpec((b, None, 8, LANES), lambda i: (0, i, 0, 0))],
        out_shape=[jax.ShapeDtypeStruct((b, n, QK_W), BF16),
                   jax.ShapeDtypeStruct((b, n, QK_W), BF16),
                   jax.ShapeDtypeStruct((b, n, ATTN_W), BF16),
                   jax.ShapeDtypeStruct((b, n, SSM_W), F32),
                   jax.ShapeDtypeStruct((b, n, 2 * D_MODEL), BF16),
                   jax.ShapeDtypeStruct((SSM_GROUPS, b, n // S5_T, S5_TC), BF16),
                   jax.ShapeDtypeStruct((b, nt, 8, LANES), F32)],
        scratch_shapes=[pltpu.VMEM((b, S5_SLABS, TM, LANES), F32)],
        compiler_params=_cparams(("arbitrary",)),
        name="in_proj",
    )(xl, xc, modsel, norm_g, w_in_b, gate_b, cos_t, sin_t)
    q, k, v, u, gates, xg, nrm = outs
    return q, k, v, u, gates, xg.reshape(SSM_GROUPS, b * n // S5_T, S5_TC), nrm


def _attn_kernel(lamqk_ref, subg_ref, q_ref, k_ref, v_ref, o_ref,
                 qm_sc, m_sc, l_sc, acc_sc, *, lam_init, nk):
    ki = pl.program_id(3)

    @pl.when(ki == 0)
    def _init():
        q = q_ref[...]
        lane = lax.broadcasted_iota(jnp.int32, q.shape, 1)
        zero = jnp.zeros_like(q)
        qm_sc[0] = jnp.where(lane < HEAD_DIM, q, zero)
        qm_sc[1] = jnp.where(lane >= HEAD_DIM, q, zero)
        m_sc[...] = jnp.full(m_sc.shape, NEG_BIG, F32)
        l_sc[...] = jnp.zeros(l_sc.shape, F32)
        acc_sc[...] = jnp.zeros(acc_sc.shape, F32)

    k = k_ref[...]
    v = v_ref[...]
    for c in range(2):
        s = lax.dot_general(qm_sc[c], k, NT_DIMS, preferred_element_type=F32)
        m_prev = m_sc[c]
        m_new = jnp.maximum(m_prev, jnp.max(s, axis=1, keepdims=True))
        alpha = jnp.exp2(m_prev - m_new)
        p = jnp.exp2(s - m_new)
        l_sc[c] = alpha * l_sc[c] + jnp.sum(p, axis=1, keepdims=True)
        acc_sc[c] = alpha * acc_sc[c] + jnp.dot(p.astype(BF16), v, preferred_element_type=F32)
        m_sc[c] = m_new

    @pl.when(ki == nk - 1)
    def _fin():
        lq = lamqk_ref[...]
        lam = (jnp.exp(jnp.sum(lq[0:1] * lq[1:2], axis=1, keepdims=True))
               - jnp.exp(jnp.sum(lq[2:3] * lq[3:4], axis=1, keepdims=True)) + lam_init)
        o = acc_sc[0] / l_sc[0] - lam * (acc_sc[1] / l_sc[1])
        o_ref[...] = (_rms(o, subg_ref[...]) * (1.0 - lam_init)).astype(o_ref.dtype)


def _attn_bounded_kernel(lamqk_ref, subg_ref, kmax_ref, q_ref, k_ref, v_ref, o_ref,
                         qx_sc, kx_sc, l_sc, acc_sc, *, lam_init, nk):
    ki = pl.program_id(3)
    tq = q_ref.shape[0]
    tk = k_ref.shape[0]

    @pl.when(ki == 0)
    def _init():
        q = q_ref[...]
        lane = lax.broadcasted_iota(jnp.int32, (tq, LANES), 1)
        qf = q.astype(F32)
        n2 = jnp.dot((qf * qf).astype(BF16), _map_segments(), preferred_element_type=F32)
        bound = jnp.sqrt(n2) * kmax_ref[...] * BOUND_SLACK
        zero = jnp.zeros_like(q)
        qx_sc[0, :, 0:LANES] = jnp.where(lane < HEAD_DIM, q, zero)
        qx_sc[0, :, LANES:2 * LANES] = jnp.where(lane == 0, -bound, 0.0).astype(BF16)
        qx_sc[1, :, 0:LANES] = jnp.where(lane >= HEAD_DIM, q, zero)
        qx_sc[1, :, LANES:2 * LANES] = jnp.where(lane == 0, -pltpu.roll(bound, HEAD_DIM, 1), 0.0).astype(BF16)
        lane_k = lax.broadcasted_iota(jnp.int32, (tk, LANES), 1)
        kx_sc[:, LANES:2 * LANES] = jnp.where(lane_k == 0, 1.0, 0.0).astype(BF16)
        l_sc[...] = jnp.zeros(l_sc.shape, F32)
        acc_sc[...] = jnp.zeros(acc_sc.shape, F32)

    kx_sc[:, 0:LANES] = k_ref[...]
    kx = kx_sc[...]
    v = v_ref[...]
    for c in range(2):
        s = lax.dot_general(qx_sc[c], kx, NT_DIMS, preferred_element_type=F32)
        p = jnp.exp2(s)
        part = p[:, 0:LANES]
        for j in range(1, tk // LANES):
            part = part + p[:, j * LANES:(j + 1) * LANES]
        l_sc[c] += part
        acc_sc[c] += jnp.dot(p.astype(BF16), v, preferred_element_type=F32)

    @pl.when(ki == nk - 1)
    def _fin():
        lq = lamqk_ref[...]
        lam = (jnp.exp(jnp.sum(lq[0:1] * lq[1:2], axis=1, keepdims=True))
               - jnp.exp(jnp.sum(lq[2:3] * lq[3:4], axis=1, keepdims=True)) + lam_init)
        l0 = jnp.sum(l_sc[0], axis=1, keepdims=True)
        l1 = jnp.sum(l_sc[1], axis=1, keepdims=True)
        o = acc_sc[0] / l0 - lam * (acc_sc[1] / l1)
        o_ref[...] = (_rms(o, subg_ref[...]) * (1.0 - lam_init)).astype(o_ref.dtype)


def _drop_input(kern, idx, *refs):
    return kern(*refs[:idx], *refs[idx + 1:])


def _attention(layer, q, k, v, kmax, lam_qk, subln_g, lam_init, *, bounded, tq, tk, q0, nq, k0, nk,
               o_prev=None):
    b, n, _ = q.shape
    q_idx = lambda bi, h, qi, ki: (bi, qi + q0, h)
    kv_idx = lambda bi, h, qi, ki: (bi, ki + k0, h)
    in_specs = [pl.BlockSpec((None, 4, HEAD_DIM), lambda *_: (layer, 0, 0)),
                pl.BlockSpec((None, 1, V_DIM), lambda *_: (layer, 0, 0)),
                pl.BlockSpec((None, None, 1, LANES), lambda bi, h, qi, ki: (bi, h, 0, 0)),
                pl.BlockSpec((None, tq, LANES), q_idx),
                pl.BlockSpec((None, tk, LANES), kv_idx),
                pl.BlockSpec((None, tk, LANES), kv_idx)]
    args = [lam_qk, subln_g, kmax, q, k, v]
    if bounded:
        kern = functools.partial(_attn_bounded_kernel, lam_init=lam_init, nk=nk)
        scratch = [pltpu.VMEM((2, tq, 2 * LANES), BF16),
                   pltpu.VMEM((tk, 2 * LANES), BF16),
                   pltpu.VMEM((2, tq, LANES), F32),
                   pltpu.VMEM((2, tq, V_DIM), F32)]
    else:
        kern = functools.partial(_drop_input, functools.partial(_attn_kernel, lam_init=lam_init, nk=nk), 2)
        scratch = [pltpu.VMEM((2, tq, LANES), BF16),
                   pltpu.VMEM((2, tq, 1), F32),
                   pltpu.VMEM((2, tq, 1), F32),
                   pltpu.VMEM((2, tq, V_DIM), F32)]
    aliases = {}
    if o_prev is not None:
        in_specs.append(pl.BlockSpec(memory_space=pl.ANY))
        args.append(o_prev)
        aliases = {6: 0}
        kern = functools.partial(_drop_input, kern, 6)
    return pl.pallas_call(
        kern,
        grid=(b, N_HEADS, nq, nk),
        in_specs=in_specs,
        out_specs=pl.BlockSpec((None, tq, LANES), q_idx),
        out_shape=jax.ShapeDtypeStruct((b, n, ATTN_W), BF16),
        scratch_shapes=scratch,
        input_output_aliases=aliases,
        compiler_params=_cparams(("parallel", "parallel", "parallel", "arbitrary")),
        name="diff_attn_bounded" if bounded else "diff_attn",
    )(*args)


def _attn_pair_kernel(lamqk_ref, subg_ref, kmax_ref, q_ref, k_ref, v_ref, o_ref, *, lam_init, tk):
    tq = q_ref.shape[0]
    nkeys = k_ref.shape[0]
    lane = lax.broadcasted_iota(jnp.int32, (tq, LANES), 1)
    lane_k = lax.broadcasted_iota(jnp.int32, (tk, LANES), 1)
    one_tile = jnp.where(lane_k == 0, 1.0, 0.0).astype(BF16)
    lq = lamqk_ref[...]
    lam = (jnp.exp(jnp.sum(lq[0:1] * lq[1:2], axis=1, keepdims=True))
           - jnp.exp(jnp.sum(lq[2:3] * lq[3:4], axis=1, keepdims=True)) + lam_init)
    for hh in range(2):
        cols = slice(hh * LANES, (hh + 1) * LANES)
        q = q_ref[:, cols]
        qf = q.astype(F32)
        n2 = jnp.dot((qf * qf).astype(BF16), _map_segments(), preferred_element_type=F32)
        bound = jnp.sqrt(n2) * kmax_ref[hh] * BOUND_SLACK
        zero = jnp.zeros_like(q)
        qx = [jnp.concatenate([jnp.where(lane < HEAD_DIM, q, zero),
                               jnp.where(lane == 0, -bound, 0.0).astype(BF16)], axis=1),
              jnp.concatenate([jnp.where(lane >= HEAD_DIM, q, zero),
                               jnp.where(lane == 0, -pltpu.roll(bound, HEAD_DIM, 1), 0.0).astype(BF16)],
                              axis=1)]
        l = [jnp.zeros((tq, LANES), F32)] * 2
        acc = [jnp.zeros((tq, V_DIM), F32)] * 2
        for kb in range(nkeys // tk):
            rows = slice(kb * tk, (kb + 1) * tk)
            kx = jnp.concatenate([k_ref[rows, cols], one_tile], axis=1)
            v = v_ref[rows, cols]
            for c in range(2):
                s = lax.dot_general(qx[c], kx, NT_DIMS, preferred_element_type=F32)
                p = jnp.exp2(s)
                part = p[:, 0:LANES]
                for j in range(1, tk // LANES):
                    part = part + p[:, j * LANES:(j + 1) * LANES]
                l[c] = l[c] + part
                acc[c] = acc[c] + jnp.dot(p.astype(BF16), v, preferred_element_type=F32)
        l0 = jnp.sum(l[0], axis=1, keepdims=True)
        l1 = jnp.sum(l[1], axis=1, keepdims=True)
        o = acc[0] / l0 - lam * (acc[1] / l1)
        o_ref[:, cols] = (_rms(o, subg_ref[...]) * (1.0 - lam_init)).astype(o_ref.dtype)


def _attention_pairs(layer, q, k, v, kmax, lam_qk, subln_g, lam_init, *, tq, tk, nq, nkeys):
    b, n, _ = q.shape
    q_idx = lambda bi, hp, qi: (bi, qi, hp)
    kv_idx = lambda bi, hp, qi: (bi, 0, hp)
    return pl.pallas_call(
        functools.partial(_attn_pair_kernel, lam_init=lam_init, tk=tk),
        grid=(b, N_HEADS // 2, nq),
        in_specs=[pl.BlockSpec((None, 4, HEAD_DIM), lambda *_: (layer, 0, 0)),
                  pl.BlockSpec((None, 1, V_DIM), lambda *_: (layer, 0, 0)),
                  pl.BlockSpec((None, 2, 1, LANES), lambda bi, hp, qi: (bi, hp, 0, 0)),
                  pl.BlockSpec((None, tq, 2 * LANES), q_idx),
                  pl.BlockSpec((None, nkeys, 2 * LANES), kv_idx, pipeline_mode=pl.Buffered(1)),
                  pl.BlockSpec((None, nkeys, 2 * LANES), kv_idx, pipeline_mode=pl.Buffered(1))],
        out_specs=pl.BlockSpec((None, tq, 2 * LANES), q_idx),
        out_shape=jax.ShapeDtypeStruct((b, n, ATTN_W), BF16),
        compiler_params=_cparams(("parallel", "parallel", "arbitrary")),
        name="diff_attn_pairs",
    )(lam_qk, subln_g, kmax, q, k, v)


def _attention_layer(layer, q, k, v, nrm, lam_qk, subln_g, lam_init, seq, with_ctx):
    b = q.shape[0]
    nkeys = seq + CTX_LEN
    top = jnp.sqrt(jnp.max(nrm[:, :, 0:2, 0:2 * N_HEADS], axis=1))
    qmax, kmax = top[:, :, 0:N_HEADS], top[:, :, N_HEADS:]
    small = jnp.max(qmax * kmax) * BOUND_SLACK < BOUND_LIMIT
    kmax4 = jnp.repeat(jnp.swapaxes(kmax, 1, 2), HEAD_DIM, axis=-1).reshape(b, N_HEADS, 1, LANES)

    def run(bounded):
        if bounded:
            o_n = _attention_pairs(layer, q, k, v, kmax4, lam_qk, subln_g, lam_init,
                                   tq=TQ, tk=TK, nq=seq // TQ, nkeys=nkeys)
        else:
            o_n = _attention(layer, q, k, v, kmax4, lam_qk, subln_g, lam_init, bounded=False,
                             tq=TQ, tk=TK, q0=0, nq=seq // TQ, k0=0, nk=nkeys // TK)
        if with_ctx:
            o_n = _attention(layer, q, k, v, kmax4, lam_qk, subln_g, lam_init, bounded=bounded,
                             tq=TM, tk=CTX_LEN, q0=seq // TM, nq=1, k0=seq // CTX_LEN, nk=1,
                             o_prev=o_n)
        return o_n

    return lax.cond(small, lambda: run(True), lambda: run(False))


def _cmul(ar, ai, br, bi):
    return ar * br - ai * bi, ar * bi + ai * br


def _cpow_by_bits(lr, li, expo, nbits):
    pr = jnp.ones(expo.shape, F32)
    pi = jnp.zeros(expo.shape, F32)
    br, bi = lr, li
    for bit in range(nbits):
        on = ((expo >> bit) & 1) == 1
        fr = jnp.where(on, br, 1.0)
        fi = jnp.where(on, bi, 0.0)
        pr, pi = _cmul(pr, pi, fr, fi)
        if bit + 1 < nbits:
            br, bi = _cmul(br, bi, br, bi)
    return pr, pi


def _discretize(ar, ai, log_dt):
    dt = jnp.exp(log_dt)
    er = jnp.exp(ar * dt)
    lbr = er * jnp.cos(ai * dt)
    lbi = er * jnp.sin(ai * dt)
    den = ar * ar + ai * ai
    nr = lbr - 1.0
    cr = (nr * ar + lbi * ai) / den
    ci = (lbi * ar - nr * ai) / den
    return lbr, lbi, cr, ci


def _s5_kernel(x_ref, ar_ref, ai_ref, ldt_ref, btr_ref, bti_ref, ctr_ref, cti_ref,
               y_ref, sext, toep, cs_sc, zre, zim, hre, him, *, nrows, nchunk, nbatch, nctx):
    nlat = nchunk - nctx
    blk = lax.broadcasted_iota(jnp.int32, (S5_TC, PAIR), 0) // SSM_GROUP
    lane_g = lax.broadcasted_iota(jnp.int32, (1, PAIR), 1) // SSM_STATE

    for d in range(2):
        fwd = d == 0
        step_r, step_i = [], []
        for pi in range(S5_NP):
            lbr, lbi, cr, ci = _discretize(ar_ref[d, pi], ai_ref[d, pi], ldt_ref[d, pi])
            bbr, bbi = _cmul(cr, ci, btr_ref[d, pi], bti_ref[d, pi])
            bbr_t = jnp.concatenate([bbr] * S5_T, axis=0)
            bbi_t = jnp.concatenate([bbi] * S5_T, axis=0)
            ctr_t = jnp.concatenate([ctr_ref[d, pi]] * S5_T, axis=0)
            cti_t = jnp.concatenate([cti_ref[d, pi]] * S5_T, axis=0)
            upr, upi = _cpow_by_bits(lbr, lbi, blk, 4)
            dnr, dni = _cpow_by_bits(lbr, lbi, S5_T - 1 - blk, 4)
            (per, pei), (pwr, pwi) = ((upr, upi), (dnr, dni)) if fwd else ((dnr, dni), (upr, upi))
            bsr, bsi = _cmul(pwr, pwi, bbr_t, bbi_t)
            wcr, wci = _cmul(per, pei, ctr_t, cti_t)
            c1r, c1i = _cmul(wcr, wci, lbr, lbi)
            tr, ti = lbr, lbi
            for _ in range(int(math.log2(S5_T))):
                tr, ti = _cmul(tr, ti, tr, ti)
            step_r.append(tr)
            step_i.append(ti)

            z = None
            for gg in range(2):
                g = 2 * pi + gg
                lm = lane_g == gg
                strip = (lax.dot_general(jnp.where(lm, bbr, 0.0), wcr, NT_DIMS, preferred_element_type=F32,
                                         precision=lax.Precision.HIGHEST)
                         - lax.dot_general(jnp.where(lm, bbi, 0.0), wci, NT_DIMS, preferred_element_type=F32,
                                           precision=lax.Precision.HIGHEST))
                if fwd:
                    sext[:, 0:S5_TC] = jnp.zeros((SSM_GROUP, S5_TC), F32)
                    sext[:, S5_TC:2 * S5_TC] = strip
                    for j in range(S5_T):
                        lo = S5_TC - SSM_GROUP * j
                        toep[j * SSM_GROUP:(j + 1) * SSM_GROUP, :] = sext[:, lo:lo + S5_TC].astype(BF16)
                else:
                    sext[:, 0:S5_TC] = strip
                    sext[:, S5_TC:2 * S5_TC] = jnp.zeros((SSM_GROUP, S5_TC), F32)
                    for j in range(S5_T):
                        lo = SSM_GROUP * (S5_T - 1 - j)
                        toep[j * SSM_GROUP:(j + 1) * SSM_GROUP, :] = sext[:, lo:lo + S5_TC].astype(BF16)
                x = x_ref[g]
                yi = jnp.dot(x, toep[...], preferred_element_type=F32)
                if fwd:
                    y_ref[g] = yi
                else:
                    y_ref[g] += yi
                bs = jnp.concatenate([jnp.where(lm, bsr, 0.0), jnp.where(lm, bsi, 0.0)], axis=1)
                zg = jnp.dot(x, bs.astype(BF16), preferred_element_type=F32)
                z = zg if z is None else z + zg
                cs_sc[g] = jnp.concatenate([jnp.where(lm, c1r, 0.0), jnp.where(lm, c1i, 0.0)],
                                           axis=1).astype(BF16)
            for b in range(nbatch):
                s = pi * nbatch + b
                zre[pl.ds(s, nrows, stride=SUBLANES), :] = z[b * nrows:(b + 1) * nrows, 0:PAIR]
                zim[pl.ds(s, nrows, stride=SUBLANES), :] = z[b * nrows:(b + 1) * nrows, PAIR:2 * PAIR]

        sub = lax.broadcasted_iota(jnp.int32, (SUBLANES, PAIR), 0) // nbatch
        tr = jnp.zeros((SUBLANES, PAIR), F32)
        ti = jnp.zeros((SUBLANES, PAIR), F32)
        for pi in range(S5_NP):
            tr = jnp.where(sub == pi, step_r[pi], tr)
            ti = jnp.where(sub == pi, step_i[pi], ti)
        hre[...] = jnp.zeros(hre.shape, F32)
        him[...] = jnp.zeros(him.shape, F32)

        def body(n, carry):
            if fwd:
                ch = jnp.where(n < nctx, nlat + n, n - nctx)
            else:
                ch = nchunk - 1 - n
            h_r, h_i = carry
            r = pl.multiple_of(ch * SUBLANES, SUBLANES)
            hre[pl.ds(r, SUBLANES), :] = h_r
            him[pl.ds(r, SUBLANES), :] = h_i
            n_r, n_i = _cmul(tr, ti, h_r, h_i)
            return n_r + zre[pl.ds(r, SUBLANES), :], n_i + zim[pl.ds(r, SUBLANES), :]

        zero = jnp.zeros((SUBLANES, PAIR), F32)
        lax.fori_loop(0, nchunk, body, (zero, zero), unroll=8)

        for pi in range(S5_NP):
            for b in range(nbatch):
                s = pi * nbatch + b
                h_r = hre[pl.ds(s, nrows, stride=SUBLANES), :].astype(BF16)
                h_i = him[pl.ds(s, nrows, stride=SUBLANES), :].astype(BF16)
                for gg in range(2):
                    g = 2 * pi + gg
                    y_ref[g, b * nrows:(b + 1) * nrows, :] += (
                        lax.dot_general(h_r, cs_sc[g, :, 0:PAIR], NT_DIMS, preferred_element_type=F32)
                        - lax.dot_general(h_i, cs_sc[g, :, PAIR:2 * PAIR], NT_DIMS, preferred_element_type=F32))


def _s5_params(a_re, a_im, log_dt, b_re, b_im, c_re, c_im):
    depth = a_re.shape[0]
    gp = SSM_GROUPS // 2
    row = lambda a: a.reshape(depth, 2, gp, 1, PAIR)
    ldt = jnp.broadcast_to(log_dt[..., None], a_re.shape)
    bt = lambda a: jnp.transpose(a.reshape(depth, 2, gp, 2, SSM_STATE, SSM_GROUP),
                                 (0, 1, 2, 5, 3, 4)).reshape(depth, 2, gp, SSM_GROUP, PAIR)
    ct = lambda a: jnp.transpose(a.reshape(depth, 2, gp, 2, SSM_GROUP, SSM_STATE),
                                 (0, 1, 2, 4, 3, 5)).reshape(depth, 2, gp, SSM_GROUP, PAIR)
    return row(a_re), row(a_im), row(ldt), bt(b_re), bt(b_im), ct(c_re), ct(c_im)


def _s5(layer, xg, params, nbatch, nvalid):
    g, nr, _ = xg.shape
    gstep = 2 * S5_NP
    assert S5_NP * nbatch == SUBLANES
    vec = pl.BlockSpec((None, 2, S5_NP, 1, PAIR), lambda i: (layer, 0, i, 0, 0))
    mat = pl.BlockSpec((None, 2, S5_NP, SSM_GROUP, PAIR), lambda i: (layer, 0, i, 0, 0))
    kern = functools.partial(_s5_kernel, nrows=nr // nbatch, nchunk=nvalid // S5_T, nbatch=nbatch,
                             nctx=CTX_LEN // S5_T)
    return pl.pallas_call(
        kern,
        grid=(g // gstep,),
        in_specs=[pl.BlockSpec((gstep, nr, S5_TC), lambda i: (i, 0, 0)),
                  vec, vec, vec, mat, mat, mat, mat],
        out_specs=pl.BlockSpec((gstep, nr, S5_TC), lambda i: (i, 0, 0)),
        out_shape=jax.ShapeDtypeStruct((g, nr, S5_TC), F32),
        scratch_shapes=[pltpu.VMEM((SSM_GROUP, 2 * S5_TC), F32),
                        pltpu.VMEM((S5_TC, S5_TC), BF16),
                        pltpu.VMEM((gstep, S5_TC, 2 * PAIR), BF16),
                        pltpu.VMEM((nr // nbatch * SUBLANES, PAIR), F32),
                        pltpu.VMEM((nr // nbatch * SUBLANES, PAIR), F32),
                        pltpu.VMEM((nr // nbatch * SUBLANES, PAIR), F32),
                        pltpu.VMEM((nr // nbatch * SUBLANES, PAIR), F32)],
        compiler_params=_cparams(("parallel",)),
        name="s5_scan",
    )(xg, *params)


def _tail_kernel(xl_ref, xc_ref, on_ref, yg_ref, u_ref, gate_ref, mod_ref, d_ref, wglu_ref, bglu_ref,
                 wbs_ref, wba_ref, wout_ref, ng_ref, w1_ref, b1_ref, w2_ref, b2_ref,
                 o_ref, ysc, *, n_lat_tiles):
    pa = jnp.dot(on_ref[...], wba_ref[...], preferred_element_type=F32)
    ys = _chunks_to_rows(yg_ref, ysc) + u_ref[...] * d_ref[...]
    gl = jax.nn.gelu(ys)
    z = gl * _sigmoid(jnp.dot(gl.astype(BF16), wglu_ref[...], preferred_element_type=F32) + bglu_ref[...])
    ps = jnp.dot(z.astype(BF16), wbs_ref[...], preferred_element_type=F32)
    gate = gate_ref[...].astype(F32)
    mix = gate[:, 0:D_MODEL] * pa + gate[:, D_MODEL:2 * D_MODEL] * ps
    m2 = jnp.dot(mix.astype(BF16), wout_ref[...], preferred_element_type=F32)
    x1 =_stream_tile(xl_ref, xc_ref, n_lat_tiles) + mod_ref[:, 2 * D_MODEL:3 * D_MODEL] * _rms(m2, ng_ref[1:2])
    h = _rms(x1, ng_ref[2:3]) * (1.0 + mod_ref[:, 4 * D_MODEL:5 * D_MODEL]) \
        + mod_ref[:, 3 * D_MODEL:4 * D_MODEL]
    hb = h.astype(BF16)
    o = b2_ref[...]
    for c0 in range(0, D_FF, FF_CHUNK):
        f = jnp.dot(hb, w1_ref[:, c0:c0 + FF_CHUNK], preferred_element_type=F32) + b1_ref[:, c0:c0 + FF_CHUNK]
        f = jnp.square(jnp.maximum(f, 0.0))
        o = o + jnp.dot(f.astype(BF16), w2_ref[c0:c0 + FF_CHUNK, :], preferred_element_type=F32)
    o_ref[...] = x1 + mod_ref[:, 5 * D_MODEL:6 * D_MODEL] * _rms(o, ng_ref[3:4])


def _tail(layer, xl, xc, o_n, yg, u, gates, modsel, ssm_d, w_glu, b_glu, w_br_s, w_br_a, w_out, norm_g,
          w1, b1, w2, b2, n_lat_tiles, nt):
    b = xl.shape[0]
    nt_all = n_lat_tiles + 1
    row = lambda bi, i: (bi, i, 0)
    return pl.pallas_call(
        functools.partial(_tail_kernel, n_lat_tiles=n_lat_tiles),
        grid=(b, nt),
        in_specs=_stream_specs(n_lat_tiles) + [
            pl.BlockSpec((None, TM, ATTN_W), row),
            pl.BlockSpec((SSM_GROUPS, TM // S5_T, S5_TC), lambda bi, i: (0, bi * nt_all + i, 0)),
            pl.BlockSpec((None, TM, SSM_W), row),
            pl.BlockSpec((None, TM, 2 * D_MODEL), row),
            _mod_spec(layer, n_lat_tiles),
            _layer_spec(layer, (1, SSM_W)),
            _layer_spec(layer, (SSM_W, SSM_W)),
            _layer_spec(layer, (1, SSM_W)),
            _layer_spec(layer, (SSM_W, D_MODEL)),
            _layer_spec(layer, (ATTN_W, D_MODEL)),
            _layer_spec(layer, (D_MODEL, D_MODEL)),
            _layer_spec(layer, (4, D_MODEL)),
            _layer_spec(layer, (D_MODEL, D_FF)),
            _layer_spec(layer, (1, D_FF)),
            _layer_spec(layer, (D_FF, D_MODEL)),
            _layer_spec(layer, (1, D_MODEL))],
        out_specs=pl.BlockSpec((None, TM, D_MODEL), row),
        out_shape=jax.ShapeDtypeStruct((b, nt * TM, D_MODEL), F32),
        scratch_shapes=[pltpu.VMEM((S5_SLABS, TM, LANES), F32)],
        compiler_params=_cparams(("parallel", "arbitrary")),
        name="merge_mlp",
    )(xl, xc, o_n, yg, u, gates, modsel, ssm_d, w_glu, b_glu, w_br_s, w_br_a, w_out, norm_g,
      w1, b1, w2, b2)


def _rope_tables(n_tokens):
    rows = n_tokens // GRID_W
    row = jnp.repeat(jnp.arange(rows, dtype=jnp.int32), GRID_W).astype(F32)
    col = jnp.tile(jnp.arange(GRID_W, dtype=jnp.int32), rows).astype(F32)
    inv_freq = ROPE_BASE ** (-jnp.arange(N_FREQ, dtype=F32) / N_FREQ)
    ang = jnp.stack([row[:, None] * inv_freq, col[:, None] * inv_freq], axis=1)
    cos, sin = jnp.cos(ang), jnp.sin(ang)
    cos_l = jnp.tile(jnp.concatenate([cos[:, 0], cos[:, 0], cos[:, 1], cos[:, 1]], axis=1), (1, 2))
    sin_l = jnp.tile(jnp.concatenate([-sin[:, 0], sin[:, 0], -sin[:, 1], sin[:, 1]], axis=1), (1, 2))
    cos_t = jnp.concatenate([cos_l, jnp.ones((TM, LANES), F32)], axis=0)
    sin_t = jnp.concatenate([sin_l, jnp.zeros((TM, LANES), F32)], axis=0)
    return cos_t, sin_t


def kernel(x, c, ctx, c_ctx, ada_w, ada_b, norm_g, w_in, gate_b, lam_qk, subln_g, w_br_a,
           ssm_a_re, ssm_a_im, ssm_b_re, ssm_b_im, ssm_c_re, ssm_c_im, ssm_log_dt, ssm_d,
           w_glu, b_glu, w_br_s, w_out, w_mlp1, b_mlp1, w_mlp2, b_mlp2):
    b, seq, _ = x.shape
    nvalid = seq + CTX_LEN
    assert b + 1 <= 8 and ctx.shape[1] == CTX_LEN and seq % TM == 0 and seq % TQ == 0 and nvalid % TK == 0
    n_lat_tiles = seq // TM
    cos_t, sin_t = _rope_tables(seq)
    xl, xc = x, jnp.concatenate([ctx, jnp.zeros((b, ROWS_PAD, D_MODEL), F32)], axis=1)
    cc = jnp.zeros((8, D_MODEL), F32).at[:b].set(c).at[b].set(c_ctx)
    mod = _modulation(cc, ada_w, ada_b)
    modsel = jnp.stack([mod[:, :b], jnp.broadcast_to(mod[:, b:b + 1], (DEPTH, b, 6 * D_MODEL))], axis=2)
    modsel = modsel.reshape(DEPTH, b, 2, 1, 6 * D_MODEL)
    s5_params = _s5_params(ssm_a_re, ssm_a_im, ssm_log_dt, ssm_b_re, ssm_b_im, ssm_c_re, ssm_c_im)
    row3 = lambda a: a.reshape(DEPTH, 1, -1)
    w_in_b, w_glu_b, w_br_s_b, w_br_a_b, w_out_b, w1_b, w2_b = (
        w.astype(BF16) for w in (w_in, w_glu, w_br_s, w_br_a, w_out, w_mlp1, w_mlp2))
    for i in range(DEPTH):
        last = i == DEPTH - 1
        lam_init = 0.8 - 0.6 * math.exp(-0.3 * i)
        q, k, v, u, gates, xg, nrm = _in_proj(i, xl, xc, modsel, norm_g, w_in_b, row3(gate_b),
                                              cos_t, sin_t, n_lat_tiles)
        o_n = _attention_layer(i, q, k, v, nrm, lam_qk, row3(subln_g), lam_init, seq, with_ctx=not last)
        yg = _s5(i, xg, s5_params, b, nvalid)
        xl = _tail(i, xl, xc, o_n, yg, u, gates, modsel, row3(ssm_d), w_glu_b, row3(b_glu), w_br_s_b,
                   w_br_a_b, w_out_b, norm_g, w1_b, row3(b_mlp1), w2_b, row3(b_mlp2),
                   n_lat_tiles, n_lat_tiles if last else n_lat_tiles + 1)
        if not last:
            xc = xl[:, seq:]
    return xl
```

```python
import functools
import math

import jax
import jax.numpy as jnp
from jax import lax
from jax.experimental import pallas as pl
from jax.experimental.pallas import tpu as pltpu

F32 = jnp.float32
BF16 = jnp.bfloat16

D_MODEL = 1024
DEPTH = 2
GRID_W = 64
CTX_LEN = 256
N_HEADS = 4
HEAD_DIM = 64
V_DIM = 128
QK_W = N_HEADS * 2 * HEAD_DIM
ATTN_W = 512
SSM_W = 512
SSM_GROUP = 16
SSM_GROUPS = 32
SSM_STATE = 64
D_FF = 4096
IN_COLS = 4096
N_FREQ = 16
ROPE_BASE = 10000.0
EPS = 1e-6
LOG2E = 1.4426950408889634

LANES = 128
SUBLANES = 8
TM = 512
ROWS_PAD = TM - CTX_LEN
GATHER_ROWS = 16
TQ = 1024
TK = 2816
S5_T = 16
S5_TC = S5_T * SSM_GROUP
S5_SLABS = SSM_W // LANES
S5_GPS = LANES // SSM_GROUP
PAIR = 2 * SSM_STATE
S5_NP = 4
FF_CHUNK = 512
VMEM_LIMIT = 56 * 1024 * 1024

NEG_BIG = -1e30
BOUND_SLACK = 1.03
BOUND_LIMIT = 50.0

NT_DIMS = (((1,), (1,)), ((), ()))


def _cparams(sem):
    return pltpu.CompilerParams(dimension_semantics=sem, vmem_limit_bytes=VMEM_LIMIT)


def _layer_spec(layer, shape):
    nd = len(shape)
    return pl.BlockSpec((None,) + shape, lambda *_: (layer,) + (0,) * nd, pipeline_mode=pl.Buffered(1))


def _mod_spec(layer, ctx_tile):
    return pl.BlockSpec((None, None, None, 1, 6 * D_MODEL),
                        lambda bi, i: (layer, bi, i // ctx_tile, 0, 0))


def _stream_specs(n_lat_tiles):
    return [pl.BlockSpec((None, TM, D_MODEL), lambda bi, i: (bi, jnp.minimum(i, n_lat_tiles - 1), 0)),
            pl.BlockSpec((None, TM, D_MODEL), lambda bi, i: (bi, 0, 0))]


def _stream_tile(xl_ref, xc_ref, n_lat_tiles, tile_axis=1):
    return jnp.where(pl.program_id(tile_axis) >= n_lat_tiles, xc_ref[...], xl_ref[...])


def _rms(x, g):
    ms = jnp.mean(x * x, axis=-1, keepdims=True)
    return x * lax.rsqrt(ms + EPS) * g


def _sigmoid(x):
    return 0.5 * jnp.tanh(0.5 * x) + 0.5


def _mod_kernel(c_ref, w_ref, b_ref, o_ref):
    c = c_ref[...]
    s = c * jax.nn.sigmoid(c)
    o_ref[...] = jnp.dot(s.astype(BF16), w_ref[...].astype(BF16),
                         preferred_element_type=F32) + b_ref[...]


def _modulation(cc, ada_w, ada_b):
    depth, _, n = ada_w.shape
    bn = 1024
    return pl.pallas_call(
        _mod_kernel,
        grid=(depth, n // bn),
        in_specs=[pl.BlockSpec((8, D_MODEL), lambda l, j: (0, 0)),
                  pl.BlockSpec((None, D_MODEL, bn), lambda l, j: (l, 0, j)),
                  pl.BlockSpec((None, 1, bn), lambda l, j: (l, 0, j))],
        out_specs=pl.BlockSpec((None, 8, bn), lambda l, j: (l, 0, j)),
        out_shape=jax.ShapeDtypeStruct((depth, 8, n), F32),
        compiler_params=_cparams(("arbitrary", "arbitrary")),
        name="adaln_mod",
    )(cc, ada_w, ada_b.reshape(depth, 1, n))


def _lane_window(off):
    lane = lax.broadcasted_iota(jnp.int32, (GATHER_ROWS, LANES), 1)
    return (lane >= off) & (lane < off + SSM_GROUP)


def _rows_to_chunks(u, usc, xg_ref):
    for sl in range(S5_SLABS):
        usc[sl] = u[:, sl * LANES:(sl + 1) * LANES]
    for part in range(TM // (S5_T * GATHER_ROWS)):
        r0 = part * GATHER_ROWS
        for sl in range(S5_SLABS):
            acc = [[None] * (S5_TC // LANES) for _ in range(S5_GPS)]
            for t in range(S5_T):
                v = usc[sl, pl.ds(r0 * S5_T + t, GATHER_ROWS, stride=S5_T), :]
                lt, off = divmod(t * SSM_GROUP, LANES)
                win = _lane_window(off)
                for gl in range(S5_GPS):
                    shift = (off - gl * SSM_GROUP) % LANES
                    moved = pltpu.roll(v, shift, 1) if shift else v
                    prev = acc[gl][lt]
                    acc[gl][lt] = jnp.where(win, moved, 0.0 if prev is None else prev)
            for gl in range(S5_GPS):
                for lt in range(S5_TC // LANES):
                    xg_ref[sl * S5_GPS + gl, r0:r0 + GATHER_ROWS, lt * LANES:(lt + 1) * LANES] = (
                        acc[gl][lt].astype(xg_ref.dtype))


def _chunks_to_rows(yg_ref, ysc):
    for part in range(TM // (S5_T * GATHER_ROWS)):
        r0 = part * GATHER_ROWS
        for sl in range(S5_SLABS):
            tiles = [[yg_ref[sl * S5_GPS + gl, r0:r0 + GATHER_ROWS, lt * LANES:(lt + 1) * LANES]
                      for lt in range(S5_TC // LANES)] for gl in range(S5_GPS)]
            for t in range(S5_T):
                lt, off = divmod(t * SSM_GROUP, LANES)
                v = None
                for gl in range(S5_GPS):
                    shift = (gl * SSM_GROUP - off) % LANES
                    src = tiles[gl][lt]
                    moved = pltpu.roll(src, shift, 1) if shift else src
                    v = jnp.where(_lane_window(gl * SSM_GROUP), moved, 0.0 if v is None else v)
                ysc[sl, pl.ds(r0 * S5_T + t, GATHER_ROWS, stride=S5_T), :] = v
    return jnp.concatenate([ysc[sl] for sl in range(S5_SLABS)], axis=1)


def _map_segments():
    li = lax.broadcasted_iota(jnp.int32, (LANES, LANES), 0) // HEAD_DIM
    lj = lax.broadcasted_iota(jnp.int32, (LANES, LANES), 1) // HEAD_DIM
    return (li == lj).astype(BF16)


def _in_kernel(xl_ref, xc_ref, mod_ref, g_ref, w_ref, gb_ref, cos_ref, sin_ref,
               q_ref, k_ref, v_ref, u_ref, gate_ref, xg_ref, nrm_ref, usc, *, n_lat_tiles):
    for bi in range(xl_ref.shape[0]):
        _in_rows(_stream_tile(xl_ref.at[bi], xc_ref.at[bi], n_lat_tiles, tile_axis=0),
                 mod_ref.at[bi], g_ref, w_ref, gb_ref, cos_ref, sin_ref,
                 q_ref.at[bi], k_ref.at[bi], v_ref.at[bi], u_ref.at[bi], gate_ref.at[bi],
                 xg_ref.at[:, bi], nrm_ref.at[bi], usc.at[bi])


def _in_rows(x, mod_ref, g_ref, w_ref, gb_ref, cos_ref, sin_ref,
             q_ref, k_ref, v_ref, u_ref, gate_ref, xg_ref, nrm_ref, usc):
    h = _rms(x, g_ref[0:1]) * (1.0 + mod_ref[:, D_MODEL:2 * D_MODEL]) + mod_ref[:, 0:D_MODEL]
    hb = h.astype(BF16)
    cos = cos_ref[...]
    sin = sin_ref[...]
    lane = lax.broadcasted_iota(jnp.int32, (TM, LANES), 1)
    first_half = (lane & 31) < 16
    sel = (lax.broadcasted_iota(jnp.int32, (8, LANES), 0)
           == lax.broadcasted_iota(jnp.int32, (8, LANES), 1) // HEAD_DIM).astype(BF16)
    slot = lax.broadcasted_iota(jnp.int32, (8, LANES), 1)
    nrm = jnp.zeros((8, LANES), F32)

    def rope_store(col0, out_ref, scale, slot0, nrm):
        t = jnp.dot(hb, w_ref[:, col0:col0 + QK_W], preferred_element_type=F32)
        for j in range(QK_W // LANES):
            tj = t[:, j * LANES:(j + 1) * LANES]
            partner = jnp.where(first_half, pltpu.roll(tj, LANES - 16, 1), pltpu.roll(tj, 16, 1))
            r = tj * cos + partner * sin
            if scale != 1.0:
                r = r * scale
            out_ref[:, j * LANES:(j + 1) * LANES] = r.astype(out_ref.dtype)
            n2 = lax.dot_general(sel, (r * r).astype(BF16), NT_DIMS, preferred_element_type=F32)
            nrm = jnp.where(slot == slot0 + j, jnp.max(n2, axis=1, keepdims=True), nrm)
        return nrm

    nrm = rope_store(0, q_ref, HEAD_DIM ** -0.5 * LOG2E, 0, nrm)
    nrm = rope_store(QK_W, k_ref, 1.0, N_HEADS, nrm)
    nrm_ref[...] = nrm
    v_ref[...] = jnp.dot(hb, w_ref[:, 1024:1536], preferred_element_type=F32).astype(v_ref.dtype)
    u = jnp.dot(hb, w_ref[:, 1536:2048], preferred_element_type=F32)
    u_ref[...] = u
    _rows_to_chunks(u, usc, xg_ref)
    g = jnp.dot(hb, w_ref[:, 2048:IN_COLS], preferred_element_type=F32) + gb_ref[...]
    gate_ref[...] = _sigmoid(g).astype(gate_ref.dtype)


def _in_proj(layer, xl, xc, modsel, norm_g, w_in_b, gate_b, cos_t, sin_t, n_lat_tiles):
    b = xl.shape[0]
    nt = n_lat_tiles + 1
    n = nt * TM
    row = lambda i: (0, i, 0)
    outs = pl.pallas_call(
        functools.partial(_in_kernel, n_lat_tiles=n_lat_tiles),
        grid=(nt,),
        in_specs=[pl.BlockSpec((b, TM, D_MODEL), lambda i: (0, jnp.minimum(i, n_lat_tiles - 1), 0)),
                  pl.BlockSpec((b, TM, D_MODEL), lambda i: (0, 0, 0)),
                  pl.BlockSpec((None, b, None, 1, 6 * D_MODEL), lambda i: (layer, 0, i // n_lat_tiles, 0, 0)),
                  _layer_spec(layer, (4, D_MODEL)),
                  _layer_spec(layer, (D_MODEL, IN_COLS)),
                  _layer_spec(layer, (1, 2 * D_MODEL)),
                  pl.BlockSpec((TM, LANES), lambda i: (i, 0)),
                  pl.BlockSpec((TM, LANES), lambda i: (i, 0))],
        out_specs=[pl.BlockSpec((b, TM, QK_W), row),
                   pl.BlockSpec((b, TM, QK_W), row),
                   pl.BlockSpec((b, TM, ATTN_W), row),
                   pl.BlockSpec((b, TM, SSM_W), row),
                   pl.BlockSpec((b, TM, 2 * D_MODEL), row),
                   pl.BlockSpec((SSM_GROUPS, b, TM // S5_T, S5_TC), lambda i: (0, 0, i, 0)),
                   pl.BlockSpec((b, None, 8, LANES), lambda i: (0, i, 0, 0))],
        out_shape=[jax.ShapeDtypeStruct((b, n, QK_W), BF16),
                   jax.ShapeDtypeStruct((b, n, QK_W), BF16),
                   jax.ShapeDtypeStruct((b, n, ATTN_W), BF16),
                   jax.ShapeDtypeStruct((b, n, SSM_W), F32),
                   jax.ShapeDtypeStruct((b, n, 2 * D_MODEL), BF16),
                   jax.ShapeDtypeStruct((SSM_GROUPS, b, n // S5_T, S5_TC), BF16),
                   jax.ShapeDtypeStruct((b, nt, 8, LANES), F32)],
        scratch_shapes=[pltpu.VMEM((b, S5_SLABS, TM, LANES), F32)],
        compiler_params=_cparams(("arbitrary",)),
        name="in_proj",
    )(xl, xc, modsel, norm_g, w_in_b, gate_b, cos_t, sin_t)
    q, k, v, u, gates, xg, nrm = outs
    return q, k, v, u, gates, xg.reshape(SSM_GROUPS, b * n // S5_T, S5_TC), nrm


def _attn_kernel(lamqk_ref, subg_ref, q_ref, k_ref, v_ref, o_ref,
                 qm_sc, m_sc, l_sc, acc_sc, *, lam_init, nk):
    ki = pl.program_id(3)

    @pl.when(ki == 0)
    def _init():
        q = q_ref[...]
        lane = lax.broadcasted_iota(jnp.int32, q.shape, 1)
        zero = jnp.zeros_like(q)
        qm_sc[0] = jnp.where(lane < HEAD_DIM, q, zero)
        qm_sc[1] = jnp.where(lane >= HEAD_DIM, q, zero)
        m_sc[...] = jnp.full(m_sc.shape, NEG_BIG, F32)
        l_sc[...] = jnp.zeros(l_sc.shape, F32)
        acc_sc[...] = jnp.zeros(acc_sc.shape, F32)

    k = k_ref[...]
    v = v_ref[...]
    for c in range(2):
        s = lax.dot_general(qm_sc[c], k, NT_DIMS, preferred_element_type=F32)
        m_prev = m_sc[c]
        m_new = jnp.maximum(m_prev, jnp.max(s, axis=1, keepdims=True))
        alpha = jnp.exp2(m_prev - m_new)
        p = jnp.exp2(s - m_new)
        l_sc[c] = alpha * l_sc[c] + jnp.sum(p, axis=1, keepdims=True)
        acc_sc[c] = alpha * acc_sc[c] + jnp.dot(p.astype(BF16), v, preferred_element_type=F32)
        m_sc[c] = m_new

    @pl.when(ki == nk - 1)
    def _fin():
        lq = lamqk_ref[...]
        lam = (jnp.exp(jnp.sum(lq[0:1] * lq[1:2], axis=1, keepdims=True))
               - jnp.exp(jnp.sum(lq[2:3] * lq[3:4], axis=1, keepdims=True)) + lam_init)
        o = acc_sc[0] / l_sc[0] - lam * (acc_sc[1] / l_sc[1])
        o_ref[...] = (_rms(o, subg_ref[...]) * (1.0 - lam_init)).astype(o_ref.dtype)


def _attn_bounded_kernel(lamqk_ref, subg_ref, kmax_ref, q_ref, k_ref, v_ref, o_ref,
                         qx_sc, kx_sc, l_sc, acc_sc, *, lam_init, nk):
    ki = pl.program_id(3)
    tq = q_ref.shape[0]
    tk = k_ref.shape[0]

    @pl.when(ki == 0)
    def _init():
        q = q_ref[...]
        lane = lax.broadcasted_iota(jnp.int32, (tq, LANES), 1)
        qf = q.astype(F32)
        n2 = jnp.dot((qf * qf).astype(BF16), _map_segments(), preferred_element_type=F32)
        bound = jnp.sqrt(n2) * kmax_ref[...] * BOUND_SLACK
        zero = jnp.zeros_like(q)
        qx_sc[0, :, 0:LANES] = jnp.where(lane < HEAD_DIM, q, zero)
        qx_sc[0, :, LANES:2 * LANES] = jnp.where(lane == 0, -bound, 0.0).astype(BF16)
        qx_sc[1, :, 0:LANES] = jnp.where(lane >= HEAD_DIM, q, zero)
        qx_sc[1, :, LANES:2 * LANES] = jnp.where(lane == 0, -pltpu.roll(bound, HEAD_DIM, 1), 0.0).astype(BF16)
        lane_k = lax.broadcasted_iota(jnp.int32, (tk, LANES), 1)
        kx_sc[:, LANES:2 * LANES] = jnp.where(lane_k == 0, 1.0, 0.0).astype(BF16)
        l_sc[...] = jnp.zeros(l_sc.shape, F32)
        acc_sc[...] = jnp.zeros(acc_sc.shape, F32)

    kx_sc[:, 0:LANES] = k_ref[...]
    kx = kx_sc[...]
    v = v_ref[...]
    for c in range(2):
        s = lax.dot_general(qx_sc[c], kx, NT_DIMS, preferred_element_type=F32)
        p = jnp.exp2(s)
        part = p[:, 0:LANES]
        for j in range(1, tk // LANES):
            part = part + p[:, j * LANES:(j + 1) * LANES]
        l_sc[c] += part
        acc_sc[c] += jnp.dot(p.astype(BF16), v, preferred_element_type=F32)

    @pl.when(ki == nk - 1)
    def _fin():
        lq = lamqk_ref[...]
        lam = (jnp.exp(jnp.sum(lq[0:1] * lq[1:2], axis=1, keepdims=True))
               - jnp.exp(jnp.sum(lq[2:3] * lq[3:4], axis=1, keepdims=True)) + lam_init)
        l0 = jnp.sum(l_sc[0], axis=1, keepdims=True)
        l1 = jnp.sum(l_sc[1], axis=1, keepdims=True)
        o = acc_sc[0] / l0 - lam * (acc_sc[1] / l1)
        o_ref[...] = (_rms(o, subg_ref[...]) * (1.0 - lam_init)).astype(o_ref.dtype)


def _drop_input(kern, idx, *refs):
    return kern(*refs[:idx], *refs[idx + 1:])


def _attention(layer, q, k, v, kmax, lam_qk, subln_g, lam_init, *, bounded, tq, tk, q0, nq, k0, nk,
               o_prev=None):
    b, n, _ = q.shape
    q_idx = lambda bi, h, qi, ki: (bi, qi + q0, h)
    kv_idx = lambda bi, h, qi, ki: (bi, ki + k0, h)
    in_specs = [pl.BlockSpec((None, 4, HEAD_DIM), lambda *_: (layer, 0, 0)),
                pl.BlockSpec((None, 1, V_DIM), lambda *_: (layer, 0, 0)),
                pl.BlockSpec((None, None, 1, LANES), lambda bi, h, qi, ki: (bi, h, 0, 0)),
                pl.BlockSpec((None, tq, LANES), q_idx),
                pl.BlockSpec((None, tk, LANES), kv_idx),
                pl.BlockSpec((None, tk, LANES), kv_idx)]
    args = [lam_qk, subln_g, kmax, q, k, v]
    if bounded:
        kern = functools.partial(_attn_bounded_kernel, lam_init=lam_init, nk=nk)
        scratch = [pltpu.VMEM((2, tq, 2 * LANES), BF16),
                   pltpu.VMEM((tk, 2 * LANES), BF16),
                   pltpu.VMEM((2, tq, LANES), F32),
                   pltpu.VMEM((2, tq, V_DIM), F32)]
    else:
        kern = functools.partial(_drop_input, functools.partial(_attn_kernel, lam_init=lam_init, nk=nk), 2)
        scratch = [pltpu.VMEM((2, tq, LANES), BF16),
                   pltpu.VMEM((2, tq, 1), F32),
                   pltpu.VMEM((2, tq, 1), F32),
                   pltpu.VMEM((2, tq, V_DIM), F32)]
    aliases = {}
    if o_prev is not None:
        in_specs.append(pl.BlockSpec(memory_space=pl.ANY))
        args.append(o_prev)
        aliases = {6: 0}
        kern = functools.partial(_drop_input, kern, 6)
    return pl.pallas_call(
        kern,
        grid=(b, N_HEADS, nq, nk),
        in_specs=in_specs,
        out_specs=pl.BlockSpec((None, tq, LANES), q_idx),
        out_shape=jax.ShapeDtypeStruct((b, n, ATTN_W), BF16),
        scratch_shapes=scratch,
        input_output_aliases=aliases,
        compiler_params=_cparams(("parallel", "parallel", "parallel", "arbitrary")),
        name="diff_attn_bounded" if bounded else "diff_attn",
    )(*args)


def _attn_pair_kernel(lamqk_ref, subg_ref, kmax_ref, q_ref, k_ref, v_ref, o_ref, *, lam_init, tk):
    tq = q_ref.shape[0]
    nkeys = k_ref.shape[0]
    lane = lax.broadcasted_iota(jnp.int32, (tq, LANES), 1)
    lane_k = lax.broadcasted_iota(jnp.int32, (tk, LANES), 1)
    one_tile = jnp.where(lane_k == 0, 1.0, 0.0).astype(BF16)
    lq = lamqk_ref[...]
    lam = (jnp.exp(jnp.sum(lq[0:1] * lq[1:2], axis=1, keepdims=True))
           - jnp.exp(jnp.sum(lq[2:3] * lq[3:4], axis=1, keepdims=True)) + lam_init)
    for hh in range(2):
        cols = slice(hh * LANES, (hh + 1) * LANES)
        q = q_ref[:, cols]
        qf = q.astype(F32)
        n2 = jnp.dot((qf * qf).astype(BF16), _map_segments(), preferred_element_type=F32)
        bound = jnp.sqrt(n2) * kmax_ref[hh] * BOUND_SLACK
        zero = jnp.zeros_like(q)
        qx = [jnp.concatenate([jnp.where(lane < HEAD_DIM, q, zero),
                               jnp.where(lane == 0, -bound, 0.0).astype(BF16)], axis=1),
              jnp.concatenate([jnp.where(lane >= HEAD_DIM, q, zero),
                               jnp.where(lane == 0, -pltpu.roll(bound, HEAD_DIM, 1), 0.0).astype(BF16)],
                              axis=1)]
        l = [jnp.zeros((tq, LANES), F32)] * 2
        acc = [jnp.zeros((tq, V_DIM), F32)] * 2
        for kb in range(nkeys // tk):
            rows = slice(kb * tk, (kb + 1) * tk)
            kx = jnp.concatenate([k_ref[rows, cols], one_tile], axis=1)
            v = v_ref[rows, cols]
            for c in range(2):
                s = lax.dot_general(qx[c], kx, NT_DIMS, preferred_element_type=F32)
                p = jnp.exp2(s)
                part = p[:, 0:LANES]
                for j in range(1, tk // LANES):
                    part = part + p[:, j * LANES:(j + 1) * LANES]
                l[c] = l[c] + part
                acc[c] = acc[c] + jnp.dot(p.astype(BF16), v, preferred_element_type=F32)
        l0 = jnp.sum(l[0], axis=1, keepdims=True)
        l1 = jnp.sum(l[1], axis=1, keepdims=True)
        o = acc[0] / l0 - lam * (acc[1] / l1)
        o_ref[:, cols] = (_rms(o, subg_ref[...]) * (1.0 - lam_init)).astype(o_ref.dtype)


def _attention_pairs(layer, q, k, v, kmax, lam_qk, subln_g, lam_init, *, tq, tk, nq, nkeys):
    b, n, _ = q.shape
    q_idx = lambda bi, hp, qi: (bi, qi, hp)
    kv_idx = lambda bi, hp, qi: (bi, 0, hp)
    return pl.pallas_call(
        functools.partial(_attn_pair_kernel, lam_init=lam_init, tk=tk),
        grid=(b, N_HEADS // 2, nq),
        in_specs=[pl.BlockSpec((None, 4, HEAD_DIM), lambda *_: (layer, 0, 0)),
                  pl.BlockSpec((None, 1, V_DIM), lambda *_: (layer, 0, 0)),
                  pl.BlockSpec((None, 2, 1, LANES), lambda bi, hp, qi: (bi, hp, 0, 0)),
                  pl.BlockSpec((None, tq, 2 * LANES), q_idx),
                  pl.BlockSpec((None, nkeys, 2 * LANES), kv_idx, pipeline_mode=pl.Buffered(1)),
                  pl.BlockSpec((None, nkeys, 2 * LANES), kv_idx, pipeline_mode=pl.Buffered(1))],
        out_specs=pl.BlockSpec((None, tq, 2 * LANES), q_idx),
        out_shape=jax.ShapeDtypeStruct((b, n, ATTN_W), BF16),
        compiler_params=_cparams(("parallel", "parallel", "arbitrary")),
        name="diff_attn_pairs",
    )(lam_qk, subln_g, kmax, q, k, v)


def _attention_layer(layer, q, k, v, nrm, lam_qk, subln_g, lam_init, seq, with_ctx):
    b = q.shape[0]
    nkeys = seq + CTX_LEN
    top = jnp.sqrt(jnp.max(nrm[:, :, 0:2, 0:2 * N_HEADS], axis=1))
    qmax, kmax = top[:, :, 0:N_HEADS], top[:, :, N_HEADS:]
    small = jnp.max(qmax * kmax) * BOUND_SLACK < BOUND_LIMIT
    kmax4 = jnp.repeat(jnp.swapaxes(kmax, 1, 2), HEAD_DIM, axis=-1).reshape(b, N_HEADS, 1, LANES)

    def run(bounded):
        if bounded:
            o_n = _attention_pairs(layer, q, k, v, kmax4, lam_qk, subln_g, lam_init,
                                   tq=TQ, tk=TK, nq=seq // TQ, nkeys=nkeys)
        else:
            o_n = _attention(layer, q, k, v, kmax4, lam_qk, subln_g, lam_init, bounded=False,
                             tq=TQ, tk=TK, q0=0, nq=seq // TQ, k0=0, nk=nkeys // TK)
        if with_ctx:
            o_n = _attention(layer, q, k, v, kmax4, lam_qk, subln_g, lam_init, bounded=bounded,
                             tq=TM, tk=CTX_LEN, q0=seq // TM, nq=1, k0=seq // CTX_LEN, nk=1,
                             o_prev=o_n)
        return o_n

    return lax.cond(small, lambda: run(True), lambda: run(False))


def _cmul(ar, ai, br, bi):
    return ar * br - ai * bi, ar * bi + ai * br


def _cpow_by_bits(lr, li, expo, nbits):
    pr = jnp.ones(expo.shape, F32)
    pi = jnp.zeros(expo.shape, F32)
    br, bi = lr, li
    for bit in range(nbits):
        on = ((expo >> bit) & 1) == 1
        fr = jnp.where(on, br, 1.0)
        fi = jnp.where(on, bi, 0.0)
        pr, pi = _cmul(pr, pi, fr, fi)
        if bit + 1 < nbits:
            br, bi = _cmul(br, bi, br, bi)
    return pr, pi


def _discretize(ar, ai, log_dt):
    dt = jnp.exp(log_dt)
    er = jnp.exp(ar * dt)
    lbr = er * jnp.cos(ai * dt)
    lbi = er * jnp.sin(ai * dt)
    den = ar * ar + ai * ai
    nr = lbr - 1.0
    cr = (nr * ar + lbi * ai) / den
    ci = (lbi * ar - nr * ai) / den
    return lbr, lbi, cr, ci


def _s5_kernel(x_ref, ar_ref, ai_ref, ldt_ref, btr_ref, bti_ref, ctr_ref, cti_ref,
               y_ref, sext, toep, cs_sc, zre, zim, hre, him, *, nrows, nchunk, nbatch, nctx):
    nlat = nchunk - nctx
    blk = lax.broadcasted_iota(jnp.int32, (S5_TC, PAIR), 0) // SSM_GROUP
    lane_g = lax.broadcasted_iota(jnp.int32, (1, PAIR), 1) // SSM_STATE

    for d in range(2):
        fwd = d == 0
        step_r, step_i = [], []
        for pi in range(S5_NP):
            lbr, lbi, cr, ci = _discretize(ar_ref[d, pi], ai_ref[d, pi], ldt_ref[d, pi])
            bbr, bbi = _cmul(cr, ci, btr_ref[d, pi], bti_ref[d, pi])
            bbr_t = jnp.concatenate([bbr] * S5_T, axis=0)
            bbi_t = jnp.concatenate([bbi] * S5_T, axis=0)
            ctr_t = jnp.concatenate([ctr_ref[d, pi]] * S5_T, axis=0)
            cti_t = jnp.concatenate([cti_ref[d, pi]] * S5_T, axis=0)
            upr, upi = _cpow_by_bits(lbr, lbi, blk, 4)
            dnr, dni = _cpow_by_bits(lbr, lbi, S5_T - 1 - blk, 4)
            (per, pei), (pwr, pwi) = ((upr, upi), (dnr, dni)) if fwd else ((dnr, dni), (upr, upi))
            bsr, bsi = _cmul(pwr, pwi, bbr_t, bbi_t)
            wcr, wci = _cmul(per, pei, ctr_t, cti_t)
            c1r, c1i = _cmul(wcr, wci, lbr, lbi)
            tr, ti = lbr, lbi
            for _ in range(int(math.log2(S5_T))):
                tr, ti = _cmul(tr, ti, tr, ti)
            step_r.append(tr)
            step_i.append(ti)

            z = None
            for gg in range(2):
                g = 2 * pi + gg
                lm = lane_g == gg
                strip = (lax.dot_general(jnp.where(lm, bbr, 0.0), wcr, NT_DIMS, preferred_element_type=F32,
                                         precision=lax.Precision.HIGHEST)
                         - lax.dot_general(jnp.where(lm, bbi, 0.0), wci, NT_DIMS, preferred_element_type=F32,
                                           precision=lax.Precision.HIGHEST))
                if fwd:
                    sext[:, 0:S5_TC] = jnp.zeros((SSM_GROUP, S5_TC), F32)
                    sext[:, S5_TC:2 * S5_TC] = strip
                    for j in range(S5_T):
                        lo = S5_TC - SSM_GROUP * j
                        toep[j * SSM_GROUP:(j + 1) * SSM_GROUP, :] = sext[:, lo:lo + S5_TC].astype(BF16)
                else:
                    sext[:, 0:S5_TC] = strip
                    sext[:, S5_TC:2 * S5_TC] = jnp.zeros((SSM_GROUP, S5_TC), F32)
                    for j in range(S5_T):
                        lo = SSM_GROUP * (S5_T - 1 - j)
                        toep[j * SSM_GROUP:(j + 1) * SSM_GROUP, :] = sext[:, lo:lo + S5_TC].astype(BF16)
                x = x_ref[g]
                yi = jnp.dot(x, toep[...], preferred_element_type=F32)
                if fwd:
                    y_ref[g] = yi
                else:
                    y_ref[g] += yi
                bs = jnp.concatenate([jnp.where(lm, bsr, 0.0), jnp.where(lm, bsi, 0.0)], axis=1)
                zg = jnp.dot(x, bs.astype(BF16), preferred_element_type=F32)
                z = zg if z is None else z + zg
                cs_sc[g] = jnp.concatenate([jnp.where(lm, c1r, 0.0), jnp.where(lm, c1i, 0.0)],
                                           axis=1).astype(BF16)
            for b in range(nbatch):
                s = pi * nbatch + b
                zre[pl.ds(s, nrows, stride=SUBLANES), :] = z[b * nrows:(b + 1) * nrows, 0:PAIR]
                zim[pl.ds(s, nrows, stride=SUBLANES), :] = z[b * nrows:(b + 1) * nrows, PAIR:2 * PAIR]

        sub = lax.broadcasted_iota(jnp.int32, (SUBLANES, PAIR), 0) // nbatch
        tr = jnp.zeros((SUBLANES, PAIR), F32)
        ti = jnp.zeros((SUBLANES, PAIR), F32)
        for pi in range(S5_NP):
            tr = jnp.where(sub == pi, step_r[pi], tr)
            ti = jnp.where(sub == pi, step_i[pi], ti)
        hre[...] = jnp.zeros(hre.shape, F32)
        him[...] = jnp.zeros(him.shape, F32)

        def body(n, carry):
            if fwd:
                ch = jnp.where(n < nctx, nlat + n, n - nctx)
            else:
                ch = nchunk - 1 - n
            h_r, h_i = carry
            r = pl.multiple_of(ch * SUBLANES, SUBLANES)
            hre[pl.ds(r, SUBLANES), :] = h_r
            him[pl.ds(r, SUBLANES), :] = h_i
            n_r, n_i = _cmul(tr, ti, h_r, h_i)
            return n_r + zre[pl.ds(r, SUBLANES), :], n_i + zim[pl.ds(r, SUBLANES), :]

        zero = jnp.zeros((SUBLANES, PAIR), F32)
        lax.fori_loop(0, nchunk, body, (zero, zero), unroll=8)

        for pi in range(S5_NP):
            for b in range(nbatch):
                s = pi * nbatch + b
                h_r = hre[pl.ds(s, nrows, stride=SUBLANES), :].astype(BF16)
                h_i = him[pl.ds(s, nrows, stride=SUBLANES), :].astype(BF16)
                for gg in range(2):
                    g = 2 * pi + gg
                    y_ref[g, b * nrows:(b + 1) * nrows, :] += (
                        lax.dot_general(h_r, cs_sc[g, :, 0:PAIR], NT_DIMS, preferred_element_type=F32)
                        - lax.dot_general(h_i, cs_sc[g, :, PAIR:2 * PAIR], NT_DIMS, preferred_element_type=F32))


def _s5_params(a_re, a_im, log_dt, b_re, b_im, c_re, c_im):
    depth = a_re.shape[0]
    gp = SSM_GROUPS // 2
    row = lambda a: a.reshape(depth, 2, gp, 1, PAIR)
    ldt = jnp.broadcast_to(log_dt[..., None], a_re.shape)
    bt = lambda a: jnp.transpose(a.reshape(depth, 2, gp, 2, SSM_STATE, SSM_GROUP),
                                 (0, 1, 2, 5, 3, 4)).reshape(depth, 2, gp, SSM_GROUP, PAIR)
    ct = lambda a: jnp.transpose(a.reshape(depth, 2, gp, 2, SSM_GROUP, SSM_STATE),
                                 (0, 1, 2, 4, 3, 5)).reshape(depth, 2, gp, SSM_GROUP, PAIR)
    return row(a_re), row(a_im), row(ldt), bt(b_re), bt(b_im), ct(c_re), ct(c_im)


def _s5(layer, xg, params, nbatch, nvalid):
    g, nr, _ = xg.shape
    gstep = 2 * S5_NP
    assert S5_NP * nbatch == SUBLANES
    vec = pl.BlockSpec((None, 2, S5_NP, 1, PAIR), lambda i: (layer, 0, i, 0, 0))
    mat = pl.BlockSpec((None, 2, S5_NP, SSM_GROUP, PAIR), lambda i: (layer, 0, i, 0, 0))
    kern = functools.partial(_s5_kernel, nrows=nr // nbatch, nchunk=nvalid // S5_T, nbatch=nbatch,
                             nctx=CTX_LEN // S5_T)
    return pl.pallas_call(
        kern,
        grid=(g // gstep,),
        in_specs=[pl.BlockSpec((gstep, nr, S5_TC), lambda i: (i, 0, 0)),
                  vec, vec, vec, mat, mat, mat, mat],
        out_specs=pl.BlockSpec((gstep, nr, S5_TC), lambda i: (i, 0, 0)),
        out_shape=jax.ShapeDtypeStruct((g, nr, S5_TC), F32),
        scratch_shapes=[pltpu.VMEM((SSM_GROUP, 2 * S5_TC), F32),
                        pltpu.VMEM((S5_TC, S5_TC), BF16),
                        pltpu.VMEM((gstep, S5_TC, 2 * PAIR), BF16),
                        pltpu.VMEM((nr // nbatch * SUBLANES, PAIR), F32),
                        pltpu.VMEM((nr // nbatch * SUBLANES, PAIR), F32),
                        pltpu.VMEM((nr // nbatch * SUBLANES, PAIR), F32),
                        pltpu.VMEM((nr // nbatch * SUBLANES, PAIR), F32)],
        compiler_params=_cparams(("parallel",)),
        name="s5_scan",
    )(xg, *params)


def _tail_kernel(xl_ref, xc_ref, on_ref, yg_ref, u_ref, gate_ref, mod_ref, d_ref, wglu_ref, bglu_ref,
                 wbs_ref, wba_ref, wout_ref, ng_ref, w1_ref, b1_ref, w2_ref, b2_ref,
                 o_ref, ysc, *, n_lat_tiles):
    pa = jnp.dot(on_ref[...], wba_ref[...], preferred_element_type=F32)
    ys = _chunks_to_rows(yg_ref, ysc) + u_ref[...] * d_ref[...]
    gl = jax.nn.gelu(ys)
    z = gl * _sigmoid(jnp.dot(gl.astype(BF16), wglu_ref[...], preferred_element_type=F32) + bglu_ref[...])
    ps = jnp.dot(z.astype(BF16), wbs_ref[...], preferred_element_type=F32)
    gate = gate_ref[...].astype(F32)
    mix = gate[:, 0:D_MODEL] * pa + gate[:, D_MODEL:2 * D_MODEL] * ps
    m2 = jnp.dot(mix.astype(BF16), wout_ref[...], preferred_element_type=F32)
    x1 =_stream_tile(xl_ref, xc_ref, n_lat_tiles) + mod_ref[:, 2 * D_MODEL:3 * D_MODEL] * _rms(m2, ng_ref[1:2])
    h = _rms(x1, ng_ref[2:3]) * (1.0 + mod_ref[:, 4 * D_MODEL:5 * D_MODEL]) \
        + mod_ref[:, 3 * D_MODEL:4 * D_MODEL]
    hb = h.astype(BF16)
    o = b2_ref[...]
    for c0 in range(0, D_FF, FF_CHUNK):
        f = jnp.dot(hb, w1_ref[:, c0:c0 + FF_CHUNK], preferred_element_type=F32) + b1_ref[:, c0:c0 + FF_CHUNK]
        f = jnp.square(jnp.maximum(f, 0.0))
        o = o + jnp.dot(f.astype(BF16), w2_ref[c0:c0 + FF_CHUNK, :], preferred_element_type=F32)
    o_ref[...] = x1 + mod_ref[:, 5 * D_MODEL:6 * D_MODEL] * _rms(o, ng_ref[3:4])


def _tail(layer, xl, xc, o_n, yg, u, gates, modsel, ssm_d, w_glu, b_glu, w_br_s, w_br_a, w_out, norm_g,
          w1, b1, w2, b2, n_lat_tiles, nt):
    b = xl.shape[0]
    nt_all = n_lat_tiles + 1
    row = lambda bi, i: (bi, i, 0)
    return pl.pallas_call(
        functools.partial(_tail_kernel, n_lat_tiles=n_lat_tiles),
        grid=(b, nt),
        in_specs=_stream_specs(n_lat_tiles) + [
            pl.BlockSpec((None, TM, ATTN_W), row),
            pl.BlockSpec((SSM_GROUPS, TM // S5_T, S5_TC), lambda bi, i: (0, bi * nt_all + i, 0)),
            pl.BlockSpec((None, TM, SSM_W), row),
            pl.BlockSpec((None, TM, 2 * D_MODEL), row),
            _mod_spec(layer, n_lat_tiles),
            _layer_spec(layer, (1, SSM_W)),
            _layer_spec(layer, (SSM_W, SSM_W)),
            _layer_spec(layer, (1, SSM_W)),
            _layer_spec(layer, (SSM_W, D_MODEL)),
            _layer_spec(layer, (ATTN_W, D_MODEL)),
            _layer_spec(layer, (D_MODEL, D_MODEL)),
            _layer_spec(layer, (4, D_MODEL)),
            _layer_spec(layer, (D_MODEL, D_FF)),
            _layer_spec(layer, (1, D_FF)),
            _layer_spec(layer, (D_FF, D_MODEL)),
            _layer_spec(layer, (1, D_MODEL))],
        out_specs=pl.BlockSpec((None, TM, D_MODEL), row),
        out_shape=jax.ShapeDtypeStruct((b, nt * TM, D_MODEL), F32),
        scratch_shapes=[pltpu.VMEM((S5_SLABS, TM, LANES), F32)],
        compiler_params=_cparams(("parallel", "arbitrary")),
        name="merge_mlp",
    )(xl, xc, o_n, yg, u, gates, modsel, ssm_d, w_glu, b_glu, w_br_s, w_br_a, w_out, norm_g,
      w1, b1, w2, b2)


def _rope_tables(n_tokens):
    rows = n_tokens // GRID_W
    row = jnp.repeat(jnp.arange(rows, dtype=jnp.int32), GRID_W).astype(F32)
    col = jnp.tile(jnp.arange(GRID_W, dtype=jnp.int32), rows).astype(F32)
    inv_freq = ROPE_BASE ** (-jnp.arange(N_FREQ, dtype=F32) / N_FREQ)
    ang = jnp.stack([row[:, None] * inv_freq, col[:, None] * inv_freq], axis=1)
    cos, sin = jnp.cos(ang), jnp.sin(ang)
    cos_l = jnp.tile(jnp.concatenate([cos[:, 0], cos[:, 0], cos[:, 1], cos[:, 1]], axis=1), (1, 2))
    sin_l = jnp.tile(jnp.concatenate([-sin[:, 0], sin[:, 0], -sin[:, 1], sin[:, 1]], axis=1), (1, 2))
    cos_t = jnp.concatenate([cos_l, jnp.ones((TM, LANES), F32)], axis=0)
    sin_t = jnp.concatenate([sin_l, jnp.zeros((TM, LANES), F32)], axis=0)
    return cos_t, sin_t


def kernel(x, c, ctx, c_ctx, ada_w, ada_b, norm_g, w_in, gate_b, lam_qk, subln_g, w_br_a,
           ssm_a_re, ssm_a_im, ssm_b_re, ssm_b_im, ssm_c_re, ssm_c_im, ssm_log_dt, ssm_d,
           w_glu, b_glu, w_br_s, w_out, w_mlp1, b_mlp1, w_mlp2, b_mlp2):
    b, seq, _ = x.shape
    nvalid = seq + CTX_LEN
    assert b + 1 <= 8 and ctx.shape[1] == CTX_LEN and seq % TM == 0 and seq % TQ == 0 and nvalid % TK == 0
    n_lat_tiles = seq // TM
    cos_t, sin_t = _rope_tables(seq)
    xl, xc = x, jnp.concatenate([ctx, jnp.zeros((b, ROWS_PAD, D_MODEL), F32)], axis=1)
    cc = jnp.zeros((8, D_MODEL), F32).at[:b].set(c).at[b].set(c_ctx)
    mod = _modulation(cc, ada_w, ada_b)
    modsel = jnp.stack([mod[:, :b], jnp.broadcast_to(mod[:, b:b + 1], (DEPTH, b, 6 * D_MODEL))], axis=2)
    modsel = modsel.reshape(DEPTH, b, 2, 1, 6 * D_MODEL)
    s5_params = _s5_params(ssm_a_re, ssm_a_im, ssm_log_dt, ssm_b_re, ssm_b_im, ssm_c_re, ssm_c_im)
    row3 = lambda a: a.reshape(DEPTH, 1, -1)
    w_in_b, w_glu_b, w_br_s_b, w_br_a_b, w_out_b, w1_b, w2_b = (
        w.astype(BF16) for w in (w_in, w_glu, w_br_s, w_br_a, w_out, w_mlp1, w_mlp2))
    for i in range(DEPTH):
        last = i == DEPTH - 1
        lam_init = 0.8 - 0.6 * math.exp(-0.3 * i)
        q, k, v, u, gates, xg, nrm = _in_proj(i, xl, xc, modsel, norm_g, w_in_b, row3(gate_b),
                                              cos_t, sin_t, n_lat_tiles)
        o_n = _attention_layer(i, q, k, v, nrm, lam_qk, row3(subln_g), lam_init, seq, with_ctx=not last)
        yg = _s5(i, xg, s5_params, b, nvalid)
        xl = _tail(i, xl, xc, o_n, yg, u, gates, modsel, row3(ssm_d), w_glu_b, row3(b_glu), w_br_s_b,
                   w_br_a_b, w_out_b, norm_g, w1_b, row3(b_mlp1), w2_b, row3(b_mlp2),
                   n_lat_tiles, n_lat_tiles if last else n_lat_tiles + 1)
        if not last:
            xc = xl[:, seq:]
    return xl
```

```python
import functools
import math

import jax
import jax.numpy as jnp
from jax import lax
from jax.experimental import pallas as pl
from jax.experimental.pallas import tpu as pltpu

F32 = jnp.float32
BF16 = jnp.bfloat16

D_MODEL = 1024
DEPTH = 2
GRID_W = 64
CTX_LEN = 256
N_HEADS = 4
HEAD_DIM = 64
V_DIM = 128
QK_W = N_HEADS * 2 * HEAD_DIM
ATTN_W = 512
SSM_W = 512
SSM_GROUP = 16
SSM_GROUPS = 32
SSM_STATE = 64
D_FF = 4096
IN_COLS = 4096
N_FREQ = 16
ROPE_BASE = 10000.0
EPS = 1e-6
LOG2E = 1.4426950408889634

LANES = 128
SUBLANES = 8
TM = 512
ROWS_PAD = TM - CTX_LEN
GATHER_ROWS = 16
TQ = 1024
TK = 2816
S5_T = 16
S5_TC = S5_T * SSM_GROUP
S5_SLABS = SSM_W // LANES
S5_GPS = LANES // SSM_GROUP
PAIR = 2 * SSM_STATE
S5_NP = 4
FF_CHUNK = 1024
VMEM_LIMIT = 56 * 1024 * 1024

NEG_BIG = -1e30
BOUND_SLACK = 1.03
BOUND_LIMIT = 50.0

NT_DIMS = (((1,), (1,)), ((), ()))


def _cparams(sem):
    return pltpu.CompilerParams(dimension_semantics=sem, vmem_limit_bytes=VMEM_LIMIT)


def _layer_spec(layer, shape):
    nd = len(shape)
    return pl.BlockSpec((None,) + shape, lambda *_: (layer,) + (0,) * nd, pipeline_mode=pl.Buffered(1))


def _mod_spec(layer, ctx_tile):
    return pl.BlockSpec((None, None, None, 1, 6 * D_MODEL),
                        lambda bi, i: (layer, bi, i // ctx_tile, 0, 0))


def _stream_specs(n_lat_tiles):
    return [pl.BlockSpec((None, TM, D_MODEL), lambda bi, i: (bi, jnp.minimum(i, n_lat_tiles - 1), 0)),
            pl.BlockSpec((None, TM, D_MODEL), lambda bi, i: (bi, 0, 0))]


def _stream_tile(xl_ref, xc_ref, n_lat_tiles, tile_axis=1):
    return jnp.where(pl.program_id(tile_axis) >= n_lat_tiles, xc_ref[...], xl_ref[...])


def _rms(x, g):
    ms = jnp.mean(x * x, axis=-1, keepdims=True)
    return x * lax.rsqrt(ms + EPS) * g


def _sigmoid(x):
    return 0.5 * jnp.tanh(0.5 * x) + 0.5


def _mod_kernel(c_ref, w_ref, b_ref, o_ref):
    c = c_ref[...]
    s = c * jax.nn.sigmoid(c)
    o_ref[...] = jnp.dot(s.astype(BF16), w_ref[...].astype(BF16),
                         preferred_element_type=F32) + b_ref[...]


def _modulation(cc, ada_w, ada_b):
    depth, _, n = ada_w.shape
    bn = 1024
    return pl.pallas_call(
        _mod_kernel,
        grid=(depth, n // bn),
        in_specs=[pl.BlockSpec((8, D_MODEL), lambda l, j: (0, 0)),
                  pl.BlockSpec((None, D_MODEL, bn), lambda l, j: (l, 0, j)),
                  pl.BlockSpec((None, 1, bn), lambda l, j: (l, 0, j))],
        out_specs=pl.BlockSpec((None, 8, bn), lambda l, j: (l, 0, j)),
        out_shape=jax.ShapeDtypeStruct((depth, 8, n), F32),
        compiler_params=_cparams(("arbitrary", "arbitrary")),
        name="adaln_mod",
    )(cc, ada_w, ada_b.reshape(depth, 1, n))


def _lane_window(off):
    lane = lax.broadcasted_iota(jnp.int32, (GATHER_ROWS, LANES), 1)
    return (lane >= off) & (lane < off + SSM_GROUP)


def _rows_to_chunks(u, usc, xg_ref):
    for sl in range(S5_SLABS):
        usc[sl] = u[:, sl * LANES:(sl + 1) * LANES]
    for part in range(TM // (S5_T * GATHER_ROWS)):
        r0 = part * GATHER_ROWS
        for sl in range(S5_SLABS):
            acc = [[None] * (S5_TC // LANES) for _ in range(S5_GPS)]
            for t in range(S5_T):
                v = usc[sl, pl.ds(r0 * S5_T + t, GATHER_ROWS, stride=S5_T), :]
                lt, off = divmod(t * SSM_GROUP, LANES)
                win = _lane_window(off)
                for gl in range(S5_GPS):
                    shift = (off - gl * SSM_GROUP) % LANES
                    moved = pltpu.roll(v, shift, 1) if shift else v
                    prev = acc[gl][lt]
                    acc[gl][lt] = jnp.where(win, moved, 0.0 if prev is None else prev)
            for gl in range(S5_GPS):
                for lt in range(S5_TC // LANES):
                    xg_ref[sl * S5_GPS + gl, r0:r0 + GATHER_ROWS, lt * LANES:(lt + 1) * LANES] = (
                        acc[gl][lt].astype(xg_ref.dtype))


def _chunks_to_rows(yg_ref, ysc):
    for part in range(TM // (S5_T * GATHER_ROWS)):
        r0 = part * GATHER_ROWS
        for sl in range(S5_SLABS):
            tiles = [[yg_ref[sl * S5_GPS + gl, r0:r0 + GATHER_ROWS, lt * LANES:(lt + 1) * LANES]
                      for lt in range(S5_TC // LANES)] for gl in range(S5_GPS)]
            for t in range(S5_T):
                lt, off = divmod(t * SSM_GROUP, LANES)
                v = None
                for gl in range(S5_GPS):
                    shift = (gl * SSM_GROUP - off) % LANES
                    src = tiles[gl][lt]
                    moved = pltpu.roll(src, shift, 1) if shift else src
                    v = jnp.where(_lane_window(gl * SSM_GROUP), moved, 0.0 if v is None else v)
                ysc[sl, pl.ds(r0 * S5_T + t, GATHER_ROWS, stride=S5_T), :] = v
    return jnp.concatenate([ysc[sl] for sl in range(S5_SLABS)], axis=1)


def _map_segments():
    li = lax.broadcasted_iota(jnp.int32, (LANES, LANES), 0) // HEAD_DIM
    lj = lax.broadcasted_iota(jnp.int32, (LANES, LANES), 1) // HEAD_DIM
    return (li == lj).astype(BF16)


def _in_kernel(xl_ref, xc_ref, mod_ref, g_ref, w_ref, gb_ref, cos_ref, sin_ref,
               q_ref, k_ref, v_ref, u_ref, gate_ref, xg_ref, nrm_ref, usc, *, n_lat_tiles):
    for bi in range(xl_ref.shape[0]):
        _in_rows(_stream_tile(xl_ref.at[bi], xc_ref.at[bi], n_lat_tiles, tile_axis=0),
                 mod_ref.at[bi], g_ref, w_ref, gb_ref, cos_ref, sin_ref,
                 q_ref.at[bi], k_ref.at[bi], v_ref.at[bi], u_ref.at[bi], gate_ref.at[bi],
                 xg_ref.at[:, bi], nrm_ref.at[bi], usc.at[bi])


def _in_rows(x, mod_ref, g_ref, w_ref, gb_ref, cos_ref, sin_ref,
             q_ref, k_ref, v_ref, u_ref, gate_ref, xg_ref, nrm_ref, usc):
    h = _rms(x, g_ref[0:1]) * (1.0 + mod_ref[:, D_MODEL:2 * D_MODEL]) + mod_ref[:, 0:D_MODEL]
    hb = h.astype(BF16)
    cos = cos_ref[...]
    sin = sin_ref[...]
    lane = lax.broadcasted_iota(jnp.int32, (TM, LANES), 1)
    first_half = (lane & 31) < 16
    sel = (lax.broadcasted_iota(jnp.int32, (8, LANES), 0)
           == lax.broadcasted_iota(jnp.int32, (8, LANES), 1) // HEAD_DIM).astype(BF16)
    slot = lax.broadcasted_iota(jnp.int32, (8, LANES), 1)
    nrm = jnp.zeros((8, LANES), F32)

    def rope_store(col0, out_ref, scale, slot0, nrm):
        t = jnp.dot(hb, w_ref[:, col0:col0 + QK_W], preferred_element_type=F32)
        for j in range(QK_W // LANES):
            tj = t[:, j * LANES:(j + 1) * LANES]
            partner = jnp.where(first_half, pltpu.roll(tj, LANES - 16, 1), pltpu.roll(tj, 16, 1))
            r = tj * cos + partner * sin
            if scale != 1.0:
                r = r * scale
            out_ref[:, j * LANES:(j + 1) * LANES] = r.astype(out_ref.dtype)
            n2 = lax.dot_general(sel, (r * r).astype(BF16), NT_DIMS, preferred_element_type=F32)
            nrm = jnp.where(slot == slot0 + j, jnp.max(n2, axis=1, keepdims=True), nrm)
        return nrm

    nrm = rope_store(0, q_ref, HEAD_DIM ** -0.5 * LOG2E, 0, nrm)
    nrm = rope_store(QK_W, k_ref, 1.0, N_HEADS, nrm)
    nrm_ref[...] = nrm
    v_ref[...] = jnp.dot(hb, w_ref[:, 1024:1536], preferred_element_type=F32).astype(v_ref.dtype)
    u = jnp.dot(hb, w_ref[:, 1536:2048], preferred_element_type=F32)
    u_ref[...] = u
    _rows_to_chunks(u, usc, xg_ref)
    g = jnp.dot(hb, w_ref[:, 2048:IN_COLS], preferred_element_type=F32) + gb_ref[...]
    gate_ref[...] = _sigmoid(g).astype(gate_ref.dtype)


def _in_proj(layer, xl, xc, modsel, norm_g, w_in_b, gate_b, cos_t, sin_t, n_lat_tiles):
    b = xl.shape[0]
    nt = n_lat_tiles + 1
    n = nt * TM
    row = lambda i: (0, i, 0)
    outs = pl.pallas_call(
        functools.partial(_in_kernel, n_lat_tiles=n_lat_tiles),
        grid=(nt,),
        in_specs=[pl.BlockSpec((b, TM, D_MODEL), lambda i: (0, jnp.minimum(i, n_lat_tiles - 1), 0)),
                  pl.BlockSpec((b, TM, D_MODEL), lambda i: (0, 0, 0)),
                  pl.BlockSpec((None, b, None, 1, 6 * D_MODEL), lambda i: (layer, 0, i // n_lat_tiles, 0, 0)),
                  _layer_spec(layer, (4, D_MODEL)),
                  _layer_spec(layer, (D_MODEL, IN_COLS)),
                  _layer_spec(layer, (1, 2 * D_MODEL)),
                  pl.BlockSpec((TM, LANES), lambda i: (i, 0)),
                  pl.BlockSpec((TM, LANES), lambda i: (i, 0))],
        out_specs=[pl.BlockSpec((b, TM, QK_W), row),
                   pl.BlockSpec((b, TM, QK_W), row),
                   pl.BlockSpec((b, TM, ATTN_W), row),
                   pl.BlockSpec((b, TM, SSM_W), row),
                   pl.BlockSpec((b, TM, 2 * D_MODEL), row),
                   pl.BlockSpec((SSM_GROUPS, b, TM // S5_T, S5_TC), lambda i: (0, 0, i, 0)),
                   pl.BlockSpec((b, None, 8, LANES), lambda i: (0, i, 0, 0))],
        out_shape=[jax.ShapeDtypeStruct((b, n, QK_W), BF16),
                   jax.ShapeDtypeStruct((b, n, QK_W), BF16),
                   jax.ShapeDtypeStruct((b, n, ATTN_W), BF16),
                   jax.ShapeDtypeStruct((b, n, SSM_W), F32),
                   jax.ShapeDtypeStruct((b, n, 2 * D_MODEL), BF16),
                   jax.ShapeDtypeStruct((SSM_GROUPS, b, n // S5_T, S5_TC), BF16),
                   jax.ShapeDtypeStruct((b, nt, 8, LANES), F32)],
        scratch_shapes=[pltpu.VMEM((b, S5_SLABS, TM, LANES), F32)],
        compiler_params=_cparams(("arbitrary",)),
        name="in_proj",
    )(xl, xc, modsel, norm_g, w_in_b, gate_b, cos_t, sin_t)
    q, k, v, u, gates, xg, nrm = outs
    return q, k, v, u, gates, xg.reshape(SSM_GROUPS, b * n // S5_T, S5_TC), nrm


def _attn_kernel(lamqk_ref, subg_ref, q_ref, k_ref, v_ref, o_ref,
                 qm_sc, m_sc, l_sc, acc_sc, *, lam_init, nk):
    ki = pl.program_id(3)

    @pl.when(ki == 0)
    def _init():
        q = q_ref[...]
        lane = lax.broadcasted_iota(jnp.int32, q.shape, 1)
        zero = jnp.zeros_like(q)
        qm_sc[0] = jnp.where(lane < HEAD_DIM, q, zero)
        qm_sc[1] = jnp.where(lane >= HEAD_DIM, q, zero)
        m_sc[...] = jnp.full(m_sc.shape, NEG_BIG, F32)
        l_sc[...] = jnp.zeros(l_sc.shape, F32)
        acc_sc[...] = jnp.zeros(acc_sc.shape, F32)

    k = k_ref[...]
    v = v_ref[...]
    for c in range(2):
        s = lax.dot_general(qm_sc[c], k, NT_DIMS, preferred_element_type=F32)
        m_prev = m_sc[c]
        m_new = jnp.maximum(m_prev, jnp.max(s, axis=1, keepdims=True))
        alpha = jnp.exp2(m_prev - m_new)
        p = jnp.exp2(s - m_new)
        l_sc[c] = alpha * l_sc[c] + jnp.sum(p, axis=1, keepdims=True)
        acc_sc[c] = alpha * acc_sc[c] + jnp.dot(p.astype(BF16), v, preferred_element_type=F32)
        m_sc[c] = m_new

    @pl.when(ki == nk - 1)
    def _fin():
        lq = lamqk_ref[...]
        lam = (jnp.exp(jnp.sum(lq[0:1] * lq[1:2], axis=1, keepdims=True))
               - jnp.exp(jnp.sum(lq[2:3] * lq[3:4], axis=1, keepdims=True)) + lam_init)
        o = acc_sc[0] / l_sc[0] - lam * (acc_sc[1] / l_sc[1])
        o_ref[...] = (_rms(o, subg_ref[...]) * (1.0 - lam_init)).astype(o_ref.dtype)


def _attn_bounded_kernel(lamqk_ref, subg_ref, kmax_ref, q_ref, k_ref, v_ref, o_ref,
                         qx_sc, kx_sc, l_sc, acc_sc, *, lam_init, nk):
    ki = pl.program_id(3)
    tq = q_ref.shape[0]
    tk = k_ref.shape[0]

    @pl.when(ki == 0)
    def _init():
        q = q_ref[...]
        lane = lax.broadcasted_iota(jnp.int32, (tq, LANES), 1)
        qf = q.astype(F32)
        n2 = jnp.dot((qf * qf).astype(BF16), _map_segments(), preferred_element_type=F32)
        bound = jnp.sqrt(n2) * kmax_ref[...] * BOUND_SLACK
        zero = jnp.zeros_like(q)
        qx_sc[0, :, 0:LANES] = jnp.where(lane < HEAD_DIM, q, zero)
        qx_sc[0, :, LANES:2 * LANES] = jnp.where(lane == 0, -bound, 0.0).astype(BF16)
        qx_sc[1, :, 0:LANES] = jnp.where(lane >= HEAD_DIM, q, zero)
        qx_sc[1, :, LANES:2 * LANES] = jnp.where(lane == 0, -pltpu.roll(bound, HEAD_DIM, 1), 0.0).astype(BF16)
        lane_k = lax.broadcasted_iota(jnp.int32, (tk, LANES), 1)
        kx_sc[:, LANES:2 * LANES] = jnp.where(lane_k == 0, 1.0, 0.0).astype(BF16)
        l_sc[...] = jnp.zeros(l_sc.shape, F32)
        acc_sc[...] = jnp.zeros(acc_sc.shape, F32)

    kx_sc[:, 0:LANES] = k_ref[...]
    kx = kx_sc[...]
    v = v_ref[...]
    for c in range(2):
        s = lax.dot_general(qx_sc[c], kx, NT_DIMS, preferred_element_type=F32)
        p = jnp.exp2(s)
        part = p[:, 0:LANES]
        for j in range(1, tk // LANES):
            part = part + p[:, j * LANES:(j + 1) * LANES]
        l_sc[c] += part
        acc_sc[c] += jnp.dot(p.astype(BF16), v, preferred_element_type=F32)

    @pl.when(ki == nk - 1)
    def _fin():
        lq = lamqk_ref[...]
        lam = (jnp.exp(jnp.sum(lq[0:1] * lq[1:2], axis=1, keepdims=True))
               - jnp.exp(jnp.sum(lq[2:3] * lq[3:4], axis=1, keepdims=True)) + lam_init)
        l0 = jnp.sum(l_sc[0], axis=1, keepdims=True)
        l1 = jnp.sum(l_sc[1], axis=1, keepdims=True)
        o = acc_sc[0] / l0 - lam * (acc_sc[1] / l1)
        o_ref[...] = (_rms(o, subg_ref[...]) * (1.0 - lam_init)).astype(o_ref.dtype)


def _drop_input(kern, idx, *refs):
    return kern(*refs[:idx], *refs[idx + 1:])


def _attention(layer, q, k, v, kmax, lam_qk, subln_g, lam_init, *, bounded, tq, tk, q0, nq, k0, nk,
               o_prev=None):
    b, n, _ = q.shape
    q_idx = lambda bi, h, qi, ki: (bi, qi + q0, h)
    kv_idx = lambda bi, h, qi, ki: (bi, ki + k0, h)
    in_specs = [pl.BlockSpec((None, 4, HEAD_DIM), lambda *_: (layer, 0, 0)),
                pl.BlockSpec((None, 1, V_DIM), lambda *_: (layer, 0, 0)),
                pl.BlockSpec((None, None, 1, LANES), lambda bi, h, qi, ki: (bi, h, 0, 0)),
                pl.BlockSpec((None, tq, LANES), q_idx),
                pl.BlockSpec((None, tk, LANES), kv_idx),
                pl.BlockSpec((None, tk, LANES), kv_idx)]
    args = [lam_qk, subln_g, kmax, q, k, v]
    if bounded:
        kern = functools.partial(_attn_bounded_kernel, lam_init=lam_init, nk=nk)
        scratch = [pltpu.VMEM((2, tq, 2 * LANES), BF16),
                   pltpu.VMEM((tk, 2 * LANES), BF16),
                   pltpu.VMEM((2, tq, LANES), F32),
                   pltpu.VMEM((2, tq, V_DIM), F32)]
    else:
        kern = functools.partial(_drop_input, functools.partial(_attn_kernel, lam_init=lam_init, nk=nk), 2)
        scratch = [pltpu.VMEM((2, tq, LANES), BF16),
                   pltpu.VMEM((2, tq, 1), F32),
                   pltpu.VMEM((2, tq, 1), F32),
                   pltpu.VMEM((2, tq, V_DIM), F32)]
    aliases = {}
    if o_prev is not None:
        in_specs.append(pl.BlockSpec(memory_space=pl.ANY))
        args.append(o_prev)
        aliases = {6: 0}
        kern = functools.partial(_drop_input, kern, 6)
    return pl.pallas_call(
        kern,
        grid=(b, N_HEADS, nq, nk),
        in_specs=in_specs,
        out_specs=pl.BlockSpec((None, tq, LANES), q_idx),
        out_shape=jax.ShapeDtypeStruct((b, n, ATTN_W), BF16),
        scratch_shapes=scratch,
        input_output_aliases=aliases,
        compiler_params=_cparams(("parallel", "parallel", "parallel", "arbitrary")),
        name="diff_attn_bounded" if bounded else "diff_attn",
    )(*args)


def _attn_pair_kernel(lamqk_ref, subg_ref, kmax_ref, q_ref, k_ref, v_ref, o_ref, *, lam_init, tk):
    tq = q_ref.shape[0]
    nkeys = k_ref.shape[0]
    lane = lax.broadcasted_iota(jnp.int32, (tq, LANES), 1)
    lane_k = lax.broadcasted_iota(jnp.int32, (tk, LANES), 1)
    one_tile = jnp.where(lane_k == 0, 1.0, 0.0).astype(BF16)
    lq = lamqk_ref[...]
    lam = (jnp.exp(jnp.sum(lq[0:1] * lq[1:2], axis=1, keepdims=True))
           - jnp.exp(jnp.sum(lq[2:3] * lq[3:4], axis=1, keepdims=True)) + lam_init)
    for hh in range(2):
        cols = slice(hh * LANES, (hh + 1) * LANES)
        q = q_ref[:, cols]
        qf = q.astype(F32)
        n2 = jnp.dot((qf * qf).astype(BF16), _map_segments(), preferred_element_type=F32)
        bound = jnp.sqrt(n2) * kmax_ref[hh] * BOUND_SLACK
        zero = jnp.zeros_like(q)
        qx = [jnp.concatenate([jnp.where(lane < HEAD_DIM, q, zero),
                               jnp.where(lane == 0, -bound, 0.0).astype(BF16)], axis=1),
              jnp.concatenate([jnp.where(lane >= HEAD_DIM, q, zero),
                               jnp.where(lane == 0, -pltpu.roll(bound, HEAD_DIM, 1), 0.0).astype(BF16)],
                              axis=1)]
        l = [jnp.zeros((tq, LANES), F32)] * 2
        acc = [jnp.zeros((tq, V_DIM), F32)] * 2
        for kb in range(nkeys // tk):
            rows = slice(kb * tk, (kb + 1) * tk)
            kx = jnp.concatenate([k_ref[rows, cols], one_tile], axis=1)
            v = v_ref[rows, cols]
            for c in range(2):
                s = lax.dot_general(qx[c], kx, NT_DIMS, preferred_element_type=F32)
                p = jnp.exp2(s)
                part = p[:, 0:LANES]
                for j in range(1, tk // LANES):
                    part = part + p[:, j * LANES:(j + 1) * LANES]
                l[c] = l[c] + part
                acc[c] = acc[c] + jnp.dot(p.astype(BF16), v, preferred_element_type=F32)
        l0 = jnp.sum(l[0], axis=1, keepdims=True)
        l1 = jnp.sum(l[1], axis=1, keepdims=True)
        o = acc[0] / l0 - lam * (acc[1] / l1)
        o_ref[:, cols] = (_rms(o, subg_ref[...]) * (1.0 - lam_init)).astype(o_ref.dtype)


def _attention_pairs(layer, q, k, v, kmax, lam_qk, subln_g, lam_init, *, tq, tk, nq, nkeys):
    b, n, _ = q.shape
    q_idx = lambda bi, hp, qi: (bi, qi, hp)
    kv_idx = lambda bi, hp, qi: (bi, 0, hp)
    return pl.pallas_call(
        functools.partial(_attn_pair_kernel, lam_init=lam_init, tk=tk),
        grid=(b, N_HEADS // 2, nq),
        in_specs=[pl.BlockSpec((None, 4, HEAD_DIM), lambda *_: (layer, 0, 0)),
                  pl.BlockSpec((None, 1, V_DIM), lambda *_: (layer, 0, 0)),
                  pl.BlockSpec((None, 2, 1, LANES), lambda bi, hp, qi: (bi, hp, 0, 0)),
                  pl.BlockSpec((None, tq, 2 * LANES), q_idx),
                  pl.BlockSpec((None, nkeys, 2 * LANES), kv_idx, pipeline_mode=pl.Buffered(1)),
                  pl.BlockSpec((None, nkeys, 2 * LANES), kv_idx, pipeline_mode=pl.Buffered(1))],
        out_specs=pl.BlockSpec((None, tq, 2 * LANES), q_idx),
        out_shape=jax.ShapeDtypeStruct((b, n, ATTN_W), BF16),
        compiler_params=_cparams(("parallel", "parallel", "arbitrary")),
        name="diff_attn_pairs",
    )(lam_qk, subln_g, kmax, q, k, v)


def _attention_layer(layer, q, k, v, nrm, lam_qk, subln_g, lam_init, seq, with_ctx):
    b = q.shape[0]
    nkeys = seq + CTX_LEN
    top = jnp.sqrt(jnp.max(nrm[:, :, 0:2, 0:2 * N_HEADS], axis=1))
    qmax, kmax = top[:, :, 0:N_HEADS], top[:, :, N_HEADS:]
    small = jnp.max(qmax * kmax) * BOUND_SLACK < BOUND_LIMIT
    kmax4 = jnp.repeat(jnp.swapaxes(kmax, 1, 2), HEAD_DIM, axis=-1).reshape(b, N_HEADS, 1, LANES)

    def run(bounded):
        if bounded:
            o_n = _attention_pairs(layer, q, k, v, kmax4, lam_qk, subln_g, lam_init,
                                   tq=TQ, tk=TK, nq=seq // TQ, nkeys=nkeys)
        else:
            o_n = _attention(layer, q, k, v, kmax4, lam_qk, subln_g, lam_init, bounded=False,
                             tq=TQ, tk=TK, q0=0, nq=seq // TQ, k0=0, nk=nkeys // TK)
        if with_ctx:
            o_n = _attention(layer, q, k, v, kmax4, lam_qk, subln_g, lam_init, bounded=bounded,
                             tq=TM, tk=CTX_LEN, q0=seq // TM, nq=1, k0=seq // CTX_LEN, nk=1,
                             o_prev=o_n)
        return o_n

    return lax.cond(small, lambda: run(True), lambda: run(False))


def _cmul(ar, ai, br, bi):
    return ar * br - ai * bi, ar * bi + ai * br


def _block_powers(lr, li):
    pr = jnp.ones((SSM_GROUP, lr.shape[1]), F32)
    pi = jnp.zeros((SSM_GROUP, lr.shape[1]), F32)
    sr, si = lr, li
    for _ in range(int(math.log2(S5_T))):
        qr, qi = _cmul(pr, pi, sr, si)
        pr = jnp.concatenate([pr, qr], axis=0)
        pi = jnp.concatenate([pi, qi], axis=0)
        sr, si = _cmul(sr, si, sr, si)
    return pr, pi


def _reverse_blocks(a):
    return jnp.concatenate([a[t * SSM_GROUP:(t + 1) * SSM_GROUP] for t in reversed(range(S5_T))], axis=0)


def _discretize(ar, ai, log_dt):
    dt = jnp.exp(log_dt)
    er = jnp.exp(ar * dt)
    lbr = er * jnp.cos(ai * dt)
    lbi = er * jnp.sin(ai * dt)
    den = ar * ar + ai * ai
    nr = lbr - 1.0
    cr = (nr * ar + lbi * ai) / den
    ci = (lbi * ar - nr * ai) / den
    return lbr, lbi, cr, ci


def _s5_kernel(x_ref, ar_ref, ai_ref, ldt_ref, btr_ref, bti_ref, ctr_ref, cti_ref,
               y_ref, sext, toep, cs_sc, zre, zim, hre, him, *, nrows, nchunk, nbatch, nctx):
    nlat = nchunk - nctx
    lane_g = lax.broadcasted_iota(jnp.int32, (1, PAIR), 1) // SSM_STATE

    for d in range(2):
        fwd = d == 0
        step_r, step_i = [], []
        for pi in range(S5_NP):
            lbr, lbi, cr, ci = _discretize(ar_ref[d, pi], ai_ref[d, pi], ldt_ref[d, pi])
            bbr, bbi = _cmul(cr, ci, btr_ref[d, pi], bti_ref[d, pi])
            bbr_t = jnp.concatenate([bbr] * S5_T, axis=0)
            bbi_t = jnp.concatenate([bbi] * S5_T, axis=0)
            ctr_t = jnp.concatenate([ctr_ref[d, pi]] * S5_T, axis=0)
            cti_t = jnp.concatenate([cti_ref[d, pi]] * S5_T, axis=0)
            upr, upi = _block_powers(lbr, lbi)
            dnr, dni = _reverse_blocks(upr), _reverse_blocks(upi)
            (per, pei), (pwr, pwi) = ((upr, upi), (dnr, dni)) if fwd else ((dnr, dni), (upr, upi))
            bsr, bsi = _cmul(pwr, pwi, bbr_t, bbi_t)
            wcr, wci = _cmul(per, pei, ctr_t, cti_t)
            c1r, c1i = _cmul(wcr, wci, lbr, lbi)
            tr, ti = lbr, lbi
            for _ in range(int(math.log2(S5_T))):
                tr, ti = _cmul(tr, ti, tr, ti)
            step_r.append(tr)
            step_i.append(ti)

            z = None
            for gg in range(2):
                g = 2 * pi + gg
                lm = lane_g == gg
                strip = (lax.dot_general(jnp.where(lm, bbr, 0.0), wcr, NT_DIMS, preferred_element_type=F32,
                                         precision=lax.Precision.HIGHEST)
                         - lax.dot_general(jnp.where(lm, bbi, 0.0), wci, NT_DIMS, preferred_element_type=F32,
                                           precision=lax.Precision.HIGHEST))
                if fwd:
                    sext[:, 0:S5_TC] = jnp.zeros((SSM_GROUP, S5_TC), F32)
                    sext[:, S5_TC:2 * S5_TC] = strip
                    for j in range(S5_T):
                        lo = S5_TC - SSM_GROUP * j
                        toep[j * SSM_GROUP:(j + 1) * SSM_GROUP, :] = sext[:, lo:lo + S5_TC].astype(BF16)
                else:
                    sext[:, 0:S5_TC] = strip
                    sext[:, S5_TC:2 * S5_TC] = jnp.zeros((SSM_GROUP, S5_TC), F32)
                    for j in range(S5_T):
                        lo = SSM_GROUP * (S5_T - 1 - j)
                        toep[j * SSM_GROUP:(j + 1) * SSM_GROUP, :] = sext[:, lo:lo + S5_TC].astype(BF16)
                x = x_ref[g]
                yi = jnp.dot(x, toep[...], preferred_element_type=F32)
                if fwd:
                    y_ref[g] = yi
                else:
                    y_ref[g] += yi
                bs = jnp.concatenate([jnp.where(lm, bsr, 0.0), jnp.where(lm, bsi, 0.0)], axis=1)
                zg = jnp.dot(x, bs.astype(BF16), preferred_element_type=F32)
                z = zg if z is None else z + zg
                cs_sc[g] = jnp.concatenate([jnp.where(lm, c1r, 0.0), jnp.where(lm, c1i, 0.0)],
                                           axis=1).astype(BF16)
            for b in range(nbatch):
                s = pi * nbatch + b
                zre[pl.ds(s, nrows, stride=SUBLANES), :] = z[b * nrows:(b + 1) * nrows, 0:PAIR]
                zim[pl.ds(s, nrows, stride=SUBLANES), :] = z[b * nrows:(b + 1) * nrows, PAIR:2 * PAIR]

        sub = lax.broadcasted_iota(jnp.int32, (SUBLANES, PAIR), 0) // nbatch
        tr = jnp.zeros((SUBLANES, PAIR), F32)
        ti = jnp.zeros((SUBLANES, PAIR), F32)
        for pi in range(S5_NP):
            tr = jnp.where(sub == pi, step_r[pi], tr)
            ti = jnp.where(sub == pi, step_i[pi], ti)
        hre[...] = jnp.zeros(hre.shape, F32)
        him[...] = jnp.zeros(him.shape, F32)

        def body(n, carry):
            if fwd:
                ch = jnp.where(n < nctx, nlat + n, n - nctx)
            else:
                ch = nchunk - 1 - n
            h_r, h_i = carry
            r = pl.multiple_of(ch * SUBLANES, SUBLANES)
            hre[pl.ds(r, SUBLANES), :] = h_r
            him[pl.ds(r, SUBLANES), :] = h_i
            n_r, n_i = _cmul(tr, ti, h_r, h_i)
            return n_r + zre[pl.ds(r, SUBLANES), :], n_i + zim[pl.ds(r, SUBLANES), :]

        zero = jnp.zeros((SUBLANES, PAIR), F32)
        lax.fori_loop(0, nchunk, body, (zero, zero), unroll=8)

        for pi in range(S5_NP):
            for b in range(nbatch):
                s = pi * nbatch + b
                h_r = hre[pl.ds(s, nrows, stride=SUBLANES), :].astype(BF16)
                h_i = him[pl.ds(s, nrows, stride=SUBLANES), :].astype(BF16)
                for gg in range(2):
                    g = 2 * pi + gg
                    y_ref[g, b * nrows:(b + 1) * nrows, :] += (
                        lax.dot_general(h_r, cs_sc[g, :, 0:PAIR], NT_DIMS, preferred_element_type=F32)
                        - lax.dot_general(h_i, cs_sc[g, :, PAIR:2 * PAIR], NT_DIMS, preferred_element_type=F32))


def _s5_params(a_re, a_im, log_dt, b_re, b_im, c_re, c_im):
    depth = a_re.shape[0]
    gp = SSM_GROUPS // 2
    row = lambda a: a.reshape(depth, 2, gp, 1, PAIR)
    ldt = jnp.broadcast_to(log_dt[..., None], a_re.shape)
    bt = lambda a: jnp.transpose(a.reshape(depth, 2, gp, 2, SSM_STATE, SSM_GROUP),
                                 (0, 1, 2, 5, 3, 4)).reshape(depth, 2, gp, SSM_GROUP, PAIR)
    ct = lambda a: jnp.transpose(a.reshape(depth, 2, gp, 2, SSM_GROUP, SSM_STATE),
                                 (0, 1, 2, 4, 3, 5)).reshape(depth, 2, gp, SSM_GROUP, PAIR)
    return row(a_re), row(a_im), row(ldt), bt(b_re), bt(b_im), ct(c_re), ct(c_im)


def _s5(layer, xg, params, nbatch, nvalid):
    g, nr, _ = xg.shape
    gstep = 2 * S5_NP
    assert S5_NP * nbatch == SUBLANES
    vec = pl.BlockSpec((None, 2, S5_NP, 1, PAIR), lambda i: (layer, 0, i, 0, 0))
    mat = pl.BlockSpec((None, 2, S5_NP, SSM_GROUP, PAIR), lambda i: (layer, 0, i, 0, 0))
    kern = functools.partial(_s5_kernel, nrows=nr // nbatch, nchunk=nvalid // S5_T, nbatch=nbatch,
                             nctx=CTX_LEN // S5_T)
    return pl.pallas_call(
        kern,
        grid=(g // gstep,),
        in_specs=[pl.BlockSpec((gstep, nr, S5_TC), lambda i: (i, 0, 0)),
                  vec, vec, vec, mat, mat, mat, mat],
        out_specs=pl.BlockSpec((gstep, nr, S5_TC), lambda i: (i, 0, 0)),
        out_shape=jax.ShapeDtypeStruct((g, nr, S5_TC), F32),
        scratch_shapes=[pltpu.VMEM((SSM_GROUP, 2 * S5_TC), F32),
                        pltpu.VMEM((S5_TC, S5_TC), BF16),
                        pltpu.VMEM((gstep, S5_TC, 2 * PAIR), BF16),
                        pltpu.VMEM((nr // nbatch * SUBLANES, PAIR), F32),
                        pltpu.VMEM((nr // nbatch * SUBLANES, PAIR), F32),
                        pltpu.VMEM((nr // nbatch * SUBLANES, PAIR), F32),
                        pltpu.VMEM((nr // nbatch * SUBLANES, PAIR), F32)],
        compiler_params=_cparams(("parallel",)),
        name="s5_scan",
    )(xg, *params)


def _tail_kernel(xl_ref, xc_ref, on_ref, yg_ref, u_ref, gate_ref, mod_ref, d_ref, wglu_ref, bglu_ref,
                 wbs_ref, wba_ref, wout_ref, ng_ref, w1_ref, b1_ref, w2_ref, b2_ref,
                 o_ref, ysc, *, n_lat_tiles):
    pa = jnp.dot(on_ref[...], wba_ref[...], preferred_element_type=F32)
    ys = _chunks_to_rows(yg_ref, ysc) + u_ref[...] * d_ref[...]
    gl = jax.nn.gelu(ys)
    z = gl * _sigmoid(jnp.dot(gl.astype(BF16), wglu_ref[...], preferred_element_type=F32) + bglu_ref[...])
    ps = jnp.dot(z.astype(BF16), wbs_ref[...], preferred_element_type=F32)
    gate = gate_ref[...].astype(F32)
    mix = gate[:, 0:D_MODEL] * pa + gate[:, D_MODEL:2 * D_MODEL] * ps
    m2 = jnp.dot(mix.astype(BF16), wout_ref[...], preferred_element_type=F32)
    x1 =_stream_tile(xl_ref, xc_ref, n_lat_tiles) + mod_ref[:, 2 * D_MODEL:3 * D_MODEL] * _rms(m2, ng_ref[1:2])
    h = _rms(x1, ng_ref[2:3]) * (1.0 + mod_ref[:, 4 * D_MODEL:5 * D_MODEL]) \
        + mod_ref[:, 3 * D_MODEL:4 * D_MODEL]
    hb = h.astype(BF16)
    o = b2_ref[...]
    for c0 in range(0, D_FF, FF_CHUNK):
        f = jnp.dot(hb, w1_ref[:, c0:c0 + FF_CHUNK], preferred_element_type=F32) + b1_ref[:, c0:c0 + FF_CHUNK]
        f = jnp.square(jnp.maximum(f, 0.0))
        o = o + jnp.dot(f.astype(BF16), w2_ref[c0:c0 + FF_CHUNK, :], preferred_element_type=F32)
    o_ref[...] = x1 + mod_ref[:, 5 * D_MODEL:6 * D_MODEL] * _rms(o, ng_ref[3:4])


def _tail(layer, xl, xc, o_n, yg, u, gates, modsel, ssm_d, w_glu, b_glu, w_br_s, w_br_a, w_out, norm_g,
          w1, b1, w2, b2, n_lat_tiles, nt):
    b = xl.shape[0]
    nt_all = n_lat_tiles + 1
    row = lambda bi, i: (bi, i, 0)
    return pl.pallas_call(
        functools.partial(_tail_kernel, n_lat_tiles=n_lat_tiles),
        grid=(b, nt),
        in_specs=_stream_specs(n_lat_tiles) + [
            pl.BlockSpec((None, TM, ATTN_W), row),
            pl.BlockSpec((SSM_GROUPS, TM // S5_T, S5_TC), lambda bi, i: (0, bi * nt_all + i, 0)),
            pl.BlockSpec((None, TM, SSM_W), row),
            pl.BlockSpec((None, TM, 2 * D_MODEL), row),
            _mod_spec(layer, n_lat_tiles),
            _layer_spec(layer, (1, SSM_W)),
            _layer_spec(layer, (SSM_W, SSM_W)),
            _layer_spec(layer, (1, SSM_W)),
            _layer_spec(layer, (SSM_W, D_MODEL)),
            _layer_spec(layer, (ATTN_W, D_MODEL)),
            _layer_spec(layer, (D_MODEL, D_MODEL)),
            _layer_spec(layer, (4, D_MODEL)),
            _layer_spec(layer, (D_MODEL, D_FF)),
            _layer_spec(layer, (1, D_FF)),
            _layer_spec(layer, (D_FF, D_MODEL)),
            _layer_spec(layer, (1, D_MODEL))],
        out_specs=pl.BlockSpec((None, TM, D_MODEL), row),
        out_shape=jax.ShapeDtypeStruct((b, nt * TM, D_MODEL), F32),
        scratch_shapes=[pltpu.VMEM((S5_SLABS, TM, LANES), F32)],
        compiler_params=_cparams(("parallel", "arbitrary")),
        name="merge_mlp",
    )(xl, xc, o_n, yg, u, gates, modsel, ssm_d, w_glu, b_glu, w_br_s, w_br_a, w_out, norm_g,
      w1, b1, w2, b2)


def _rope_tables(n_tokens):
    rows = n_tokens // GRID_W
    row = jnp.repeat(jnp.arange(rows, dtype=jnp.int32), GRID_W).astype(F32)
    col = jnp.tile(jnp.arange(GRID_W, dtype=jnp.int32), rows).astype(F32)
    inv_freq = ROPE_BASE ** (-jnp.arange(N_FREQ, dtype=F32) / N_FREQ)
    ang = jnp.stack([row[:, None] * inv_freq, col[:, None] * inv_freq], axis=1)
    cos, sin = jnp.cos(ang), jnp.sin(ang)
    cos_l = jnp.tile(jnp.concatenate([cos[:, 0], cos[:, 0], cos[:, 1], cos[:, 1]], axis=1), (1, 2))
    sin_l = jnp.tile(jnp.concatenate([-sin[:, 0], sin[:, 0], -sin[:, 1], sin[:, 1]], axis=1), (1, 2))
    cos_t = jnp.concatenate([cos_l, jnp.ones((TM, LANES), F32)], axis=0)
    sin_t = jnp.concatenate([sin_l, jnp.zeros((TM, LANES), F32)], axis=0)
    return cos_t, sin_t


def kernel(x, c, ctx, c_ctx, ada_w, ada_b, norm_g, w_in, gate_b, lam_qk, subln_g, w_br_a,
           ssm_a_re, ssm_a_im, ssm_b_re, ssm_b_im, ssm_c_re, ssm_c_im, ssm_log_dt, ssm_d,
           w_glu, b_glu, w_br_s, w_out, w_mlp1, b_mlp1, w_mlp2, b_mlp2):
    b, seq, _ = x.shape
    nvalid = seq + CTX_LEN
    assert b + 1 <= 8 and ctx.shape[1] == CTX_LEN and seq % TM == 0 and seq % TQ == 0 and nvalid % TK == 0
    n_lat_tiles = seq // TM
    cos_t, sin_t = _rope_tables(seq)
    xl, xc = x, jnp.concatenate([ctx, jnp.zeros((b, ROWS_PAD, D_MODEL), F32)], axis=1)
    cc = jnp.zeros((8, D_MODEL), F32).at[:b].set(c).at[b].set(c_ctx)
    mod = _modulation(cc, ada_w, ada_b)
    modsel = jnp.stack([mod[:, :b], jnp.broadcast_to(mod[:, b:b + 1], (DEPTH, b, 6 * D_MODEL))], axis=2)
    modsel = modsel.reshape(DEPTH, b, 2, 1, 6 * D_MODEL)
    s5_params = _s5_params(ssm_a_re, ssm_a_im, ssm_log_dt, ssm_b_re, ssm_b_im, ssm_c_re, ssm_c_im)
    row3 = lambda a: a.reshape(DEPTH, 1, -1)
    w_in_b, w_glu_b, w_br_s_b, w_br_a_b, w_out_b, w1_b, w2_b = (
        w.astype(BF16) for w in (w_in, w_glu, w_br_s, w_br_a, w_out, w_mlp1, w_mlp2))
    for i in range(DEPTH):
        last = i == DEPTH - 1
        lam_init = 0.8 - 0.6 * math.exp(-0.3 * i)
        q, k, v, u, gates, xg, nrm = _in_proj(i, xl, xc, modsel, norm_g, w_in_b, row3(gate_b),
                                              cos_t, sin_t, n_lat_tiles)
        o_n = _attention_layer(i, q, k, v, nrm, lam_qk, row3(subln_g), lam_init, seq, with_ctx=not last)
        yg = _s5(i, xg, s5_params, b, nvalid)
        xl = _tail(i, xl, xc, o_n, yg, u, gates, modsel, row3(ssm_d), w_glu_b, row3(b_glu), w_br_s_b,
                   w_br_a_b, w_out_b, norm_g, w1_b, row3(b_mlp1), w2_b, row3(b_mlp2),
                   n_lat_tiles, n_lat_tiles if last else n_lat_tiles + 1)
        if not last:
            xc = xl[:, seq:]
    return xl
```
